```python
import jax, jax.numpy as jnp
from jax import lax
import numpy as np

D_MODEL = 2048
BATCH = 8
SEQ = 4096
DEPTH = 2

CHUNK = 64
D_RNN = 2048
LRU_BLOCKS = 8
LRU_BLOCK_W = D_RNN // LRU_BLOCKS
LRU_C = 8.0
CONV_A_WIDTH = 4
D_CONV = 2048
CONV_B_WIDTH = 3
D_FF = 4 * D_MODEL
EPS = 1e-6

SPLIT_SIZES = (D_RNN, D_RNN, D_CONV, D_CONV, D_CONV, D_MODEL, D_MODEL)
SPLIT_POINTS = tuple(int(v) for v in np.cumsum(SPLIT_SIZES)[:-1])
N_IN = int(sum(SPLIT_SIZES))

kernel_name = "hybrid_rglru_shortconv_gated_trunk"


def _rmsnorm(x, g):
    xf = x.astype(jnp.float32)
    y = xf * lax.rsqrt(jnp.mean(xf * xf, axis=-1, keepdims=True) + EPS)
    return (y * g.astype(jnp.float32)).astype(x.dtype)


def _causal_dwconv(x, w):
    k, c = w.shape
    return lax.conv_general_dilated(
        x, w[:, None, :].astype(x.dtype), window_strides=(1,), padding=[(k - 1, 0)],
        dimension_numbers=("NWC", "WIO", "NWC"), feature_group_count=c)


def _lru_combine(left, right):
    a_l, b_l = left
    a_r, b_r = right
    return a_l * a_r, a_r * b_l + b_r


def _rg_lru(x, wr, br, wi, bi, lam):
    bsz, s, c = x.shape
    xb = x.reshape(bsz, s, LRU_BLOCKS, LRU_BLOCK_W)
    r = jax.nn.sigmoid(jnp.einsum("bsnh,nhk->bsnk", xb, wr) + br).reshape(bsz, s, c)
    i = jax.nn.sigmoid(jnp.einsum("bsnh,nhk->bsnk", xb, wi) + bi).reshape(bsz, s, c)
    log_a = -LRU_C * r.astype(jnp.float32) * jax.nn.softplus(-lam.astype(jnp.float32))
    a = jnp.exp(log_a)
    mult = jnp.sqrt(-jnp.expm1(2.0 * log_a))
    b = mult * (i * x).astype(jnp.float32)
    _, h = lax.associative_scan(_lru_combine, (a, b), axis=1)
    return h.astype(x.dtype)


def _layer(x, g1, w_in, b_in, conv_a_w, conv_a_b, lru_wr, lru_br, lru_wi, lru_bi, lru_lam,
           conv_b_w, w_pa, w_pb, w_o, g2, w_mlp1, w_mlp2):
    h = _rmsnorm(x, g1)
    z = jnp.einsum("bsd,dn->bsn", h, w_in) + b_in
    xa, ya, cb, cc, cx, ga, gb = jnp.split(z, SPLIT_POINTS, axis=-1)
    xa = _causal_dwconv(xa, conv_a_w) + conv_a_b
    xa = _rg_lru(xa, lru_wr, lru_br, lru_wi, lru_bi, lru_lam)
    out_a = jnp.einsum("bsc,cd->bsd", xa * jax.nn.gelu(ya), w_pa)
    out_b = jnp.einsum("bsc,cd->bsd", cb * _causal_dwconv(cc * cx, conv_b_w), w_pb)
    merged = jax.nn.sigmoid(ga) * out_a + jax.nn.sigmoid(gb) * out_b
    x = x + jnp.einsum("bsd,de->bse", merged, w_o)
    h2 = _rmsnorm(x, g2)
    u = jnp.square(jax.nn.relu(jnp.einsum("bsd,df->bsf", h2, w_mlp1)))
    return x + jnp.einsum("bsf,fd->bsd", u, w_mlp2)


def _fwd_setup_inputs(seed: int = 0) -> dict:
    key = jax.random.key(seed)
    ks = jax.random.split(key, 24)
    f32 = jnp.float32
    L = DEPTH

    def nrm(k, shape, scale):
        return jax.random.normal(k, shape, f32) * scale

    u = jax.random.uniform(ks[10], (L, D_RNN), f32, minval=0.9, maxval=0.999)
    p = u ** (1.0 / LRU_C)
    lru_lam = jnp.log(p) - jnp.log1p(-p)
    return {
        "x": nrm(ks[0], (BATCH, SEQ, D_MODEL), 1.0),
        "norm1_g": 1.0 + nrm(ks[1], (L, D_MODEL), 0.02),
        "w_in": nrm(ks[2], (L, D_MODEL, N_IN), D_MODEL ** -0.5),
        "b_in": nrm(ks[3], (L, N_IN), 0.02),
        "conv_a_w": nrm(ks[4], (L, CONV_A_WIDTH, D_RNN), CONV_A_WIDTH ** -0.5),
        "conv_a_b": nrm(ks[5], (L, D_RNN), 0.02),
        "lru_wr": nrm(ks[6], (L, LRU_BLOCKS, LRU_BLOCK_W, LRU_BLOCK_W), LRU_BLOCK_W ** -0.5),
        "lru_br": nrm(ks[7], (L, LRU_BLOCKS, LRU_BLOCK_W), 0.02),
        "lru_wi": nrm(ks[8], (L, LRU_BLOCKS, LRU_BLOCK_W, LRU_BLOCK_W), LRU_BLOCK_W ** -0.5),
        "lru_bi": nrm(ks[9], (L, LRU_BLOCKS, LRU_BLOCK_W), 0.02),
        "lru_lam": lru_lam,
        "conv_b_w": nrm(ks[11], (L, CONV_B_WIDTH, D_CONV), CONV_B_WIDTH ** -0.5),
        "w_pa": nrm(ks[12], (L, D_RNN, D_MODEL), D_RNN ** -0.5),
        "w_pb": nrm(ks[13], (L, D_CONV, D_MODEL), D_CONV ** -0.5),
        "w_o": nrm(ks[14], (L, D_MODEL, D_MODEL), D_MODEL ** -0.5),
        "norm2_g": 1.0 + nrm(ks[15], (L, D_MODEL), 0.02),
        "w_mlp1": nrm(ks[16], (L, D_MODEL, D_FF), D_MODEL ** -0.5),
        "w_mlp2": nrm(ks[17], (L, D_FF, D_MODEL), D_FF ** -0.5),
        "final_g": 1.0 + nrm(ks[18], (D_MODEL,), 0.02),
    }


def _fwd_reference(x, norm1_g, w_in, b_in, conv_a_w, conv_a_b, lru_wr, lru_br, lru_wi, lru_bi,
              lru_lam, conv_b_w, w_pa, w_pb, w_o, norm2_g, w_mlp1, w_mlp2, final_g):
    for l in range(DEPTH):
        x = _layer(x, norm1_g[l], w_in[l], b_in[l], conv_a_w[l], conv_a_b[l], lru_wr[l], lru_br[l],
                   lru_wi[l], lru_bi[l], lru_lam[l], conv_b_w[l], w_pa[l], w_pb[l], w_o[l],
                   norm2_g[l], w_mlp1[l], w_mlp2[l])
    return _rmsnorm(x, final_g)


import jax as _jax
import jax.numpy as _jnp

TWIN_FORMAT = 'train_step'
FWD_PARAMS = ['x', 'norm1_g', 'w_in', 'b_in', 'conv_a_w', 'conv_a_b', 'lru_wr', 'lru_br', 'lru_wi', 'lru_bi', 'lru_lam', 'conv_b_w', 'w_pa', 'w_pb', 'w_o', 'norm2_g', 'w_mlp1', 'w_mlp2', 'final_g']
TWIN_WEIGHTS = ['norm1_g', 'w_in', 'b_in', 'conv_a_w', 'conv_a_b', 'lru_wr', 'lru_br', 'lru_wi', 'lru_bi', 'lru_lam', 'conv_b_w', 'w_pa', 'w_pb', 'w_o', 'norm2_g', 'w_mlp1', 'w_mlp2', 'final_g']
TWIN_DIFF_INPUT = 'x'
TWIN_INPUTS = ['x', 'norm1_g', 'w_in', 'b_in', 'conv_a_w', 'conv_a_b', 'lru_wr', 'lru_br', 'lru_wi', 'lru_bi', 'lru_lam', 'conv_b_w', 'w_pa', 'w_pb', 'w_o', 'norm2_g', 'w_mlp1', 'w_mlp2', 'final_g', 'loss_target', 'm_norm1_g', 'm_w_in', 'm_b_in', 'm_conv_a_w', 'm_conv_a_b', 'm_lru_wr', 'm_lru_br', 'm_lru_wi', 'm_lru_bi', 'm_lru_lam', 'm_conv_b_w', 'm_w_pa', 'm_w_pb', 'm_w_o', 'm_norm2_g', 'm_w_mlp1', 'm_w_mlp2', 'm_final_g', 'v_norm1_g', 'v_w_in', 'v_b_in', 'v_conv_a_w', 'v_conv_a_b', 'v_lru_wr', 'v_lru_br', 'v_lru_wi', 'v_lru_bi', 'v_lru_lam', 'v_conv_b_w', 'v_w_pa', 'v_w_pb', 'v_w_o', 'v_norm2_g', 'v_w_mlp1', 'v_w_mlp2', 'v_final_g']
TWIN_OUTPUTS = ['loss', 'grad_x', 'grad_norm1_g', 'grad_w_in', 'grad_b_in', 'grad_conv_a_w', 'grad_conv_a_b', 'grad_lru_wr', 'grad_lru_br', 'grad_lru_wi', 'grad_lru_bi', 'grad_lru_lam', 'grad_conv_b_w', 'grad_w_pa', 'grad_w_pb', 'grad_w_o', 'grad_norm2_g', 'grad_w_mlp1', 'grad_w_mlp2', 'grad_final_g', 'delta_norm1_g', 'delta_w_in', 'delta_b_in', 'delta_conv_a_w', 'delta_conv_a_b', 'delta_lru_wr', 'delta_lru_br', 'delta_lru_wi', 'delta_lru_bi', 'delta_lru_lam', 'delta_conv_b_w', 'delta_w_pa', 'delta_w_pb', 'delta_w_o', 'delta_norm2_g', 'delta_w_mlp1', 'delta_w_mlp2', 'delta_final_g', 'new_m_norm1_g', 'new_m_w_in', 'new_m_b_in', 'new_m_conv_a_w', 'new_m_conv_a_b', 'new_m_lru_wr', 'new_m_lru_br', 'new_m_lru_wi', 'new_m_lru_bi', 'new_m_lru_lam', 'new_m_conv_b_w', 'new_m_w_pa', 'new_m_w_pb', 'new_m_w_o', 'new_m_norm2_g', 'new_m_w_mlp1', 'new_m_w_mlp2', 'new_m_final_g', 'new_v_norm1_g', 'new_v_w_in', 'new_v_b_in', 'new_v_conv_a_w', 'new_v_conv_a_b', 'new_v_lru_wr', 'new_v_lru_br', 'new_v_lru_wi', 'new_v_lru_bi', 'new_v_lru_lam', 'new_v_conv_b_w', 'new_v_w_pa', 'new_v_w_pb', 'new_v_w_o', 'new_v_norm2_g', 'new_v_w_mlp1', 'new_v_w_mlp2', 'new_v_final_g']
TWIN_LEAF_KINDS = {'loss': 'loss', 'grad_x': 'grad_x', 'grad_norm1_g': 'grad_w', 'grad_w_in': 'grad_w', 'grad_b_in': 'grad_w', 'grad_conv_a_w': 'grad_w', 'grad_conv_a_b': 'grad_w', 'grad_lru_wr': 'grad_w', 'grad_lru_br': 'grad_w', 'grad_lru_wi': 'grad_w', 'grad_lru_bi': 'grad_w', 'grad_lru_lam': 'grad_w', 'grad_conv_b_w': 'grad_w', 'grad_w_pa': 'grad_w', 'grad_w_pb': 'grad_w', 'grad_w_o': 'grad_w', 'grad_norm2_g': 'grad_w', 'grad_w_mlp1': 'grad_w', 'grad_w_mlp2': 'grad_w', 'grad_final_g': 'grad_w', 'delta_norm1_g': 'delta_w', 'delta_w_in': 'delta_w', 'delta_b_in': 'delta_w', 'delta_conv_a_w': 'delta_w', 'delta_conv_a_b': 'delta_w', 'delta_lru_wr': 'delta_w', 'delta_lru_br': 'delta_w', 'delta_lru_wi': 'delta_w', 'delta_lru_bi': 'delta_w', 'delta_lru_lam': 'delta_w', 'delta_conv_b_w': 'delta_w', 'delta_w_pa': 'delta_w', 'delta_w_pb': 'delta_w', 'delta_w_o': 'delta_w', 'delta_norm2_g': 'delta_w', 'delta_w_mlp1': 'delta_w', 'delta_w_mlp2': 'delta_w', 'delta_final_g': 'delta_w', 'new_m_norm1_g': 'new_m', 'new_m_w_in': 'new_m', 'new_m_b_in': 'new_m', 'new_m_conv_a_w': 'new_m', 'new_m_conv_a_b': 'new_m', 'new_m_lru_wr': 'new_m', 'new_m_lru_br': 'new_m', 'new_m_lru_wi': 'new_m', 'new_m_lru_bi': 'new_m', 'new_m_lru_lam': 'new_m', 'new_m_conv_b_w': 'new_m', 'new_m_w_pa': 'new_m', 'new_m_w_pb': 'new_m', 'new_m_w_o': 'new_m', 'new_m_norm2_g': 'new_m', 'new_m_w_mlp1': 'new_m', 'new_m_w_mlp2': 'new_m', 'new_m_final_g': 'new_m', 'new_v_norm1_g': 'new_v', 'new_v_w_in': 'new_v', 'new_v_b_in': 'new_v', 'new_v_conv_a_w': 'new_v', 'new_v_conv_a_b': 'new_v', 'new_v_lru_wr': 'new_v', 'new_v_lru_br': 'new_v', 'new_v_lru_wi': 'new_v', 'new_v_lru_bi': 'new_v', 'new_v_lru_lam': 'new_v', 'new_v_conv_b_w': 'new_v', 'new_v_w_pa': 'new_v', 'new_v_w_pb': 'new_v', 'new_v_w_o': 'new_v', 'new_v_norm2_g': 'new_v', 'new_v_w_mlp1': 'new_v', 'new_v_w_mlp2': 'new_v', 'new_v_final_g': 'new_v'}


def _forward(args):
    return _fwd_reference(*[args[k] for k in FWD_PARAMS])


def _output_shape():
    def fwd():
        inp = _fwd_setup_inputs(0)
        return _fwd_reference(*[inp[k] for k in FWD_PARAMS])
    out = _jax.eval_shape(fwd)
    return out.shape, out.dtype

N_MICROBATCH = 1
ADAM_LR = 0.001
ADAM_B1 = 0.9
ADAM_B2 = 0.999
ADAM_EPS = 1e-08
ADAM_WD = 0.01
ADAM_STEP = 10
PER_EXAMPLE_BATCH_AXIS = {'x': 0, 'loss_target': 0}
SHARED_INPUTS = []
_WEIGHT_DTYPES = {'norm1_g': _jnp.float32, 'w_in': _jnp.float32, 'b_in': _jnp.float32, 'conv_a_w': _jnp.float32, 'conv_a_b': _jnp.float32, 'lru_wr': _jnp.float32, 'lru_br': _jnp.float32, 'lru_wi': _jnp.float32, 'lru_bi': _jnp.float32, 'lru_lam': _jnp.float32, 'conv_b_w': _jnp.float32, 'w_pa': _jnp.float32, 'w_pb': _jnp.float32, 'w_o': _jnp.float32, 'norm2_g': _jnp.float32, 'w_mlp1': _jnp.float32, 'w_mlp2': _jnp.float32, 'final_g': _jnp.float32}
MOMENT_SCALE = {'norm1_g': 8.608775e-02, 'w_in': 3.206626e-02, 'b_in': 8.157619e-02, 'conv_a_w': 3.200361e-02, 'conv_a_b': 2.064142e-01, 'lru_wr': 4.940636e-03, 'lru_br': 6.430960e-03, 'lru_wi': 8.973042e-03, 'lru_bi': 1.164262e-02, 'lru_lam': 1.513426e-02, 'conv_b_w': 4.182509e-02, 'w_pa': 3.314134e-02, 'w_pb': 4.167518e-02, 'w_o': 5.161679e-02, 'norm2_g': 7.405516e-02, 'w_mlp1': 3.672900e-02, 'w_mlp2': 7.751052e-02, 'final_g': 1.627604e+01}


def _to_microbatches(a, axis):
    t = _jnp.moveaxis(a, axis, 0)
    t = t.reshape((N_MICROBATCH, t.shape[0] // N_MICROBATCH) + t.shape[1:])
    return _jnp.moveaxis(t, 1, axis + 1)


def setup_inputs(seed: int = 0) -> dict:
    inp = _fwd_setup_inputs(seed)
    key = _jax.random.fold_in(_jax.random.key(seed), 7919)
    shape, _ = _output_shape()
    out = dict(inp)
    out["loss_target"] = _jax.random.normal(_jax.random.fold_in(key, 0), shape, _jnp.float32)
    for i, name in enumerate(TWIN_WEIGHTS):
        w = inp[name].astype(_jnp.float32)
        if MOMENT_SCALE is None:
            s = _jnp.sqrt(_jnp.mean(_jnp.square(w)) + 1e-30)
        else:
            s = MOMENT_SCALE[name]
        km, kv = _jax.random.split(_jax.random.fold_in(key, i + 1))
        out[name] = w
        out["m_" + name] = s * _jax.random.normal(km, w.shape, _jnp.float32)
        out["v_" + name] = (s * s) * _jax.random.uniform(kv, w.shape, _jnp.float32, 0.5, 1.5)
    if N_MICROBATCH > 1:
        for name, axis in PER_EXAMPLE_BATCH_AXIS.items():
            out[name] = _to_microbatches(out[name], axis)
    return {'x': out['x'], 'norm1_g': out['norm1_g'], 'w_in': out['w_in'], 'b_in': out['b_in'], 'conv_a_w': out['conv_a_w'], 'conv_a_b': out['conv_a_b'], 'lru_wr': out['lru_wr'], 'lru_br': out['lru_br'], 'lru_wi': out['lru_wi'], 'lru_bi': out['lru_bi'], 'lru_lam': out['lru_lam'], 'conv_b_w': out['conv_b_w'], 'w_pa': out['w_pa'], 'w_pb': out['w_pb'], 'w_o': out['w_o'], 'norm2_g': out['norm2_g'], 'w_mlp1': out['w_mlp1'], 'w_mlp2': out['w_mlp2'], 'final_g': out['final_g'], 'loss_target': out['loss_target'], 'm_norm1_g': out['m_norm1_g'], 'm_w_in': out['m_w_in'], 'm_b_in': out['m_b_in'], 'm_conv_a_w': out['m_conv_a_w'], 'm_conv_a_b': out['m_conv_a_b'], 'm_lru_wr': out['m_lru_wr'], 'm_lru_br': out['m_lru_br'], 'm_lru_wi': out['m_lru_wi'], 'm_lru_bi': out['m_lru_bi'], 'm_lru_lam': out['m_lru_lam'], 'm_conv_b_w': out['m_conv_b_w'], 'm_w_pa': out['m_w_pa'], 'm_w_pb': out['m_w_pb'], 'm_w_o': out['m_w_o'], 'm_norm2_g': out['m_norm2_g'], 'm_w_mlp1': out['m_w_mlp1'], 'm_w_mlp2': out['m_w_mlp2'], 'm_final_g': out['m_final_g'], 'v_norm1_g': out['v_norm1_g'], 'v_w_in': out['v_w_in'], 'v_b_in': out['v_b_in'], 'v_conv_a_w': out['v_conv_a_w'], 'v_conv_a_b': out['v_conv_a_b'], 'v_lru_wr': out['v_lru_wr'], 'v_lru_br': out['v_lru_br'], 'v_lru_wi': out['v_lru_wi'], 'v_lru_bi': out['v_lru_bi'], 'v_lru_lam': out['v_lru_lam'], 'v_conv_b_w': out['v_conv_b_w'], 'v_w_pa': out['v_w_pa'], 'v_w_pb': out['v_w_pb'], 'v_w_o': out['v_w_o'], 'v_norm2_g': out['v_norm2_g'], 'v_w_mlp1': out['v_w_mlp1'], 'v_w_mlp2': out['v_w_mlp2'], 'v_final_g': out['v_final_g']}


def _loss(weights, diff, rest, loss_target):
    with _jax.named_scope("forward"):
        args = {**rest, TWIN_DIFF_INPUT: diff, **{k: w.astype(_WEIGHT_DTYPES[k]) for k, w in weights.items()}}
        y = _forward(args)
    with _jax.named_scope("loss_head"):
        err = _jnp.square(y.astype(_jnp.float32) - loss_target)
        return 0.5 * _jnp.sum(_jnp.mean(err, axis=-1)) if err.ndim else 0.5 * err


def _adamw(w, g, m, v):
    m = ADAM_B1 * m + (1.0 - ADAM_B1) * g
    v = ADAM_B2 * v + (1.0 - ADAM_B2) * _jnp.square(g)
    m_hat = m / (1.0 - ADAM_B1 ** ADAM_STEP)
    v_hat = v / (1.0 - ADAM_B2 ** ADAM_STEP)
    delta = -ADAM_LR * (m_hat / (_jnp.sqrt(v_hat) + ADAM_EPS) + ADAM_WD * w)
    return delta, m, v


def reference(x, norm1_g, w_in, b_in, conv_a_w, conv_a_b, lru_wr, lru_br, lru_wi, lru_bi, lru_lam, conv_b_w, w_pa, w_pb, w_o, norm2_g, w_mlp1, w_mlp2, final_g, loss_target, m_norm1_g, m_w_in, m_b_in, m_conv_a_w, m_conv_a_b, m_lru_wr, m_lru_br, m_lru_wi, m_lru_bi, m_lru_lam, m_conv_b_w, m_w_pa, m_w_pb, m_w_o, m_norm2_g, m_w_mlp1, m_w_mlp2, m_final_g, v_norm1_g, v_w_in, v_b_in, v_conv_a_w, v_conv_a_b, v_lru_wr, v_lru_br, v_lru_wi, v_lru_bi, v_lru_lam, v_conv_b_w, v_w_pa, v_w_pb, v_w_o, v_norm2_g, v_w_mlp1, v_w_mlp2, v_final_g):
    given = dict(x=x, norm1_g=norm1_g, w_in=w_in, b_in=b_in, conv_a_w=conv_a_w, conv_a_b=conv_a_b, lru_wr=lru_wr, lru_br=lru_br, lru_wi=lru_wi, lru_bi=lru_bi, lru_lam=lru_lam, conv_b_w=conv_b_w, w_pa=w_pa, w_pb=w_pb, w_o=w_o, norm2_g=norm2_g, w_mlp1=w_mlp1, w_mlp2=w_mlp2, final_g=final_g, loss_target=loss_target, m_norm1_g=m_norm1_g, m_w_in=m_w_in, m_b_in=m_b_in, m_conv_a_w=m_conv_a_w, m_conv_a_b=m_conv_a_b, m_lru_wr=m_lru_wr, m_lru_br=m_lru_br, m_lru_wi=m_lru_wi, m_lru_bi=m_lru_bi, m_lru_lam=m_lru_lam, m_conv_b_w=m_conv_b_w, m_w_pa=m_w_pa, m_w_pb=m_w_pb, m_w_o=m_w_o, m_norm2_g=m_norm2_g, m_w_mlp1=m_w_mlp1, m_w_mlp2=m_w_mlp2, m_final_g=m_final_g, v_norm1_g=v_norm1_g, v_w_in=v_w_in, v_b_in=v_b_in, v_conv_a_w=v_conv_a_w, v_conv_a_b=v_conv_a_b, v_lru_wr=v_lru_wr, v_lru_br=v_lru_br, v_lru_wi=v_lru_wi, v_lru_bi=v_lru_bi, v_lru_lam=v_lru_lam, v_conv_b_w=v_conv_b_w, v_w_pa=v_w_pa, v_w_pb=v_w_pb, v_w_o=v_w_o, v_norm2_g=v_norm2_g, v_w_mlp1=v_w_mlp1, v_w_mlp2=v_w_mlp2, v_final_g=v_final_g)
    weights = {n: given[n] for n in TWIN_WEIGHTS}
    shared = {n: given[n] for n in SHARED_INPUTS}
    per_example = {n: given[n] for n in ['x']}
    grad_fn = _jax.value_and_grad(_loss, argnums=(0, 1))

    def one_microbatch(ex, loss_target):
        ex = dict(ex)
        diff = ex.pop(TWIN_DIFF_INPUT)
        return grad_fn(weights, diff, {**shared, **ex}, loss_target)

    if N_MICROBATCH == 1:
        loss, (grad_w, grad_x) = one_microbatch(per_example, given["loss_target"])
    else:
        def body(carry, xs):
            loss_sum, grad_sum = carry
            l_k, (gw_k, gx_k) = one_microbatch(xs[0], xs[1])
            with _jax.named_scope("update"):
                return (loss_sum + l_k, _jax.tree.map(_jnp.add, grad_sum, gw_k)), gx_k

        init = (_jnp.zeros((), _jnp.float32), _jax.tree.map(_jnp.zeros_like, weights))
        (loss, grad_w), grad_x = _jax.lax.scan(body, init, (per_example, given["loss_target"]))
    with _jax.named_scope("update"):
        delta_w, new_m, new_v = {}, {}, {}
        for n in TWIN_WEIGHTS:
            delta_w[n], new_m[n], new_v[n] = _adamw(weights[n], grad_w[n], given["m_" + n], given["v_" + n])
    return (loss, grad_x, *[grad_w[n] for n in TWIN_WEIGHTS], *[delta_w[n] for n in TWIN_WEIGHTS],
            *[new_m[n] for n in TWIN_WEIGHTS], *[new_v[n] for n in TWIN_WEIGHTS])
```

```python
import functools
import math

import jax
import jax.numpy as jnp
from jax import lax
from jax.experimental import pallas as pl
from jax.experimental.pallas import tpu as pltpu

F32 = jnp.float32
BF16 = jnp.bfloat16

N_DEV = 8
DEPTH = 2
N_SPLIT = 7
N_BRANCH = 5
EPS = 1e-6
LRU_C = 8.0
CONV_A_WIDTH = 4
CONV_B_WIDTH = 3
GELU_C = math.sqrt(2.0 / math.pi)
GELU_A = 0.044715

ADAM_LR = 0.001
ADAM_B1 = 0.9
ADAM_B2 = 0.999
ADAM_EPS = 1e-08
ADAM_WD = 0.01
ADAM_STEP = 10

LANES = 128
SUBLANES = 8
VMEM_LIMIT_CAP = 56 * 2 ** 20
MESH = pl.DeviceIdType.MESH
ANY = pl.BlockSpec(memory_space=pl.ANY)


def _tile(n, target, align=LANES):
    if n <= target:
        return n
    t = (target // align) * align
    while t >= align:
        if n % t == 0:
            return t
        t -= align
    return n


def _nbytes(shape, dtype):
    return math.prod(shape) * jnp.dtype(dtype).itemsize


def _vmem_limit(block_bytes, scratch_bytes=0):
    est = 2 * block_bytes + scratch_bytes
    return int(min(max(2 * est, 32 * 2 ** 20), VMEM_LIMIT_CAP))


def _my_coords():
    return lax.axis_index("x"), lax.axis_index("y"), lax.axis_index("c")


def _flat(x, y, c):
    return 4 * x + 2 * y + c


def _peer(k, x, y, c):
    return (1 - x if k & 4 else x, 1 - y if k & 2 else y, 1 - c if k & 1 else c)


def _slab(ref, axis, start, size):
    idx = tuple(pl.ds(start, size) if d == axis else slice(None) for d in range(len(ref.shape)))
    return ref.at[idx]


def all_gather(name, shards, axes):
    n = len(shards)
    sizes = [s.shape[a] for s, a in zip(shards, axes)]
    out_shape = []
    for s, a in zip(shards, axes):
        full = list(s.shape)
        full[a] *= N_DEV
        out_shape.append(jax.ShapeDtypeStruct(tuple(full), s.dtype))

    def body(*refs):
        srcs, outs = refs[:n], refs[n:2 * n]
        send_sems, recv_sems, local_sems = refs[2 * n:]
        x, y, c = _my_coords()
        me = _flat(x, y, c)
        local = []
        for i in range(n):
            cp = pltpu.make_async_copy(srcs[i], _slab(outs[i], axes[i], me * sizes[i], sizes[i]), local_sems.at[i])
            cp.start()
            local.append(cp)
        remote = []
        for k in range(1, N_DEV):
            for i in range(n):
                mine = _slab(outs[i], axes[i], me * sizes[i], sizes[i])
                cp = pltpu.make_async_remote_copy(
                    src_ref=srcs[i], dst_ref=mine,
                    send_sem=send_sems.at[(k - 1) * n + i], recv_sem=recv_sems.at[(k - 1) * n + i],
                    device_id=_peer(k, x, y, c), device_id_type=MESH)
                cp.start()
                remote.append(cp)
        for cp in remote:
            cp.wait()
        for cp in local:
            cp.wait()

    return pl.pallas_call(
        body, name=name, out_shape=out_shape,
        in_specs=[ANY] * n, out_specs=[ANY] * n,
        scratch_shapes=[pltpu.SemaphoreType.DMA(((N_DEV - 1) * n,)),
                        pltpu.SemaphoreType.DMA(((N_DEV - 1) * n,)),
                        pltpu.SemaphoreType.DMA((n,))],
    )(*shards)


def exchange_grads(name, grads, axes, layer, prev=None):
    n = len(grads)
    sizes = [g.shape[a] // N_DEV for g, a in zip(grads, axes)]
    out_shape = []
    for g, a, sz in zip(grads, axes, sizes):
        shard = list(g.shape)
        shard[a] = sz
        out_shape.append(jax.ShapeDtypeStruct((N_DEV, DEPTH, *shard), g.dtype))
    n_in = n if prev is None else 2 * n

    def body(*refs):
        srcs, outs = refs[:n], refs[n_in:n_in + n]
        send_sems, recv_sems, local_sems = refs[n_in + n:]
        x, y, c = _my_coords()
        me = _flat(x, y, c)
        local = []
        for i in range(n):
            cp = pltpu.make_async_copy(_slab(srcs[i], axes[i], me * sizes[i], sizes[i]), outs[i].at[me, layer],
                                       local_sems.at[i])
            cp.start()
            local.append(cp)
        remote = []
        for k in range(1, N_DEV):
            px, py, pc = _peer(k, x, y, c)
            peer = _flat(px, py, pc)
            for i in range(n):
                cp = pltpu.make_async_remote_copy(
                    src_ref=_slab(srcs[i], axes[i], peer * sizes[i], sizes[i]), dst_ref=outs[i].at[me, layer],
                    send_sem=send_sems.at[(k - 1) * n + i], recv_sem=recv_sems.at[(k - 1) * n + i],
                    device_id=(px, py, pc), device_id_type=MESH)
                cp.start()
                remote.append(cp)
        for cp in remote:
            cp.wait()
        for cp in local:
            cp.wait()

    operands = list(grads) + ([] if prev is None else list(prev))
    aliases = {} if prev is None else {n + i: i for i in range(n)}
    return pl.pallas_call(
        body, name=name, out_shape=out_shape,
        in_specs=[ANY] * n_in, out_specs=[ANY] * n,
        input_output_aliases=aliases,
        scratch_shapes=[pltpu.SemaphoreType.DMA(((N_DEV - 1) * n,)),
                        pltpu.SemaphoreType.DMA(((N_DEV - 1) * n,)),
                        pltpu.SemaphoreType.DMA((n,))],
    )(*operands)


def _mm(name, a, b, *, ta, tb, grid, tm, tn, tk, a_spec, b_spec, outs, extras=(), epilogue=None, alias=None):
    n_extra, n_out = len(extras), len(outs)
    nk = grid[2]
    dn = (((0,) if ta else (1,), (1,) if tb else (0,)), ((), ()))

    def finish(acc, extra_refs, out_refs):
        vals = epilogue(acc, *[r[...] for r in extra_refs]) if epilogue is not None else (acc,)
        for o_ref, v in zip(out_refs, vals):
            o_ref[...] = v.astype(o_ref.dtype)

    def body(*refs):
        a_ref, b_ref = refs[0], refs[1]
        extra_refs = refs[2:2 + n_extra]
        first_out = 2 + n_extra + (1 if alias is not None else 0)
        out_refs = refs[first_out:first_out + n_out]
        part = lax.dot_general(a_ref[...], b_ref[...], dn, preferred_element_type=F32)
        if nk == 1:
            finish(part, extra_refs, out_refs)
            return
        acc_ref = refs[-1]
        k = pl.program_id(2)

        @pl.when(k == 0)
        def _():
            acc_ref[...] = part

        @pl.when(k > 0)
        def _():
            acc_ref[...] += part

        @pl.when(k == nk - 1)
        def _():
            finish(acc_ref[...], extra_refs, out_refs)

    operands = [a, b] + [e for e, _ in extras]
    in_specs = [a_spec, b_spec] + [s for _, s in extras]
    aliases = {}
    if alias is not None:
        aliases = {len(operands): 0}
        operands.append(alias)
        in_specs.append(ANY)
    blocks = _nbytes((tm, tk), a.dtype) + _nbytes((tk, tn), b.dtype)
    blocks += sum(_nbytes((tm, tn), s.dtype) for s, _ in outs)
    blocks += sum(_nbytes((e.shape[-2] if e.shape[-2] < tm else tm, tn), e.dtype) for e, _ in extras)
    scratch = [] if nk == 1 else [pltpu.VMEM((tm, tn), F32)]
    return pl.pallas_call(
        body, name=name, grid=grid,
        out_shape=[s for s, _ in outs], in_specs=in_specs, out_specs=[s for _, s in outs],
        input_output_aliases=aliases, scratch_shapes=scratch,
        compiler_params=pltpu.CompilerParams(
            dimension_semantics=("parallel", "parallel", "arbitrary"),
            vmem_limit_bytes=_vmem_limit(blocks, 0 if nk == 1 else 3 * tm * tn * 4)),
    )(*operands)


def _mm_tiles(m, n, k):
    return _tile(m, 512), _tile(n, 1024), _tile(k, 2048)


def mm_plain(name, a, b, *, ta=False, tb=False, out_dtype=F32, epilogue=None, extras=(), extra_kinds=(),
             out_dtypes=None):
    m = a.shape[1] if ta else a.shape[0]
    k = a.shape[0] if ta else a.shape[1]
    n = b.shape[0] if tb else b.shape[1]
    tm, tn, tk = _mm_tiles(m, n, k)
    grid = (m // tm, n // tn, k // tk)
    a_spec = pl.BlockSpec((tk, tm), lambda i, j, kk: (kk, i)) if ta else pl.BlockSpec((tm, tk), lambda i, j, kk: (i, kk))
    b_spec = pl.BlockSpec((tn, tk), lambda i, j, kk: (j, kk)) if tb else pl.BlockSpec((tk, tn), lambda i, j, kk: (kk, j))
    ex = []
    for e, kind in zip(extras, extra_kinds):
        if kind == "row":
            ex.append((e, pl.BlockSpec((1, tn), lambda i, j, kk: (0, j))))
        else:
            ex.append((e, pl.BlockSpec((tm, tn), lambda i, j, kk: (i, j))))
    out_dtypes = out_dtypes or [out_dtype]
    outs = [(jax.ShapeDtypeStruct((m, n), dt), pl.BlockSpec((tm, tn), lambda i, j, kk: (i, j))) for dt in out_dtypes]
    res = _mm(name, a, b, ta=ta, tb=tb, grid=grid, tm=tm, tn=tn, tk=tk, a_spec=a_spec, b_spec=b_spec,
              outs=outs, extras=ex, epilogue=epilogue)
    return res if len(res) > 1 else res[0]


def mm_in_proj(name, h, w, bias, col0, parts):
    t, d = h.shape
    tm, tn, tk = _mm_tiles(t, d, d)
    per = d // tn
    off = col0 // tn
    grid = (t // tm, parts * per, d // tk)
    outs = [(jax.ShapeDtypeStruct((parts, t, d), F32),
             pl.BlockSpec((None, tm, tn), lambda i, j, kk: (j // per, i, j % per)))]
    ex = [(bias, pl.BlockSpec((1, tn), lambda i, j, kk: (0, j + off)))]
    return _mm(name, h, w, ta=False, tb=False, grid=grid, tm=tm, tn=tn, tk=tk,
               a_spec=pl.BlockSpec((tm, tk), lambda i, j, kk: (i, kk)),
               b_spec=pl.BlockSpec((tk, tn), lambda i, j, kk: (kk, j + off)),
               outs=outs, extras=ex, epilogue=lambda acc, bb: (acc + bb,))[0]


def mm_in_proj_dw(name, h, dz, col0, n_total, prev=None):
    parts, t, d = dz.shape
    tm, tn, tk = _mm_tiles(d, d, t)
    per = d // tn
    off = col0 // tn
    grid = (d // tm, parts * per, t // tk)
    outs = [(jax.ShapeDtypeStruct((d, n_total), BF16), pl.BlockSpec((tm, tn), lambda i, j, kk: (i, j + off)))]
    return _mm(name, h, dz, ta=True, tb=False, grid=grid, tm=tm, tn=tn, tk=tk,
               a_spec=pl.BlockSpec((tk, tm), lambda i, j, kk: (kk, i)),
               b_spec=pl.BlockSpec((None, tk, tn), lambda i, j, kk: (j // per, kk, j % per)),
               outs=outs, alias=prev)[0]


def mm_in_proj_dh(name, dz, w, col0, res=None):
    parts, t, d = dz.shape
    tm, tn, tk = _mm_tiles(t, d, d)
    per = d // tk
    off = col0 // tk
    grid = (t // tm, d // tn, parts * per)
    outs = [(jax.ShapeDtypeStruct((t, d), F32), pl.BlockSpec((tm, tn), lambda i, j, kk: (i, j)))]
    ex, epi = [], None
    if res is not None:
        ex = [(res, pl.BlockSpec((tm, tn), lambda i, j, kk: (i, j)))]
        epi = lambda acc, r: (acc + r,)
    return _mm(name, dz, w, ta=False, tb=True, grid=grid, tm=tm, tn=tn, tk=tk,
               a_spec=pl.BlockSpec((None, tm, tk), lambda i, j, kk: (kk // per, i, kk % per)),
               b_spec=pl.BlockSpec((tn, tk), lambda i, j, kk: (j, kk + off)),
               outs=outs, extras=ex, epilogue=epi)[0]


ROW_TILE = 256


def rmsnorm_fwd(name, x, g):
    t, d = x.shape
    tr = _tile(t, ROW_TILE, SUBLANES)

    def body(x_ref, g_ref, h_ref):
        xv = x_ref[...]
        rstd = lax.rsqrt(jnp.mean(xv * xv, axis=-1, keepdims=True) + EPS)
        h_ref[...] = (xv * rstd * g_ref[...]).astype(BF16)

    return pl.pallas_call(
        body, name=name, grid=(t // tr,),
        out_shape=jax.ShapeDtypeStruct((t, d), BF16),
        in_specs=[pl.BlockSpec((tr, d), lambda i: (i, 0)), pl.BlockSpec((1, d), lambda i: (0, 0))],
        out_specs=pl.BlockSpec((tr, d), lambda i: (i, 0)),
        compiler_params=pltpu.CompilerParams(dimension_semantics=("parallel",)),
    )(x, g)


def _rmsnorm_bwd_math(xv, gv, dout):
    rstd = lax.rsqrt(jnp.mean(xv * xv, axis=-1, keepdims=True) + EPS)
    xhat = xv * rstd
    dy = dout * gv
    dx = rstd * (dy - xhat * jnp.mean(dy * xhat, axis=-1, keepdims=True))
    dg = jnp.sum(dout * xhat, axis=0, keepdims=True)
    return dx, dg


def rmsnorm_bwd(name, dh, x, g, dres):
    t, d = x.shape
    tr = _tile(t, ROW_TILE, SUBLANES)

    def body(dh_ref, x_ref, g_ref, dres_ref, dx_ref, dxb_ref, dg_ref):
        dx, dg = _rmsnorm_bwd_math(x_ref[...], g_ref[...], dh_ref[...])
        dx = dx + dres_ref[...]
        dx_ref[...] = dx
        dxb_ref[...] = dx.astype(BF16)

        @pl.when(pl.program_id(0) == 0)
        def _():
            dg_ref[...] = dg

        @pl.when(pl.program_id(0) > 0)
        def _():
            dg_ref[...] += dg

    row = pl.BlockSpec((tr, d), lambda i: (i, 0))
    vec = pl.BlockSpec((1, d), lambda i: (0, 0))
    return pl.pallas_call(
        body, name=name, grid=(t // tr,),
        out_shape=[jax.ShapeDtypeStruct((t, d), F32), jax.ShapeDtypeStruct((t, d), BF16),
                   jax.ShapeDtypeStruct((1, d), F32)],
        in_specs=[row, row, vec, row], out_specs=[row, row, vec],
        compiler_params=pltpu.CompilerParams(dimension_semantics=("arbitrary",)),
    )(dh, x, g, dres)


def loss_head(name, x, g, target):
    t, d = x.shape
    tr = _tile(t, ROW_TILE, SUBLANES)

    def body(x_ref, g_ref, tgt_ref, dx_ref, dxb_ref, dg_ref, loss_ref):
        xv, gv = x_ref[...], g_ref[...]
        rstd = lax.rsqrt(jnp.mean(xv * xv, axis=-1, keepdims=True) + EPS)
        err = xv * rstd * gv - tgt_ref[...]
        lsum = (0.5 / d) * jnp.sum(err * err, axis=0, keepdims=True)
        dx, dg = _rmsnorm_bwd_math(xv, gv, err * (1.0 / d))
        dx_ref[...] = dx
        dxb_ref[...] = dx.astype(BF16)

        @pl.when(pl.program_id(0) == 0)
        def _():
            dg_ref[...] = dg
            loss_ref[...] = lsum

        @pl.when(pl.program_id(0) > 0)
        def _():
            dg_ref[...] += dg
            loss_ref[...] += lsum

    row = pl.BlockSpec((tr, d), lambda i: (i, 0))
    vec = pl.BlockSpec((1, d), lambda i: (0, 0))
    return pl.pallas_call(
        body, name=name, grid=(t // tr,),
        out_shape=[jax.ShapeDtypeStruct((t, d), F32), jax.ShapeDtypeStruct((t, d), BF16),
                   jax.ShapeDtypeStruct((1, d), F32), jax.ShapeDtypeStruct((1, d), F32)],
        in_specs=[row, vec, row], out_specs=[row, row, vec, vec],
        compiler_params=pltpu.CompilerParams(dimension_semantics=("arbitrary",)),
    )(x, g, target)


TIME_TILE = 256


def _sigmoid(v):
    return 1.0 / (1.0 + jnp.exp(-v))


def _gelu(v):
    return 0.5 * v * (1.0 + jnp.tanh(GELU_C * (v + GELU_A * v * v * v)))


def _gelu_grad(v):
    th = jnp.tanh(GELU_C * (v + GELU_A * v * v * v))
    return 0.5 * (1.0 + th) + 0.5 * v * (1.0 - th * th) * GELU_C * (1.0 + 3.0 * GELU_A * v * v)


def _one_minus_exp(v):
    series = -v * (1.0 + v * (0.5 + v * (1.0 / 6.0 + v * (1.0 / 24.0))))
    return jnp.where(v > -0.05, series, 1.0 - jnp.exp(v))


def _back(cur, halo, s, row):
    tt, cb = cur.shape
    pad = jnp.concatenate([pltpu.roll(halo, s, 0), jnp.zeros((tt - SUBLANES, cb), F32)], axis=0)
    return jnp.where(row < s, pad, pltpu.roll(cur, s, 0))


def _ahead(cur, halo, s, row):
    tt, cb = cur.shape
    pad = jnp.concatenate([jnp.zeros((tt - SUBLANES, cb), F32), pltpu.roll(halo, SUBLANES - s, 0)], axis=0)
    return jnp.where(row >= tt - s, pad, pltpu.roll(cur, tt - s, 0))


def _lru_gates(u, wr, wi, br, bi, lam):
    ub = u.astype(BF16)
    r = _sigmoid(jnp.dot(ub, wr, preferred_element_type=F32) + br)
    i = _sigmoid(jnp.dot(ub, wi, preferred_element_type=F32) + bi)
    sp = jnp.maximum(-lam, 0.0) + jnp.log1p(jnp.exp(-jnp.abs(lam)))
    log_a = -LRU_C * r * sp
    a = jnp.exp(log_a)
    mult = jnp.sqrt(_one_minus_exp(2.0 * log_a))
    return ub, r, i, a, mult


def _branch_specs(tt, cb, nt, time_of):
    per = tt // SUBLANES

    def tile(parts):
        if parts is None:
            return pl.BlockSpec((tt, cb), lambda n, s: (time_of(s), n))
        return pl.BlockSpec((parts, tt, cb), lambda n, s: (0, time_of(s), n))

    def halo_before(parts):
        if parts is None:
            return pl.BlockSpec((SUBLANES, cb), lambda n, s: (jnp.maximum(time_of(s) * per - 1, 0), n))
        return pl.BlockSpec((parts, SUBLANES, cb), lambda n, s: (0, jnp.maximum(time_of(s) * per - 1, 0), n))

    def rows(k):
        return pl.BlockSpec((k, cb), lambda n, s: (0, n))

    gate_w = pl.BlockSpec((None, cb, cb), lambda n, s: (n, 0, 0))
    return tile, halo_before, rows, gate_w


def branch_fwd(name, z5, caw, cab, cbw, wr, wi, br, bi, lam):
    _, t, d = z5.shape
    cb = wr.shape[-1]
    tt = _tile(t, TIME_TILE, SUBLANES)
    nt = t // tt
    tile, halo_before, rows, gate_w = _branch_specs(tt, cb, nt, lambda s: s)

    def body(z_ref, zh_ref, caw_ref, cab_ref, cbw_ref, wr_ref, wi_ref, br_ref, bi_ref, lam_ref,
             pa_ref, pb_ref, h_ref, carry_ref, acum_ref, bcum_ref):
        s = pl.program_id(1)
        row = lax.broadcasted_iota(jnp.int32, (tt, cb), 0)
        keep = jnp.where(s > 0, 1.0, 0.0)

        @pl.when(s == 0)
        def _():
            carry_ref[...] = jnp.zeros_like(carry_ref)

        xa, ya, cbv, ccv, cxv = (z_ref[p] for p in range(N_BRANCH))
        xa_h = zh_ref[0] * keep
        caw_v = caw_ref[...]
        u = caw_v[3:4] * xa + cab_ref[...]
        for j in range(1, CONV_A_WIDTH):
            u = u + caw_v[3 - j:4 - j] * _back(xa, xa_h, j, row)
        _, _, gi, a, mult = _lru_gates(u, wr_ref[...], wi_ref[...], br_ref[...], bi_ref[...], lam_ref[...])
        b = mult * (gi * u)

        rm = row & (SUBLANES - 1)
        for sh in (1, 2, 4):
            a_prev = jnp.where(rm >= sh, pltpu.roll(a, sh, 0), 1.0)
            b_prev = jnp.where(rm >= sh, pltpu.roll(b, sh, 0), 0.0)
            b = a * b_prev + b
            a = a * a_prev
        acum_ref[...] = a
        bcum_ref[...] = b

        def group(g, h_in):
            r0 = pl.multiple_of(g * SUBLANES, SUBLANES)
            hg = acum_ref[pl.ds(r0, SUBLANES), :] * h_in + bcum_ref[pl.ds(r0, SUBLANES), :]
            h_ref[pl.ds(r0, SUBLANES), :] = hg
            return jnp.broadcast_to(hg[SUBLANES - 1:SUBLANES, :], (SUBLANES, cb))

        carry_ref[...] = lax.fori_loop(0, tt // SUBLANES, group, carry_ref[...], unroll=4)
        pa_ref[...] = (h_ref[...] * _gelu(ya)).astype(BF16)

        q = ccv * cxv
        q_h = zh_ref[3] * zh_ref[4] * keep
        cbw_v = cbw_ref[...]
        v = cbw_v[2:3] * q
        for j in range(1, CONV_B_WIDTH):
            v = v + cbw_v[2 - j:3 - j] * _back(q, q_h, j, row)
        pb_ref[...] = (cbv * v).astype(BF16)

    act = jax.ShapeDtypeStruct((t, d), BF16)
    blocks = _nbytes((2 * N_BRANCH, tt, cb), F32) + _nbytes((2, cb, cb), BF16) + _nbytes((4, tt, cb), F32)
    return pl.pallas_call(
        body, name=name, grid=(d // cb, nt),
        out_shape=[act, act, jax.ShapeDtypeStruct((t, d), F32)],
        in_specs=[tile(N_BRANCH), halo_before(N_BRANCH), rows(CONV_A_WIDTH), rows(1), rows(CONV_B_WIDTH),
                  gate_w, gate_w, rows(1), rows(1), rows(1)],
        out_specs=[tile(None), tile(None), tile(None)],
        scratch_shapes=[pltpu.VMEM((SUBLANES, cb), F32), pltpu.VMEM((tt, cb), F32), pltpu.VMEM((tt, cb), F32)],
        compiler_params=pltpu.CompilerParams(dimension_semantics=("parallel", "arbitrary"),
                                             vmem_limit_bytes=_vmem_limit(blocks, 40 * tt * cb * 4)),
    )(z5, z5, caw, cab, cbw, wr, wi, br, bi, lam)


def branch_bwd(name, z5, hl, dpa, dpb, caw, cab, cbw, wr, wi, br, bi, lam):
    _, t, d = z5.shape
    cb = wr.shape[-1]
    tt = _tile(t, TIME_TILE, SUBLANES)
    nt = t // tt
    tile, halo_before, rows, gate_w = _branch_specs(tt, cb, nt, lambda s: nt - 1 - s)

    def body(z_ref, zh_ref, h_ref, hh_ref, dpa_ref, dpb_ref, caw_ref, cab_ref, cbw_ref, wr_ref, wi_ref,
             br_ref, bi_ref, lam_ref,
             dz_ref, dbias_ref, dcaw_ref, dcab_ref, dcbw_ref, dwr_ref, dwi_ref, dbr_ref, dbi_ref, dlam_ref,
             g_carry_ref, a_next_ref, du_next_ref, dv_next_ref, acum_ref, bcum_ref, gout_ref):
        s = pl.program_id(1)
        row = lax.broadcasted_iota(jnp.int32, (tt, cb), 0)
        keep = jnp.where(s < nt - 1, 1.0, 0.0)

        @pl.when(s == 0)
        def _():
            for r in (g_carry_ref, a_next_ref, du_next_ref, dv_next_ref, dbias_ref, dcaw_ref, dcab_ref, dcbw_ref,
                      dwr_ref, dwi_ref, dbr_ref, dbi_ref, dlam_ref):
                r[...] = jnp.zeros_like(r)

        xa, ya, cbv, ccv, cxv = (z_ref[p] for p in range(N_BRANCH))
        xa_h = zh_ref[0] * keep
        caw_v = caw_ref[...]
        wr_v, wi_v, lam_v = wr_ref[...], wi_ref[...], lam_ref[...]
        xa_back = [xa] + [_back(xa, xa_h, j, row) for j in range(1, CONV_A_WIDTH)]
        u = caw_v[3:4] * xa + cab_ref[...]
        for j in range(1, CONV_A_WIDTH):
            u = u + caw_v[3 - j:4 - j] * xa_back[j]
        ub, gr, gi, a, mult = _lru_gates(u, wr_v, wi_v, br_ref[...], bi_ref[...], lam_v)

        hv = h_ref[...]
        dpa_v = dpa_ref[...]
        dya = dpa_v * hv * _gelu_grad(ya)
        dh = dpa_v * _gelu(ya)

        a_up = _ahead(a, a_next_ref[...], 1, row)
        rm = row & (SUBLANES - 1)
        ca, cg = a_up, dh
        for sh in (1, 2, 4):
            a_nxt = jnp.where(rm + sh < SUBLANES, pltpu.roll(ca, tt - sh, 0), 1.0)
            g_nxt = jnp.where(rm + sh < SUBLANES, pltpu.roll(cg, tt - sh, 0), 0.0)
            cg = ca * g_nxt + cg
            ca = ca * a_nxt
        acum_ref[...] = ca
        bcum_ref[...] = cg
        n_groups = tt // SUBLANES

        def group(k, g_in):
            r0 = pl.multiple_of((n_groups - 1 - k) * SUBLANES, SUBLANES)
            gg = acum_ref[pl.ds(r0, SUBLANES), :] * g_in + bcum_ref[pl.ds(r0, SUBLANES), :]
            gout_ref[pl.ds(r0, SUBLANES), :] = gg
            return jnp.broadcast_to(gg[0:1, :], (SUBLANES, cb))

        g_carry_ref[...] = lax.fori_loop(0, n_groups, group, g_carry_ref[...], unroll=4)
        a_next_ref[...] = a[0:SUBLANES, :]
        gv = gout_ref[...]

        h_prev = _back(hv, hh_ref[...] * keep, 1, row)
        da = gv * h_prev
        iu = gi * u
        dmult = gv * iu
        di = gv * mult * u
        du = gv * mult * gi
        dla = da * a - dmult * (a * a) / mult
        sp = jnp.maximum(-lam_v, 0.0) + jnp.log1p(jnp.exp(-jnp.abs(lam_v)))
        dr = dla * (-LRU_C * sp)
        dlam_ref[...] += jnp.sum(dla * gr, axis=0, keepdims=True) * (LRU_C * _sigmoid(-lam_v))
        drp = dr * gr * (1.0 - gr)
        dip = di * gi * (1.0 - gi)
        dbr_ref[...] += jnp.sum(drp, axis=0, keepdims=True)
        dbi_ref[...] += jnp.sum(dip, axis=0, keepdims=True)
        drp_b, dip_b = drp.astype(BF16), dip.astype(BF16)
        tn_dims = (((0,), (0,)), ((), ()))
        nt_dims = (((1,), (1,)), ((), ()))
        dwr_ref[...] += lax.dot_general(ub, drp_b, tn_dims, preferred_element_type=F32)
        dwi_ref[...] += lax.dot_general(ub, dip_b, tn_dims, preferred_element_type=F32)
        du = du + lax.dot_general(drp_b, wr_v, nt_dims, preferred_element_type=F32)
        du = du + lax.dot_general(dip_b, wi_v, nt_dims, preferred_element_type=F32)

        du_h = du_next_ref[...]
        dxa = caw_v[3:4] * du
        dcaw_ref[3:4, :] += jnp.sum(du * xa, axis=0, keepdims=True)
        for j in range(1, CONV_A_WIDTH):
            dxa = dxa + caw_v[3 - j:4 - j] * _ahead(du, du_h, j, row)
            dcaw_ref[3 - j:4 - j, :] += jnp.sum(du * xa_back[j], axis=0, keepdims=True)
        dcab_ref[...] += jnp.sum(du, axis=0, keepdims=True)
        du_next_ref[...] = du[0:SUBLANES, :]

        q = ccv * cxv
        q_h = zh_ref[3] * zh_ref[4] * keep
        cbw_v = cbw_ref[...]
        q_back = [q] + [_back(q, q_h, j, row) for j in range(1, CONV_B_WIDTH)]
        v = cbw_v[2:3] * q
        for j in range(1, CONV_B_WIDTH):
            v = v + cbw_v[2 - j:3 - j] * q_back[j]
        dpb_v = dpb_ref[...]
        dcb = dpb_v * v
        dv = dpb_v * cbv
        dv_h = dv_next_ref[...]
        dq = cbw_v[2:3] * dv
        dcbw_ref[2:3, :] += jnp.sum(dv * q, axis=0, keepdims=True)
        for j in range(1, CONV_B_WIDTH):
            dq = dq + cbw_v[2 - j:3 - j] * _ahead(dv, dv_h, j, row)
            dcbw_ref[2 - j:3 - j, :] += jnp.sum(dv * q_back[j], axis=0, keepdims=True)
        dv_next_ref[...] = dv[0:SUBLANES, :]
        dcc = dq * cxv
        dcx = dq * ccv

        for p, val in enumerate((dxa, dya, dcb, dcc, dcx)):
            dz_ref[p] = val.astype(BF16)
            dbias_ref[p:p + 1, :] += jnp.sum(val, axis=0, keepdims=True)

    def acc_rows(k):
        return jax.ShapeDtypeStruct((k, d), F32)

    nb = d // cb
    gate_grad = jax.ShapeDtypeStruct((nb, cb, cb), F32)
    small = pltpu.VMEM((SUBLANES, cb), F32)
    big = pltpu.VMEM((tt, cb), F32)
    blocks = _nbytes((2 * (N_BRANCH + 3), tt, cb), F32) + _nbytes((4, cb, cb), F32)
    return pl.pallas_call(
        body, name=name, grid=(nb, nt),
        out_shape=[jax.ShapeDtypeStruct((N_BRANCH, t, d), BF16), acc_rows(N_BRANCH), acc_rows(CONV_A_WIDTH),
                   acc_rows(1), acc_rows(CONV_B_WIDTH), gate_grad, gate_grad, acc_rows(1), acc_rows(1), acc_rows(1)],
        in_specs=[tile(N_BRANCH), halo_before(N_BRANCH), tile(None), halo_before(None), tile(None), tile(None),
                  rows(CONV_A_WIDTH), rows(1), rows(CONV_B_WIDTH), gate_w, gate_w, rows(1), rows(1), rows(1)],
        out_specs=[tile(N_BRANCH), rows(N_BRANCH), rows(CONV_A_WIDTH), rows(1), rows(CONV_B_WIDTH),
                   gate_w, gate_w, rows(1), rows(1), rows(1)],
        scratch_shapes=[small, small, small, small, big, big, big],
        compiler_params=pltpu.CompilerParams(dimension_semantics=("parallel", "arbitrary"),
                                             vmem_limit_bytes=_vmem_limit(blocks, 60 * tt * cb * 4)),
    )(z5, z5, hl, hl, dpa, dpb, caw, cab, cbw, wr, wi, br, bi, lam)


MERGE_COLS = 512


def merge_fwd(name, z2, oa, ob):
    t, d = oa.shape
    tr, tc = _tile(t, ROW_TILE, SUBLANES), _tile(d, MERGE_COLS)

    def body(z_ref, oa_ref, ob_ref, m_ref):
        m_ref[...] = (_sigmoid(z_ref[0]) * oa_ref[...] + _sigmoid(z_ref[1]) * ob_ref[...]).astype(BF16)

    blk = pl.BlockSpec((tr, tc), lambda i, j: (i, j))
    return pl.pallas_call(
        body, name=name, grid=(t // tr, d // tc),
        out_shape=jax.ShapeDtypeStruct((t, d), BF16),
        in_specs=[pl.BlockSpec((2, tr, tc), lambda i, j: (0, i, j)), blk, blk], out_specs=blk,
        compiler_params=pltpu.CompilerParams(dimension_semantics=("parallel", "parallel")),
    )(z2, oa, ob)


def merge_bwd(name, dm, z2, oa, ob):
    t, d = oa.shape
    tr, tc = _tile(t, ROW_TILE, SUBLANES), _tile(d, MERGE_COLS)

    def body(dm_ref, z_ref, oa_ref, ob_ref, doa_ref, dob_ref, dz_ref, dbias_ref):
        dmv = dm_ref[...]
        sa, sb = _sigmoid(z_ref[0]), _sigmoid(z_ref[1])
        doa_ref[...] = (dmv * sa).astype(BF16)
        dob_ref[...] = (dmv * sb).astype(BF16)
        dga = dmv * oa_ref[...] * sa * (1.0 - sa)
        dgb = dmv * ob_ref[...] * sb * (1.0 - sb)
        dz_ref[0] = dga.astype(BF16)
        dz_ref[1] = dgb.astype(BF16)

        @pl.when(pl.program_id(1) == 0)
        def _():
            dbias_ref[...] = jnp.zeros_like(dbias_ref)

        dbias_ref[0:1, :] += jnp.sum(dga, axis=0, keepdims=True)
        dbias_ref[1:2, :] += jnp.sum(dgb, axis=0, keepdims=True)

    blk = pl.BlockSpec((tr, tc), lambda j, i: (i, j))
    pair = pl.BlockSpec((2, tr, tc), lambda j, i: (0, i, j))
    act = jax.ShapeDtypeStruct((t, d), BF16)
    return pl.pallas_call(
        body, name=name, grid=(d // tc, t // tr),
        out_shape=[act, act, jax.ShapeDtypeStruct((2, t, d), BF16), jax.ShapeDtypeStruct((2, d), F32)],
        in_specs=[blk, pair, blk, blk],
        out_specs=[blk, blk, pair, pl.BlockSpec((2, tc), lambda j, i: (0, j))],
        compiler_params=pltpu.CompilerParams(dimension_semantics=("parallel", "arbitrary")),
    )(dm, z2, oa, ob)


def _adamw_math(w, g, m, v):
    m = ADAM_B1 * m + (1.0 - ADAM_B1) * g
    v = ADAM_B2 * v + (1.0 - ADAM_B2) * (g * g)
    m_hat = m / (1.0 - ADAM_B1 ** ADAM_STEP)
    v_hat = v / (1.0 - ADAM_B2 ** ADAM_STEP)
    delta = -ADAM_LR * (m_hat / (jnp.sqrt(v_hat) + ADAM_EPS) + ADAM_WD * w)
    return delta, m, v


def _sum_in_order(ref):
    g = ref[0].astype(F32)
    for p in range(1, N_DEV):
        g = g + ref[p].astype(F32)
    return g


def adamw_sharded(name, recv, w, m, v):
    r, c = w.shape
    tr = _tile(r, max(SUBLANES, (256 * 1024) // c), 16)

    def body(recv_ref, w_ref, m_ref, v_ref, g_ref, d_ref, mo_ref, vo_ref):
        g = _sum_in_order(recv_ref)
        delta, m2, v2 = _adamw_math(w_ref[...], g, m_ref[...], v_ref[...])
        g_ref[...] = g
        d_ref[...] = delta
        mo_ref[...] = m2
        vo_ref[...] = v2

    blk = pl.BlockSpec((tr, c), lambda i: (i, 0))
    out = jax.ShapeDtypeStruct((r, c), F32)
    blocks = _nbytes((N_DEV, tr, c), recv.dtype) + 7 * _nbytes((tr, c), F32)
    return pl.pallas_call(
        body, name=name, grid=(r // tr,),
        out_shape=[out, out, out, out],
        in_specs=[pl.BlockSpec((N_DEV, tr, c), lambda i: (0, i, 0)), blk, blk, blk],
        out_specs=[blk, blk, blk, blk],
        compiler_params=pltpu.CompilerParams(dimension_semantics=("parallel",),
                                             vmem_limit_bytes=_vmem_limit(blocks)),
    )(recv, w, m, v)


def sum_partials(name, parts):
    _, r, c = parts.shape

    def body(p_ref, o_ref):
        o_ref[...] = _sum_in_order(p_ref)

    return pl.pallas_call(body, name=name, out_shape=jax.ShapeDtypeStruct((r, c), F32))(parts)


def adamw_small(name, g, w, m, v):
    def body(g_ref, w_ref, m_ref, v_ref, d_ref, mo_ref, vo_ref):
        delta, m2, v2 = _adamw_math(w_ref[...], g_ref[...], m_ref[...], v_ref[...])
        d_ref[...] = delta
        mo_ref[...] = m2
        vo_ref[...] = v2

    out = jax.ShapeDtypeStruct(w.shape, F32)
    return pl.pallas_call(body, name=name, out_shape=[out, out, out])(g, w, m, v)


def _pack(arrays):
    flat = jnp.concatenate([a.reshape(-1) for a in arrays])
    n = flat.shape[0]
    rows = -(-n // (SUBLANES * LANES)) * SUBLANES
    flat = jnp.pad(flat, (0, rows * LANES - n))
    return flat.reshape(rows, LANES)


def _unpack(packed, shapes):
    flat = packed.reshape(-1)
    out, off = [], 0
    for s in shapes:
        n = math.prod(s)
        out.append(flat[off:off + n].reshape(s))
        off += n
    return out


def kernel(x, norm1_g, w_in, b_in, conv_a_w, conv_a_b, lru_wr, lru_br, lru_wi, lru_bi, lru_lam, conv_b_w, w_pa, w_pb, w_o, norm2_g, w_mlp1, w_mlp2, final_g, loss_target, m_norm1_g, m_w_in, m_b_in, m_conv_a_w, m_conv_a_b, m_lru_wr, m_lru_br, m_lru_wi, m_lru_bi, m_lru_lam, m_conv_b_w, m_w_pa, m_w_pb, m_w_o, m_norm2_g, m_w_mlp1, m_w_mlp2, m_final_g, v_norm1_g, v_w_in, v_b_in, v_conv_a_w, v_conv_a_b, v_lru_wr, v_lru_br, v_lru_wi, v_lru_bi, v_lru_lam, v_conv_b_w, v_w_pa, v_w_pb, v_w_o, v_norm2_g, v_w_mlp1, v_w_mlp2, v_final_g):
    weights = dict(norm1_g=norm1_g, w_in=w_in, b_in=b_in, conv_a_w=conv_a_w, conv_a_b=conv_a_b, lru_wr=lru_wr,
                   lru_br=lru_br, lru_wi=lru_wi, lru_bi=lru_bi, lru_lam=lru_lam, conv_b_w=conv_b_w, w_pa=w_pa,
                   w_pb=w_pb, w_o=w_o, norm2_g=norm2_g, w_mlp1=w_mlp1, w_mlp2=w_mlp2, final_g=final_g)
    mom1 = dict(norm1_g=m_norm1_g, w_in=m_w_in, b_in=m_b_in, conv_a_w=m_conv_a_w, conv_a_b=m_conv_a_b,
                lru_wr=m_lru_wr, lru_br=m_lru_br, lru_wi=m_lru_wi, lru_bi=m_lru_bi, lru_lam=m_lru_lam,
                conv_b_w=m_conv_b_w, w_pa=m_w_pa, w_pb=m_w_pb, w_o=m_w_o, norm2_g=m_norm2_g, w_mlp1=m_w_mlp1,
                w_mlp2=m_w_mlp2, final_g=m_final_g)
    mom2 = dict(norm1_g=v_norm1_g, w_in=v_w_in, b_in=v_b_in, conv_a_w=v_conv_a_w, conv_a_b=v_conv_a_b,
                lru_wr=v_lru_wr, lru_br=v_lru_br, lru_wi=v_lru_wi, lru_bi=v_lru_bi, lru_lam=v_lru_lam,
                conv_b_w=v_conv_b_w, w_pa=v_w_pa, w_pb=v_w_pb, w_o=v_w_o, norm2_g=v_norm2_g, w_mlp1=v_w_mlp1,
                w_mlp2=v_w_mlp2, final_g=v_final_g)
    names = list(weights)

    t, d = x.shape[1], x.shape[2]
    n_in = b_in.shape[1]
    nb, bw = lru_wr.shape[1], lru_wr.shape[3]
    me = _flat(*_my_coords())
    x0 = x.reshape(t, d)
    target = loss_target.reshape(t, d)

    big = ["w_in", "w_pa", "w_pb", "w_o", "w_mlp1", "w_mlp2", "lru_wr", "lru_wi"]
    big_axis = dict(w_in=1, w_pa=0, w_pb=0, w_o=0, w_mlp1=1, w_mlp2=0, lru_wr=1, lru_wi=1)
    small_sharded = ["conv_a_w", "conv_b_w", "lru_br", "lru_bi"]
    small_packed = _pack([weights[k] for k in small_sharded])
    full = []
    for l in range(DEPTH):
        shards = [weights[k][l].astype(BF16) for k in big] + ([small_packed[None]] if l == 0 else [])
        axes = [big_axis[k] for k in big] + ([0] if l == 0 else [])
        gathered = all_gather(f"gather_weights_l{l}", shards, axes)
        full.append(dict(zip(big, gathered)))
        if l == 0:
            small_rows = gathered[-1].reshape(N_DEV, -1)
    small_full = {}
    off = 0
    for k in small_sharded:
        shard_shape = weights[k].shape
        n = math.prod(shard_shape)
        g = small_rows[:, off:off + n].reshape(N_DEV, *shard_shape)
        small_full[k] = jnp.moveaxis(g, 0, 2).reshape(shard_shape[0], shard_shape[1], N_DEV * shard_shape[2])
        off += n

    def layer_params(l):
        f = full[l]
        return dict(
            g1=norm1_g[l][None], w_in=f["w_in"], b_in=b_in[l][None], caw=small_full["conv_a_w"][l],
            cab=conv_a_b[l][None], wr=f["lru_wr"], wi=f["lru_wi"],
            br=small_full["lru_br"][l].reshape(1, d), bi=small_full["lru_bi"][l].reshape(1, d),
            lam=lru_lam[l][None], cbw=small_full["conv_b_w"][l], w_pa=f["w_pa"], w_pb=f["w_pb"], w_o=f["w_o"],
            g2=norm2_g[l][None], w1=f["w_mlp1"], w2=f["w_mlp2"])

    params = [layer_params(l) for l in range(DEPTH)]

    relu2 = lambda acc: (acc, jnp.square(jnp.maximum(acc, 0.0)))
    add_res = lambda acc, r: (acc + r,)
    relu2_bwd = lambda acc, pre: (acc * (2.0 * jnp.maximum(pre, 0.0)),)

    saved = []
    xl = x0
    for l in range(DEPTH):
        p = params[l]
        tag = f"l{l}"
        h1 = rmsnorm_fwd(f"norm1_{tag}", xl, p["g1"])
        z5 = mm_in_proj(f"in_proj5_{tag}", h1, p["w_in"], p["b_in"], 0, N_BRANCH)
        z2 = mm_in_proj(f"in_proj2_{tag}", h1, p["w_in"], p["b_in"], N_BRANCH * d, N_SPLIT - N_BRANCH)
        pa, pb, hl = branch_fwd(f"branch_fwd_{tag}", z5, p["caw"], p["cab"], p["cbw"], p["wr"], p["wi"],
                                p["br"], p["bi"], p["lam"])
        oa = mm_plain(f"proj_a_{tag}", pa, p["w_pa"])
        ob = mm_plain(f"proj_b_{tag}", pb, p["w_pb"])
        mg = merge_fwd(f"merge_{tag}", z2, oa, ob)
        x1 = mm_plain(f"proj_o_{tag}", mg, p["w_o"], epilogue=add_res, extras=(xl,), extra_kinds=("full",))
        h2 = rmsnorm_fwd(f"norm2_{tag}", x1, p["g2"])
        pre, uu = mm_plain(f"mlp1_{tag}", h2, p["w1"], epilogue=relu2, out_dtypes=[F32, BF16])
        x2 = mm_plain(f"mlp2_{tag}", uu, p["w2"], epilogue=add_res, extras=(x1,), extra_kinds=("full",))
        saved.append(dict(x0=xl, h1=h1, z5=z5, z2=z2, pa=pa, pb=pb, hl=hl, oa=oa, ob=ob, mg=mg, x1=x1, h2=h2,
                          pre=pre, uu=uu))
        xl = x2

    dx, dxb, d_final_g, loss_cols = loss_head("loss_head", xl, final_g[None], target)
    loss = lax.psum(jnp.sum(loss_cols), ("x", "y", "c"))

    small_grads = {}
    recv = None
    for l in reversed(range(DEPTH)):
        p, sv = params[l], saved[l]
        tag = f"l{l}"
        dpre = mm_plain(f"mlp2_dx_{tag}", dxb, p["w2"], tb=True, out_dtype=BF16, epilogue=relu2_bwd,
                        extras=(sv["pre"],), extra_kinds=("full",))
        dw2 = mm_plain(f"mlp2_dw_{tag}", sv["uu"], dxb, ta=True, out_dtype=BF16)
        dw1 = mm_plain(f"mlp1_dw_{tag}", sv["h2"], dpre, ta=True, out_dtype=BF16)
        dh2 = mm_plain(f"mlp1_dx_{tag}", dpre, p["w1"], tb=True)
        dx1, dx1b, dg2 = rmsnorm_bwd(f"norm2_bwd_{tag}", dh2, sv["x1"], p["g2"], dx)
        dmg = mm_plain(f"proj_o_dx_{tag}", dx1b, p["w_o"], tb=True)
        dwo = mm_plain(f"proj_o_dw_{tag}", sv["mg"], dx1b, ta=True, out_dtype=BF16)
        doa, dob, dz2, dbias2 = merge_bwd(f"merge_bwd_{tag}", dmg, sv["z2"], sv["oa"], sv["ob"])
        dpa = mm_plain(f"proj_a_dx_{tag}", doa, p["w_pa"], tb=True)
        dpb = mm_plain(f"proj_b_dx_{tag}", dob, p["w_pb"], tb=True)
        dwpa = mm_plain(f"proj_a_dw_{tag}", sv["pa"], doa, ta=True, out_dtype=BF16)
        dwpb = mm_plain(f"proj_b_dw_{tag}", sv["pb"], dob, ta=True, out_dtype=BF16)
        (dz5, dbias5, dcaw, dcab, dcbw, dwr, dwi, dbr, dbi, dlam) = branch_bwd(
            f"branch_bwd_{tag}", sv["z5"], sv["hl"], dpa, dpb, p["caw"], p["cab"], p["cbw"], p["wr"], p["wi"],
            p["br"], p["bi"], p["lam"])
        dwin = mm_in_proj_dw(f"in_proj5_dw_{tag}", sv["h1"], dz5, 0, n_in)
        dwin = mm_in_proj_dw(f"in_proj2_dw_{tag}", sv["h1"], dz2, N_BRANCH * d, n_in, prev=dwin)
        dh1 = mm_in_proj_dh(f"in_proj5_dx_{tag}", dz5, p["w_in"], 0)
        dh1 = mm_in_proj_dh(f"in_proj2_dx_{tag}", dz2, p["w_in"], N_BRANCH * d, res=dh1)
        dx, dxb, dg1 = rmsnorm_bwd(f"norm1_bwd_{tag}", dh1, sv["x0"], p["g1"], dx1)

        layer_grads = [dwin, dwpa, dwpb, dwo, dw1, dw2, dwr.astype(BF16), dwi.astype(BF16)]
        layer_axes = [1, 0, 0, 0, 1, 0, 1, 1]
        recv = exchange_grads(f"exchange_grads_{tag}", layer_grads, layer_axes, l, prev=recv)
        small_grads[l] = dict(norm1_g=dg1, b_in=jnp.concatenate([dbias5.reshape(1, -1), dbias2.reshape(1, -1)], axis=1),
                              conv_a_b=dcab, lru_lam=dlam, norm2_g=dg2, conv_a_w=dcaw, conv_b_w=dcbw,
                              lru_br=dbr, lru_bi=dbi)
    grad_x = dx.reshape(x.shape)

    replicated = ["norm1_g", "b_in", "conv_a_b", "lru_lam", "norm2_g"]
    small_list = [jnp.stack([small_grads[l][k] for l in range(DEPTH)]) for k in replicated + small_sharded]
    small_list.append(d_final_g)
    packed_partial = _pack(small_list)
    partials = all_gather("gather_small_grads", [packed_partial[None]], [0])[0]
    summed = sum_partials("sum_small_grads", partials)
    pieces = _unpack(summed, [a.shape for a in small_list])
    small_sum = dict(zip(replicated + small_sharded + ["final_g"], pieces))

    grads = {}
    for k in replicated:
        grads[k] = small_sum[k].reshape(weights[k].shape)
    grads["final_g"] = small_sum["final_g"].reshape(weights["final_g"].shape)
    for k in small_sharded:
        shard_shape = weights[k].shape
        g = small_sum[k].reshape(shard_shape[0], shard_shape[1], N_DEV, shard_shape[2])
        grads[k] = lax.dynamic_index_in_dim(g, me, axis=2, keepdims=False)

    small_names = replicated + ["final_g"] + small_sharded
    small_shapes = [weights[k].shape for k in small_names]
    pk = lambda src: _pack([src[k] for k in small_names])
    sd, sm, sv2 = adamw_small("adamw_small", pk(grads), pk(weights), pk(mom1), pk(mom2))
    delta = dict(zip(small_names, _unpack(sd, small_shapes)))
    new_m = dict(zip(small_names, _unpack(sm, small_shapes)))
    new_v = dict(zip(small_names, _unpack(sv2, small_shapes)))

    for k, rc in zip(big, recv):
        shape = weights[k].shape
        c = shape[-1]
        flat = lambda a: a.reshape(-1, c)
        g, dl, m2, v2 = adamw_sharded(f"adamw_{k}", rc.reshape(N_DEV, -1, c), flat(weights[k]), flat(mom1[k]),
                                      flat(mom2[k]))
        grads[k], delta[k], new_m[k], new_v[k] = (a.reshape(shape) for a in (g, dl, m2, v2))

    return (loss, grad_x, *[grads[k] for k in names], *[delta[k] for k in names],
            *[new_m[k] for k in names], *[new_v[k] for k in names])
```

```python
import functools
import math

import jax
import jax.numpy as jnp
from jax import lax
from jax.experimental import pallas as pl
from jax.experimental.pallas import tpu as pltpu

F32 = jnp.float32
BF16 = jnp.bfloat16

N_DEV = 8
DEPTH = 2
N_SPLIT = 7
N_BRANCH = 5
EPS = 1e-6
LRU_C = 8.0
CONV_A_WIDTH = 4
CONV_B_WIDTH = 3
GELU_C = math.sqrt(2.0 / math.pi)
GELU_A = 0.044715

ADAM_LR = 0.001
ADAM_B1 = 0.9
ADAM_B2 = 0.999
ADAM_EPS = 1e-08
ADAM_WD = 0.01
ADAM_STEP = 10

LANES = 128
SUBLANES = 8
VMEM_LIMIT_CAP = 56 * 2 ** 20
MESH = pl.DeviceIdType.MESH
ANY = pl.BlockSpec(memory_space=pl.ANY)


def _tile(n, target, align=LANES):
    if n <= target:
        return n
    t = (target // align) * align
    while t >= align:
        if n % t == 0:
            return t
        t -= align
    return n


def _nbytes(shape, dtype):
    return math.prod(shape) * jnp.dtype(dtype).itemsize


def _vmem_limit(block_bytes, scratch_bytes=0):
    est = 2 * block_bytes + scratch_bytes
    return int(min(max(2 * est, 32 * 2 ** 20), VMEM_LIMIT_CAP))


def _my_coords():
    return lax.axis_index("x"), lax.axis_index("y"), lax.axis_index("c")


def _flat(x, y, c):
    return 4 * x + 2 * y + c


def _peer(k, x, y, c):
    return (1 - x if k & 4 else x, 1 - y if k & 2 else y, 1 - c if k & 1 else c)


def _slab(ref, axis, start, size):
    idx = tuple(pl.ds(start, size) if d == axis else slice(None) for d in range(len(ref.shape)))
    return ref.at[idx]


def all_gather(name, shards, axes):
    n = len(shards)
    sizes = [s.shape[a] for s, a in zip(shards, axes)]
    out_shape = []
    for s, a in zip(shards, axes):
        full = list(s.shape)
        full[a] *= N_DEV
        out_shape.append(jax.ShapeDtypeStruct(tuple(full), s.dtype))

    def body(*refs):
        srcs, outs = refs[:n], refs[n:2 * n]
        send_sems, recv_sems, local_sems = refs[2 * n:]
        x, y, c = _my_coords()
        me = _flat(x, y, c)
        local = []
        for i in range(n):
            cp = pltpu.make_async_copy(srcs[i], _slab(outs[i], axes[i], me * sizes[i], sizes[i]), local_sems.at[i])
            cp.start()
            local.append(cp)
        remote = []
        for k in range(1, N_DEV):
            for i in range(n):
                mine = _slab(outs[i], axes[i], me * sizes[i], sizes[i])
                cp = pltpu.make_async_remote_copy(
                    src_ref=srcs[i], dst_ref=mine,
                    send_sem=send_sems.at[(k - 1) * n + i], recv_sem=recv_sems.at[(k - 1) * n + i],
                    device_id=_peer(k, x, y, c), device_id_type=MESH)
                cp.start()
                remote.append(cp)
        for cp in remote:
            cp.wait()
        for cp in local:
            cp.wait()

    return pl.pallas_call(
        body, name=name, out_shape=out_shape,
        in_specs=[ANY] * n, out_specs=[ANY] * n,
        scratch_shapes=[pltpu.SemaphoreType.DMA(((N_DEV - 1) * n,)),
                        pltpu.SemaphoreType.DMA(((N_DEV - 1) * n,)),
                        pltpu.SemaphoreType.DMA((n,))],
    )(*shards)


N_CHIP = 4
OTHER_CHIPS = ((1, 0), (0, 1), (1, 1))


def all_gather_by_chip(name, shards, axes):
    n = len(shards)
    sizes = [s.shape[a] for s, a in zip(shards, axes)]
    out_shape = []
    for s, a in zip(shards, axes):
        full = list(s.shape)
        full[a] *= N_DEV
        out_shape.append(jax.ShapeDtypeStruct(tuple(full), s.dtype))

    def body(*refs):
        srcs, outs = refs[:n], refs[n:2 * n]
        send_sems, recv_sems, local_sems = refs[2 * n:]
        x, y, c = _my_coords()
        me, sibling = (x, y, c), (x, y, 1 - c)
        chips = [(x + dx - 2 * x * dx, y + dy - 2 * y * dy) for dx, dy in OTHER_CHIPS]

        def rows(i, block):
            return _slab(outs[i], axes[i], _flat(*block) * sizes[i], sizes[i])

        def copy(kind, i, block, to, src=None):
            return pltpu.make_async_remote_copy(
                src_ref=rows(i, block) if src is None else src, dst_ref=rows(i, block),
                send_sem=send_sems.at[kind * n + i], recv_sem=recv_sems.at[kind * n + i],
                device_id=to, device_id_type=MESH)

        local = [pltpu.make_async_copy(srcs[i], rows(i, me), local_sems.at[i]) for i in range(n)]
        for cp in local:
            cp.start()
        first = [copy(0, i, me, sibling, src=srcs[i]) for i in range(n)]
        for j, chip in enumerate(chips):
            first += [copy(1 + j, i, me, (*chip, c), src=srcs[i]) for i in range(n)]
        for cp in first:
            cp.start()
        passed = []
        for j, chip in enumerate(chips):
            for i in range(n):
                copy(1 + j, i, (*chip, c), me).wait_recv()
                cp = copy(4 + j, i, (*chip, c), sibling)
                cp.start()
                passed.append(cp)
        for i in range(n):
            copy(0, i, sibling, me).wait_recv()
        for j, chip in enumerate(chips):
            for i in range(n):
                copy(4 + j, i, (*chip, 1 - c), me).wait_recv()
        for cp in first + passed:
            cp.wait_send()
        for cp in local:
            cp.wait()

    return pl.pallas_call(
        body, name=name, out_shape=out_shape,
        in_specs=[ANY] * n, out_specs=[ANY] * n,
        scratch_shapes=[pltpu.SemaphoreType.DMA((7 * n,)), pltpu.SemaphoreType.DMA((7 * n,)),
                        pltpu.SemaphoreType.DMA((n,))],
    )(*shards)


def pair_exchange(name, grads, axes):
    n = len(grads)
    sizes = [g.shape[a] // N_DEV for g, a in zip(grads, axes)]
    out_shape = []
    for g, a, sz in zip(grads, axes, sizes):
        shard = list(g.shape)
        shard[a] = sz
        out_shape.append(jax.ShapeDtypeStruct((N_CHIP, *shard), g.dtype))
    out_shape = out_shape + out_shape

    def body(*refs):
        srcs, mine, theirs = refs[:n], refs[n:2 * n], refs[2 * n:3 * n]
        send_sems, recv_sems, local_sems = refs[3 * n:]
        x, y, c = _my_coords()
        local, remote = [], []
        for i in range(n):
            for q in range(N_CHIP):
                k = i * N_CHIP + q
                keep = _slab(srcs[i], axes[i], (2 * q + c) * sizes[i], sizes[i])
                give = _slab(srcs[i], axes[i], (2 * q + 1 - c) * sizes[i], sizes[i])
                local.append(pltpu.make_async_copy(keep, mine[i].at[q], local_sems.at[k]))
                remote.append(pltpu.make_async_remote_copy(
                    src_ref=give, dst_ref=theirs[i].at[q], send_sem=send_sems.at[k], recv_sem=recv_sems.at[k],
                    device_id=(x, y, 1 - c), device_id_type=MESH))
        for cp in remote + local:
            cp.start()
        for cp in remote + local:
            cp.wait()

    return pl.pallas_call(
        body, name=name, out_shape=out_shape,
        in_specs=[ANY] * n, out_specs=[ANY] * (2 * n),
        scratch_shapes=[pltpu.SemaphoreType.DMA((N_CHIP * n,)), pltpu.SemaphoreType.DMA((N_CHIP * n,)),
                        pltpu.SemaphoreType.DMA((N_CHIP * n,))],
    )(*grads)


def chip_sum(name, mine, theirs):
    shape = mine.shape
    c = shape[-1]
    a2, b2 = mine.reshape(-1, c), theirs.reshape(-1, c)
    r = a2.shape[0]
    tr = _tile(r, max(SUBLANES, (512 * 1024) // c), 16)

    def body(a_ref, b_ref, o_ref):
        o_ref[...] = (a_ref[...].astype(F32) + b_ref[...].astype(F32)).astype(o_ref.dtype)

    blk = pl.BlockSpec((tr, c), lambda i: (i, 0))
    out = pl.pallas_call(
        body, name=name, grid=(r // tr,), out_shape=jax.ShapeDtypeStruct((r, c), mine.dtype),
        in_specs=[blk, blk], out_specs=blk,
        compiler_params=pltpu.CompilerParams(dimension_semantics=("parallel",)),
    )(a2, b2)
    return out.reshape(shape)


def chip_exchange(name, sums, layer, prev=None):
    n = len(sums)
    out_shape = [jax.ShapeDtypeStruct((N_CHIP, DEPTH, *s.shape[1:]), s.dtype) for s in sums]
    n_in = n if prev is None else 2 * n

    def body(*refs):
        srcs, outs = refs[:n], refs[n_in:n_in + n]
        send_sems, recv_sems, local_sems = refs[n_in + n:]
        x, y, c = _my_coords()
        my_chip = 2 * x + y
        local = [pltpu.make_async_copy(srcs[i].at[my_chip], outs[i].at[my_chip, layer], local_sems.at[i])
                 for i in range(n)]
        remote = []
        for j, (dx, dy) in enumerate(OTHER_CHIPS):
            tx, ty = x + dx - 2 * x * dx, y + dy - 2 * y * dy
            for i in range(n):
                remote.append(pltpu.make_async_remote_copy(
                    src_ref=srcs[i].at[2 * tx + ty], dst_ref=outs[i].at[my_chip, layer],
                    send_sem=send_sems.at[j * n + i], recv_sem=recv_sems.at[j * n + i],
                    device_id=(tx, ty, c), device_id_type=MESH))
        for cp in remote + local:
            cp.start()
        for cp in remote + local:
            cp.wait()

    operands = list(sums) + ([] if prev is None else list(prev))
    aliases = {} if prev is None else {n + i: i for i in range(n)}
    return pl.pallas_call(
        body, name=name, out_shape=out_shape,
        in_specs=[ANY] * n_in, out_specs=[ANY] * n,
        input_output_aliases=aliases,
        scratch_shapes=[pltpu.SemaphoreType.DMA((3 * n,)), pltpu.SemaphoreType.DMA((3 * n,)),
                        pltpu.SemaphoreType.DMA((n,))],
    )(*operands)


def _mm(name, a, b, *, ta, tb, grid, tm, tn, tk, a_spec, b_spec, outs, extras=(), epilogue=None, alias=None):
    n_extra, n_out = len(extras), len(outs)
    nk = grid[2]
    dn = (((0,) if ta else (1,), (1,) if tb else (0,)), ((), ()))

    def finish(acc, extra_refs, out_refs):
        vals = epilogue(acc, *[r[...] for r in extra_refs]) if epilogue is not None else (acc,)
        for o_ref, v in zip(out_refs, vals):
            o_ref[...] = v.astype(o_ref.dtype)

    def body(*refs):
        a_ref, b_ref = refs[0], refs[1]
        extra_refs = refs[2:2 + n_extra]
        first_out = 2 + n_extra + (1 if alias is not None else 0)
        out_refs = refs[first_out:first_out + n_out]
        part = lax.dot_general(a_ref[...], b_ref[...], dn, preferred_element_type=F32)
        if nk == 1:
            finish(part, extra_refs, out_refs)
            return
        acc_ref = refs[-1]
        k = pl.program_id(2)

        @pl.when(k == 0)
        def _():
            acc_ref[...] = part

        @pl.when(k > 0)
        def _():
            acc_ref[...] += part

        @pl.when(k == nk - 1)
        def _():
            finish(acc_ref[...], extra_refs, out_refs)

    operands = [a, b] + [e for e, _ in extras]
    in_specs = [a_spec, b_spec] + [s for _, s in extras]
    aliases = {}
    if alias is not None:
        aliases = {len(operands): 0}
        operands.append(alias)
        in_specs.append(ANY)
    blocks = _nbytes((tm, tk), a.dtype) + _nbytes((tk, tn), b.dtype)
    blocks += sum(_nbytes((tm, tn), s.dtype) for s, _ in outs)
    blocks += sum(_nbytes((e.shape[-2] if e.shape[-2] < tm else tm, tn), e.dtype) for e, _ in extras)
    scratch = [] if nk == 1 else [pltpu.VMEM((tm, tn), F32)]
    return pl.pallas_call(
        body, name=name, grid=grid,
        out_shape=[s for s, _ in outs], in_specs=in_specs, out_specs=[s for _, s in outs],
        input_output_aliases=aliases, scratch_shapes=scratch,
        compiler_params=pltpu.CompilerParams(
            dimension_semantics=("parallel", "parallel", "arbitrary"),
            vmem_limit_bytes=_vmem_limit(blocks, 0 if nk == 1 else 3 * tm * tn * 4)),
    )(*operands)


def _mm_tiles(m, n, k):
    return _tile(m, 512), _tile(n, 1024), _tile(k, 2048)


def mm_plain(name, a, b, *, ta=False, tb=False, out_dtype=F32, epilogue=None, extras=(), extra_kinds=(),
             out_dtypes=None):
    m = a.shape[1] if ta else a.shape[0]
    k = a.shape[0] if ta else a.shape[1]
    n = b.shape[0] if tb else b.shape[1]
    tm, tn, tk = _mm_tiles(m, n, k)
    grid = (m // tm, n // tn, k // tk)
    a_spec = pl.BlockSpec((tk, tm), lambda i, j, kk: (kk, i)) if ta else pl.BlockSpec((tm, tk), lambda i, j, kk: (i, kk))
    b_spec = pl.BlockSpec((tn, tk), lambda i, j, kk: (j, kk)) if tb else pl.BlockSpec((tk, tn), lambda i, j, kk: (kk, j))
    ex = []
    for e, kind in zip(extras, extra_kinds):
        if kind == "row":
            ex.append((e, pl.BlockSpec((1, tn), lambda i, j, kk: (0, j))))
        else:
            ex.append((e, pl.BlockSpec((tm, tn), lambda i, j, kk: (i, j))))
    out_dtypes = out_dtypes or [out_dtype]
    outs = [(jax.ShapeDtypeStruct((m, n), dt), pl.BlockSpec((tm, tn), lambda i, j, kk: (i, j))) for dt in out_dtypes]
    res = _mm(name, a, b, ta=ta, tb=tb, grid=grid, tm=tm, tn=tn, tk=tk, a_spec=a_spec, b_spec=b_spec,
              outs=outs, extras=ex, epilogue=epilogue)
    return res if len(res) > 1 else res[0]


def mm_in_proj(name, h, w, bias, col0, parts):
    t, d = h.shape
    tm, tn, tk = _mm_tiles(t, d, d)
    per = d // tn
    off = col0 // tn
    grid = (t // tm, parts * per, d // tk)
    outs = [(jax.ShapeDtypeStruct((parts, t, d), F32),
             pl.BlockSpec((None, tm, tn), lambda i, j, kk: (j // per, i, j % per)))]
    ex = [(bias, pl.BlockSpec((1, tn), lambda i, j, kk: (0, j + off)))]
    return _mm(name, h, w, ta=False, tb=False, grid=grid, tm=tm, tn=tn, tk=tk,
               a_spec=pl.BlockSpec((tm, tk), lambda i, j, kk: (i, kk)),
               b_spec=pl.BlockSpec((tk, tn), lambda i, j, kk: (kk, j + off)),
               outs=outs, extras=ex, epilogue=lambda acc, bb: (acc + bb,))[0]


def mm_in_proj_dw(name, h, dz, col0, n_total, prev=None):
    parts, t, d = dz.shape
    tm, tn, tk = _mm_tiles(d, d, t)
    per = d // tn
    off = col0 // tn
    grid = (d // tm, parts * per, t // tk)
    outs = [(jax.ShapeDtypeStruct((d, n_total), BF16), pl.BlockSpec((tm, tn), lambda i, j, kk: (i, j + off)))]
    return _mm(name, h, dz, ta=True, tb=False, grid=grid, tm=tm, tn=tn, tk=tk,
               a_spec=pl.BlockSpec((tk, tm), lambda i, j, kk: (kk, i)),
               b_spec=pl.BlockSpec((None, tk, tn), lambda i, j, kk: (j // per, kk, j % per)),
               outs=outs, alias=prev)[0]


def mm_in_proj_dh(name, dz, w, col0, res=None):
    parts, t, d = dz.shape
    tm, tn, tk = _mm_tiles(t, d, d)
    per = d // tk
    off = col0 // tk
    grid = (t // tm, d // tn, parts * per)
    outs = [(jax.ShapeDtypeStruct((t, d), F32), pl.BlockSpec((tm, tn), lambda i, j, kk: (i, j)))]
    ex, epi = [], None
    if res is not None:
        ex = [(res, pl.BlockSpec((tm, tn), lambda i, j, kk: (i, j)))]
        epi = lambda acc, r: (acc + r,)
    return _mm(name, dz, w, ta=False, tb=True, grid=grid, tm=tm, tn=tn, tk=tk,
               a_spec=pl.BlockSpec((None, tm, tk), lambda i, j, kk: (kk // per, i, kk % per)),
               b_spec=pl.BlockSpec((tn, tk), lambda i, j, kk: (j, kk + off)),
               outs=outs, extras=ex, epilogue=epi)[0]


ROW_TILE = 256


def rmsnorm_fwd(name, x, g):
    t, d = x.shape
    tr = _tile(t, ROW_TILE, SUBLANES)

    def body(x_ref, g_ref, h_ref):
        xv = x_ref[...]
        rstd = lax.rsqrt(jnp.mean(xv * xv, axis=-1, keepdims=True) + EPS)
        h_ref[...] = (xv * rstd * g_ref[...]).astype(BF16)

    return pl.pallas_call(
        body, name=name, grid=(t // tr,),
        out_shape=jax.ShapeDtypeStruct((t, d), BF16),
        in_specs=[pl.BlockSpec((tr, d), lambda i: (i, 0)), pl.BlockSpec((1, d), lambda i: (0, 0))],
        out_specs=pl.BlockSpec((tr, d), lambda i: (i, 0)),
        compiler_params=pltpu.CompilerParams(dimension_semantics=("parallel",)),
    )(x, g)


def _rmsnorm_bwd_math(xv, gv, dout):
    rstd = lax.rsqrt(jnp.mean(xv * xv, axis=-1, keepdims=True) + EPS)
    xhat = xv * rstd
    dy = dout * gv
    dx = rstd * (dy - xhat * jnp.mean(dy * xhat, axis=-1, keepdims=True))
    dg = jnp.sum(dout * xhat, axis=0, keepdims=True)
    return dx, dg


def rmsnorm_bwd(name, dh, x, g, dres):
    t, d = x.shape
    tr = _tile(t, ROW_TILE, SUBLANES)

    def body(dh_ref, x_ref, g_ref, dres_ref, dx_ref, dxb_ref, dg_ref):
        dx, dg = _rmsnorm_bwd_math(x_ref[...], g_ref[...], dh_ref[...])
        dx = dx + dres_ref[...]
        dx_ref[...] = dx
        dxb_ref[...] = dx.astype(BF16)

        @pl.when(pl.program_id(0) == 0)
        def _():
            dg_ref[...] = dg

        @pl.when(pl.program_id(0) > 0)
        def _():
            dg_ref[...] += dg

    row = pl.BlockSpec((tr, d), lambda i: (i, 0))
    vec = pl.BlockSpec((1, d), lambda i: (0, 0))
    return pl.pallas_call(
        body, name=name, grid=(t // tr,),
        out_shape=[jax.ShapeDtypeStruct((t, d), F32), jax.ShapeDtypeStruct((t, d), BF16),
                   jax.ShapeDtypeStruct((1, d), F32)],
        in_specs=[row, row, vec, row], out_specs=[row, row, vec],
        compiler_params=pltpu.CompilerParams(dimension_semantics=("arbitrary",)),
    )(dh, x, g, dres)


def loss_head(name, x, g, target):
    t, d = x.shape
    tr = _tile(t, ROW_TILE, SUBLANES)

    def body(x_ref, g_ref, tgt_ref, dx_ref, dxb_ref, dg_ref, loss_ref):
        xv, gv = x_ref[...], g_ref[...]
        rstd = lax.rsqrt(jnp.mean(xv * xv, axis=-1, keepdims=True) + EPS)
        err = xv * rstd * gv - tgt_ref[...]
        lsum = (0.5 / d) * jnp.sum(err * err, axis=0, keepdims=True)
        dx, dg = _rmsnorm_bwd_math(xv, gv, err * (1.0 / d))
        dx_ref[...] = dx
        dxb_ref[...] = dx.astype(BF16)

        @pl.when(pl.program_id(0) == 0)
        def _():
            dg_ref[...] = dg
            loss_ref[...] = lsum

        @pl.when(pl.program_id(0) > 0)
        def _():
            dg_ref[...] += dg
            loss_ref[...] += lsum

    row = pl.BlockSpec((tr, d), lambda i: (i, 0))
    vec = pl.BlockSpec((1, d), lambda i: (0, 0))
    return pl.pallas_call(
        body, name=name, grid=(t // tr,),
        out_shape=[jax.ShapeDtypeStruct((t, d), F32), jax.ShapeDtypeStruct((t, d), BF16),
                   jax.ShapeDtypeStruct((1, d), F32), jax.ShapeDtypeStruct((1, d), F32)],
        in_specs=[row, vec, row], out_specs=[row, row, vec, vec],
        compiler_params=pltpu.CompilerParams(dimension_semantics=("arbitrary",)),
    )(x, g, target)


TIME_TILE = 256


def _sigmoid(v):
    return 1.0 / (1.0 + jnp.exp(-v))


def _gelu(v):
    return 0.5 * v * (1.0 + jnp.tanh(GELU_C * (v + GELU_A * v * v * v)))


def _gelu_grad(v):
    th = jnp.tanh(GELU_C * (v + GELU_A * v * v * v))
    return 0.5 * (1.0 + th) + 0.5 * v * (1.0 - th * th) * GELU_C * (1.0 + 3.0 * GELU_A * v * v)


def _one_minus_exp(v):
    series = -v * (1.0 + v * (0.5 + v * (1.0 / 6.0 + v * (1.0 / 24.0))))
    return jnp.where(v > -0.05, series, 1.0 - jnp.exp(v))


def _back(cur, halo, s, row):
    tt, cb = cur.shape
    pad = jnp.concatenate([pltpu.roll(halo, s, 0), jnp.zeros((tt - SUBLANES, cb), F32)], axis=0)
    return jnp.where(row < s, pad, pltpu.roll(cur, s, 0))


def _ahead(cur, halo, s, row):
    tt, cb = cur.shape
    pad = jnp.concatenate([jnp.zeros((tt - SUBLANES, cb), F32), pltpu.roll(halo, SUBLANES - s, 0)], axis=0)
    return jnp.where(row >= tt - s, pad, pltpu.roll(cur, tt - s, 0))


def _lru_gates(u, wr, wi, br, bi, lam):
    ub = u.astype(BF16)
    r = _sigmoid(jnp.dot(ub, wr, preferred_element_type=F32) + br)
    i = _sigmoid(jnp.dot(ub, wi, preferred_element_type=F32) + bi)
    sp = jnp.maximum(-lam, 0.0) + jnp.log1p(jnp.exp(-jnp.abs(lam)))
    log_a = -LRU_C * r * sp
    a = jnp.exp(log_a)
    mult = jnp.sqrt(_one_minus_exp(2.0 * log_a))
    return ub, r, i, a, mult


def _branch_specs(tt, cb, nt, time_of):
    per = tt // SUBLANES

    def tile(parts):
        if parts is None:
            return pl.BlockSpec((tt, cb), lambda n, s: (time_of(s), n))
        return pl.BlockSpec((parts, tt, cb), lambda n, s: (0, time_of(s), n))

    def halo_before(parts):
        if parts is None:
            return pl.BlockSpec((SUBLANES, cb), lambda n, s: (jnp.maximum(time_of(s) * per - 1, 0), n))
        return pl.BlockSpec((parts, SUBLANES, cb), lambda n, s: (0, jnp.maximum(time_of(s) * per - 1, 0), n))

    def rows(k):
        return pl.BlockSpec((k, cb), lambda n, s: (0, n))

    gate_w = pl.BlockSpec((None, cb, cb), lambda n, s: (n, 0, 0))
    return tile, halo_before, rows, gate_w


def branch_fwd(name, z5, caw, cab, cbw, wr, wi, br, bi, lam):
    _, t, d = z5.shape
    cb = wr.shape[-1]
    tt = _tile(t, TIME_TILE, SUBLANES)
    nt = t // tt
    tile, halo_before, rows, gate_w = _branch_specs(tt, cb, nt, lambda s: s)

    def body(z_ref, zh_ref, caw_ref, cab_ref, cbw_ref, wr_ref, wi_ref, br_ref, bi_ref, lam_ref,
             pa_ref, pb_ref, h_ref, carry_ref, acum_ref, bcum_ref):
        s = pl.program_id(1)
        row = lax.broadcasted_iota(jnp.int32, (tt, cb), 0)
        keep = jnp.where(s > 0, 1.0, 0.0)

        @pl.when(s == 0)
        def _():
            carry_ref[...] = jnp.zeros_like(carry_ref)

        xa, ya, cbv, ccv, cxv = (z_ref[p] for p in range(N_BRANCH))
        xa_h = zh_ref[0] * keep
        caw_v = caw_ref[...]
        u = caw_v[3:4] * xa + cab_ref[...]
        for j in range(1, CONV_A_WIDTH):
            u = u + caw_v[3 - j:4 - j] * _back(xa, xa_h, j, row)
        _, _, gi, a, mult = _lru_gates(u, wr_ref[...], wi_ref[...], br_ref[...], bi_ref[...], lam_ref[...])
        b = mult * (gi * u)

        rm = row & (SUBLANES - 1)
        for sh in (1, 2, 4):
            a_prev = jnp.where(rm >= sh, pltpu.roll(a, sh, 0), 1.0)
            b_prev = jnp.where(rm >= sh, pltpu.roll(b, sh, 0), 0.0)
            b = a * b_prev + b
            a = a * a_prev
        acum_ref[...] = a
        bcum_ref[...] = b

        def group(g, h_in):
            r0 = pl.multiple_of(g * SUBLANES, SUBLANES)
            hg = acum_ref[pl.ds(r0, SUBLANES), :] * h_in + bcum_ref[pl.ds(r0, SUBLANES), :]
            h_ref[pl.ds(r0, SUBLANES), :] = hg
            return jnp.broadcast_to(hg[SUBLANES - 1:SUBLANES, :], (SUBLANES, cb))

        carry_ref[...] = lax.fori_loop(0, tt // SUBLANES, group, carry_ref[...], unroll=4)
        pa_ref[...] = (h_ref[...] * _gelu(ya)).astype(BF16)

        q = ccv * cxv
        q_h = zh_ref[3] * zh_ref[4] * keep
        cbw_v = cbw_ref[...]
        v = cbw_v[2:3] * q
        for j in range(1, CONV_B_WIDTH):
            v = v + cbw_v[2 - j:3 - j] * _back(q, q_h, j, row)
        pb_ref[...] = (cbv * v).astype(BF16)

    act = jax.ShapeDtypeStruct((t, d), BF16)
    blocks = _nbytes((2 * N_BRANCH, tt, cb), F32) + _nbytes((2, cb, cb), BF16) + _nbytes((4, tt, cb), F32)
    return pl.pallas_call(
        body, name=name, grid=(d // cb, nt),
        out_shape=[act, act, jax.ShapeDtypeStruct((t, d), F32)],
        in_specs=[tile(N_BRANCH), halo_before(N_BRANCH), rows(CONV_A_WIDTH), rows(1), rows(CONV_B_WIDTH),
                  gate_w, gate_w, rows(1), rows(1), rows(1)],
        out_specs=[tile(None), tile(None), tile(None)],
        scratch_shapes=[pltpu.VMEM((SUBLANES, cb), F32), pltpu.VMEM((tt, cb), F32), pltpu.VMEM((tt, cb), F32)],
        compiler_params=pltpu.CompilerParams(dimension_semantics=("parallel", "arbitrary"),
                                             vmem_limit_bytes=_vmem_limit(blocks, 40 * tt * cb * 4)),
    )(z5, z5, caw, cab, cbw, wr, wi, br, bi, lam)


def branch_bwd(name, z5, hl, dpa, dpb, caw, cab, cbw, wr, wi, br, bi, lam):
    _, t, d = z5.shape
    cb = wr.shape[-1]
    tt = _tile(t, TIME_TILE, SUBLANES)
    nt = t // tt
    tile, halo_before, rows, gate_w = _branch_specs(tt, cb, nt, lambda s: nt - 1 - s)

    def body(z_ref, zh_ref, h_ref, hh_ref, dpa_ref, dpb_ref, caw_ref, cab_ref, cbw_ref, wr_ref, wi_ref,
             br_ref, bi_ref, lam_ref,
             dz_ref, dbias_ref, dcaw_ref, dcab_ref, dcbw_ref, dwr_ref, dwi_ref, dbr_ref, dbi_ref, dlam_ref,
             g_carry_ref, a_next_ref, du_next_ref, dv_next_ref, acum_ref, bcum_ref, gout_ref):
        s = pl.program_id(1)
        row = lax.broadcasted_iota(jnp.int32, (tt, cb), 0)
        keep = jnp.where(s < nt - 1, 1.0, 0.0)

        @pl.when(s == 0)
        def _():
            for r in (g_carry_ref, a_next_ref, du_next_ref, dv_next_ref, dbias_ref, dcaw_ref, dcab_ref, dcbw_ref,
                      dwr_ref, dwi_ref, dbr_ref, dbi_ref, dlam_ref):
                r[...] = jnp.zeros_like(r)

        xa, ya, cbv, ccv, cxv = (z_ref[p] for p in range(N_BRANCH))
        xa_h = zh_ref[0] * keep
        caw_v = caw_ref[...]
        wr_v, wi_v, lam_v = wr_ref[...], wi_ref[...], lam_ref[...]
        xa_back = [xa] + [_back(xa, xa_h, j, row) for j in range(1, CONV_A_WIDTH)]
        u = caw_v[3:4] * xa + cab_ref[...]
        for j in range(1, CONV_A_WIDTH):
            u = u + caw_v[3 - j:4 - j] * xa_back[j]
        ub, gr, gi, a, mult = _lru_gates(u, wr_v, wi_v, br_ref[...], bi_ref[...], lam_v)

        hv = h_ref[...]
        dpa_v = dpa_ref[...]
        dya = dpa_v * hv * _gelu_grad(ya)
        dh = dpa_v * _gelu(ya)

        a_up = _ahead(a, a_next_ref[...], 1, row)
        rm = row & (SUBLANES - 1)
        ca, cg = a_up, dh
        for sh in (1, 2, 4):
            a_nxt = jnp.where(rm + sh < SUBLANES, pltpu.roll(ca, tt - sh, 0), 1.0)
            g_nxt = jnp.where(rm + sh < SUBLANES, pltpu.roll(cg, tt - sh, 0), 0.0)
            cg = ca * g_nxt + cg
            ca = ca * a_nxt
        acum_ref[...] = ca
        bcum_ref[...] = cg
        n_groups = tt // SUBLANES

        def group(k, g_in):
            r0 = pl.multiple_of((n_groups - 1 - k) * SUBLANES, SUBLANES)
            gg = acum_ref[pl.ds(r0, SUBLANES), :] * g_in + bcum_ref[pl.ds(r0, SUBLANES), :]
            gout_ref[pl.ds(r0, SUBLANES), :] = gg
            return jnp.broadcast_to(gg[0:1, :], (SUBLANES, cb))

        g_carry_ref[...] = lax.fori_loop(0, n_groups, group, g_carry_ref[...], unroll=4)
        a_next_ref[...] = a[0:SUBLANES, :]
        gv = gout_ref[...]

        h_prev = _back(hv, hh_ref[...] * keep, 1, row)
        da = gv * h_prev
        iu = gi * u
        dmult = gv * iu
        di = gv * mult * u
        du = gv * mult * gi
        dla = da * a - dmult * (a * a) / mult
        sp = jnp.maximum(-lam_v, 0.0) + jnp.log1p(jnp.exp(-jnp.abs(lam_v)))
        dr = dla * (-LRU_C * sp)
        dlam_ref[...] += jnp.sum(dla * gr, axis=0, keepdims=True) * (LRU_C * _sigmoid(-lam_v))
        drp = dr * gr * (1.0 - gr)
        dip = di * gi * (1.0 - gi)
        dbr_ref[...] += jnp.sum(drp, axis=0, keepdims=True)
        dbi_ref[...] += jnp.sum(dip, axis=0, keepdims=True)
        drp_b, dip_b = drp.astype(BF16), dip.astype(BF16)
        tn_dims = (((0,), (0,)), ((), ()))
        nt_dims = (((1,), (1,)), ((), ()))
        dwr_ref[...] += lax.dot_general(ub, drp_b, tn_dims, preferred_element_type=F32)
        dwi_ref[...] += lax.dot_general(ub, dip_b, tn_dims, preferred_element_type=F32)
        du = du + lax.dot_general(drp_b, wr_v, nt_dims, preferred_element_type=F32)
        du = du + lax.dot_general(dip_b, wi_v, nt_dims, preferred_element_type=F32)

        du_h = du_next_ref[...]
        dxa = caw_v[3:4] * du
        dcaw_ref[3:4, :] += jnp.sum(du * xa, axis=0, keepdims=True)
        for j in range(1, CONV_A_WIDTH):
            dxa = dxa + caw_v[3 - j:4 - j] * _ahead(du, du_h, j, row)
            dcaw_ref[3 - j:4 - j, :] += jnp.sum(du * xa_back[j], axis=0, keepdims=True)
        dcab_ref[...] += jnp.sum(du, axis=0, keepdims=True)
        du_next_ref[...] = du[0:SUBLANES, :]

        q = ccv * cxv
        q_h = zh_ref[3] * zh_ref[4] * keep
        cbw_v = cbw_ref[...]
        q_back = [q] + [_back(q, q_h, j, row) for j in range(1, CONV_B_WIDTH)]
        v = cbw_v[2:3] * q
        for j in range(1, CONV_B_WIDTH):
            v = v + cbw_v[2 - j:3 - j] * q_back[j]
        dpb_v = dpb_ref[...]
        dcb = dpb_v * v
        dv = dpb_v * cbv
        dv_h = dv_next_ref[...]
        dq = cbw_v[2:3] * dv
        dcbw_ref[2:3, :] += jnp.sum(dv * q, axis=0, keepdims=True)
        for j in range(1, CONV_B_WIDTH):
            dq = dq + cbw_v[2 - j:3 - j] * _ahead(dv, dv_h, j, row)
            dcbw_ref[2 - j:3 - j, :] += jnp.sum(dv * q_back[j], axis=0, keepdims=True)
        dv_next_ref[...] = dv[0:SUBLANES, :]
        dcc = dq * cxv
        dcx = dq * ccv

        for p, val in enumerate((dxa, dya, dcb, dcc, dcx)):
            dz_ref[p] = val.astype(BF16)
            dbias_ref[p:p + 1, :] += jnp.sum(val, axis=0, keepdims=True)

    def acc_rows(k):
        return jax.ShapeDtypeStruct((k, d), F32)

    nb = d // cb
    gate_grad = jax.ShapeDtypeStruct((nb, cb, cb), F32)
    small = pltpu.VMEM((SUBLANES, cb), F32)
    big = pltpu.VMEM((tt, cb), F32)
    blocks = _nbytes((2 * (N_BRANCH + 3), tt, cb), F32) + _nbytes((4, cb, cb), F32)
    return pl.pallas_call(
        body, name=name, grid=(nb, nt),
        out_shape=[jax.ShapeDtypeStruct((N_BRANCH, t, d), BF16), acc_rows(N_BRANCH), acc_rows(CONV_A_WIDTH),
                   acc_rows(1), acc_rows(CONV_B_WIDTH), gate_grad, gate_grad, acc_rows(1), acc_rows(1), acc_rows(1)],
        in_specs=[tile(N_BRANCH), halo_before(N_BRANCH), tile(None), halo_before(None), tile(None), tile(None),
                  rows(CONV_A_WIDTH), rows(1), rows(CONV_B_WIDTH), gate_w, gate_w, rows(1), rows(1), rows(1)],
        out_specs=[tile(N_BRANCH), rows(N_BRANCH), rows(CONV_A_WIDTH), rows(1), rows(CONV_B_WIDTH),
                   gate_w, gate_w, rows(1), rows(1), rows(1)],
        scratch_shapes=[small, small, small, small, big, big, big],
        compiler_params=pltpu.CompilerParams(dimension_semantics=("parallel", "arbitrary"),
                                             vmem_limit_bytes=_vmem_limit(blocks, 60 * tt * cb * 4)),
    )(z5, z5, hl, hl, dpa, dpb, caw, cab, cbw, wr, wi, br, bi, lam)


MERGE_COLS = 512


def merge_fwd(name, z2, oa, ob):
    t, d = oa.shape
    tr, tc = _tile(t, ROW_TILE, SUBLANES), _tile(d, MERGE_COLS)

    def body(z_ref, oa_ref, ob_ref, m_ref):
        m_ref[...] = (_sigmoid(z_ref[0]) * oa_ref[...] + _sigmoid(z_ref[1]) * ob_ref[...]).astype(BF16)

    blk = pl.BlockSpec((tr, tc), lambda i, j: (i, j))
    return pl.pallas_call(
        body, name=name, grid=(t // tr, d // tc),
        out_shape=jax.ShapeDtypeStruct((t, d), BF16),
        in_specs=[pl.BlockSpec((2, tr, tc), lambda i, j: (0, i, j)), blk, blk], out_specs=blk,
        compiler_params=pltpu.CompilerParams(dimension_semantics=("parallel", "parallel")),
    )(z2, oa, ob)


def merge_bwd(name, dm, z2, oa, ob):
    t, d = oa.shape
    tr, tc = _tile(t, ROW_TILE, SUBLANES), _tile(d, MERGE_COLS)

    def body(dm_ref, z_ref, oa_ref, ob_ref, doa_ref, dob_ref, dz_ref, dbias_ref):
        dmv = dm_ref[...]
        sa, sb = _sigmoid(z_ref[0]), _sigmoid(z_ref[1])
        doa_ref[...] = (dmv * sa).astype(BF16)
        dob_ref[...] = (dmv * sb).astype(BF16)
        dga = dmv * oa_ref[...] * sa * (1.0 - sa)
        dgb = dmv * ob_ref[...] * sb * (1.0 - sb)
        dz_ref[0] = dga.astype(BF16)
        dz_ref[1] = dgb.astype(BF16)

        @pl.when(pl.program_id(1) == 0)
        def _():
            dbias_ref[...] = jnp.zeros_like(dbias_ref)

        dbias_ref[0:1, :] += jnp.sum(dga, axis=0, keepdims=True)
        dbias_ref[1:2, :] += jnp.sum(dgb, axis=0, keepdims=True)

    blk = pl.BlockSpec((tr, tc), lambda j, i: (i, j))
    pair = pl.BlockSpec((2, tr, tc), lambda j, i: (0, i, j))
    act = jax.ShapeDtypeStruct((t, d), BF16)
    return pl.pallas_call(
        body, name=name, grid=(d // tc, t // tr),
        out_shape=[act, act, jax.ShapeDtypeStruct((2, t, d), BF16), jax.ShapeDtypeStruct((2, d), F32)],
        in_specs=[blk, pair, blk, blk],
        out_specs=[blk, blk, pair, pl.BlockSpec((2, tc), lambda j, i: (0, j))],
        compiler_params=pltpu.CompilerParams(dimension_semantics=("parallel", "arbitrary")),
    )(dm, z2, oa, ob)


def _adamw_math(w, g, m, v):
    m = ADAM_B1 * m + (1.0 - ADAM_B1) * g
    v = ADAM_B2 * v + (1.0 - ADAM_B2) * (g * g)
    m_hat = m / (1.0 - ADAM_B1 ** ADAM_STEP)
    v_hat = v / (1.0 - ADAM_B2 ** ADAM_STEP)
    delta = -ADAM_LR * (m_hat / (jnp.sqrt(v_hat) + ADAM_EPS) + ADAM_WD * w)
    return delta, m, v


def _sum_in_order(ref):
    g = ref[0].astype(F32)
    for p in range(1, ref.shape[0]):
        g = g + ref[p].astype(F32)
    return g


def adamw_sharded(name, recv, w, m, v):
    r, c = w.shape
    tr = _tile(r, max(SUBLANES, (256 * 1024) // c), 16)

    def body(recv_ref, w_ref, m_ref, v_ref, g_ref, d_ref, mo_ref, vo_ref):
        g = _sum_in_order(recv_ref)
        delta, m2, v2 = _adamw_math(w_ref[...], g, m_ref[...], v_ref[...])
        g_ref[...] = g
        d_ref[...] = delta
        mo_ref[...] = m2
        vo_ref[...] = v2

    blk = pl.BlockSpec((tr, c), lambda i: (i, 0))
    out = jax.ShapeDtypeStruct((r, c), F32)
    blocks = _nbytes((N_CHIP, tr, c), recv.dtype) + 7 * _nbytes((tr, c), F32)
    return pl.pallas_call(
        body, name=name, grid=(r // tr,),
        out_shape=[out, out, out, out],
        in_specs=[pl.BlockSpec((N_CHIP, tr, c), lambda i: (0, i, 0)), blk, blk, blk],
        out_specs=[blk, blk, blk, blk],
        compiler_params=pltpu.CompilerParams(dimension_semantics=("parallel",),
                                             vmem_limit_bytes=_vmem_limit(blocks)),
    )(recv, w, m, v)


def sum_partials(name, parts):
    _, r, c = parts.shape

    def body(p_ref, o_ref):
        o_ref[...] = _sum_in_order(p_ref)

    return pl.pallas_call(body, name=name, out_shape=jax.ShapeDtypeStruct((r, c), F32))(parts)


def adamw_small(name, g, w, m, v):
    def body(g_ref, w_ref, m_ref, v_ref, d_ref, mo_ref, vo_ref):
        delta, m2, v2 = _adamw_math(w_ref[...], g_ref[...], m_ref[...], v_ref[...])
        d_ref[...] = delta
        mo_ref[...] = m2
        vo_ref[...] = v2

    out = jax.ShapeDtypeStruct(w.shape, F32)
    return pl.pallas_call(body, name=name, out_shape=[out, out, out])(g, w, m, v)


def _pack(arrays):
    flat = jnp.concatenate([a.reshape(-1) for a in arrays])
    n = flat.shape[0]
    rows = -(-n // (SUBLANES * LANES)) * SUBLANES
    flat = jnp.pad(flat, (0, rows * LANES - n))
    return flat.reshape(rows, LANES)


def _unpack(packed, shapes):
    flat = packed.reshape(-1)
    out, off = [], 0
    for s in shapes:
        n = math.prod(s)
        out.append(flat[off:off + n].reshape(s))
        off += n
    return out


def kernel(x, norm1_g, w_in, b_in, conv_a_w, conv_a_b, lru_wr, lru_br, lru_wi, lru_bi, lru_lam, conv_b_w, w_pa, w_pb, w_o, norm2_g, w_mlp1, w_mlp2, final_g, loss_target, m_norm1_g, m_w_in, m_b_in, m_conv_a_w, m_conv_a_b, m_lru_wr, m_lru_br, m_lru_wi, m_lru_bi, m_lru_lam, m_conv_b_w, m_w_pa, m_w_pb, m_w_o, m_norm2_g, m_w_mlp1, m_w_mlp2, m_final_g, v_norm1_g, v_w_in, v_b_in, v_conv_a_w, v_conv_a_b, v_lru_wr, v_lru_br, v_lru_wi, v_lru_bi, v_lru_lam, v_conv_b_w, v_w_pa, v_w_pb, v_w_o, v_norm2_g, v_w_mlp1, v_w_mlp2, v_final_g):
    weights = dict(norm1_g=norm1_g, w_in=w_in, b_in=b_in, conv_a_w=conv_a_w, conv_a_b=conv_a_b, lru_wr=lru_wr,
                   lru_br=lru_br, lru_wi=lru_wi, lru_bi=lru_bi, lru_lam=lru_lam, conv_b_w=conv_b_w, w_pa=w_pa,
                   w_pb=w_pb, w_o=w_o, norm2_g=norm2_g, w_mlp1=w_mlp1, w_mlp2=w_mlp2, final_g=final_g)
    mom1 = dict(norm1_g=m_norm1_g, w_in=m_w_in, b_in=m_b_in, conv_a_w=m_conv_a_w, conv_a_b=m_conv_a_b,
                lru_wr=m_lru_wr, lru_br=m_lru_br, lru_wi=m_lru_wi, lru_bi=m_lru_bi, lru_lam=m_lru_lam,
                conv_b_w=m_conv_b_w, w_pa=m_w_pa, w_pb=m_w_pb, w_o=m_w_o, norm2_g=m_norm2_g, w_mlp1=m_w_mlp1,
                w_mlp2=m_w_mlp2, final_g=m_final_g)
    mom2 = dict(norm1_g=v_norm1_g, w_in=v_w_in, b_in=v_b_in, conv_a_w=v_conv_a_w, conv_a_b=v_conv_a_b,
                lru_wr=v_lru_wr, lru_br=v_lru_br, lru_wi=v_lru_wi, lru_bi=v_lru_bi, lru_lam=v_lru_lam,
                conv_b_w=v_conv_b_w, w_pa=v_w_pa, w_pb=v_w_pb, w_o=v_w_o, norm2_g=v_norm2_g, w_mlp1=v_w_mlp1,
                w_mlp2=v_w_mlp2, final_g=v_final_g)
    names = list(weights)

    t, d = x.shape[1], x.shape[2]
    n_in = b_in.shape[1]
    nb, bw = lru_wr.shape[1], lru_wr.shape[3]
    me = _flat(*_my_coords())
    x0 = x.reshape(t, d)
    target = loss_target.reshape(t, d)

    big = ["w_in", "w_pa", "w_pb", "w_o", "w_mlp1", "w_mlp2", "lru_wr", "lru_wi"]
    big_axis = dict(w_in=1, w_pa=0, w_pb=0, w_o=0, w_mlp1=1, w_mlp2=0, lru_wr=1, lru_wi=1)
    small_sharded = ["conv_a_w", "conv_b_w", "lru_br", "lru_bi"]
    small_packed = _pack([weights[k] for k in small_sharded])
    full = []
    for l in range(DEPTH):
        shards = [weights[k][l].astype(BF16) for k in big] + ([small_packed[None]] if l == 0 else [])
        axes = [big_axis[k] for k in big] + ([0] if l == 0 else [])
        gathered = all_gather_by_chip(f"gather_weights_l{l}", shards, axes)
        full.append(dict(zip(big, gathered)))
        if l == 0:
            small_rows = gathered[-1].reshape(N_DEV, -1)
    small_full = {}
    off = 0
    for k in small_sharded:
        shard_shape = weights[k].shape
        n = math.prod(shard_shape)
        g = small_rows[:, off:off + n].reshape(N_DEV, *shard_shape)
        small_full[k] = jnp.moveaxis(g, 0, 2).reshape(shard_shape[0], shard_shape[1], N_DEV * shard_shape[2])
        off += n

    def layer_params(l):
        f = full[l]
        return dict(
            g1=norm1_g[l][None], w_in=f["w_in"], b_in=b_in[l][None], caw=small_full["conv_a_w"][l],
            cab=conv_a_b[l][None], wr=f["lru_wr"], wi=f["lru_wi"],
            br=small_full["lru_br"][l].reshape(1, d), bi=small_full["lru_bi"][l].reshape(1, d),
            lam=lru_lam[l][None], cbw=small_full["conv_b_w"][l], w_pa=f["w_pa"], w_pb=f["w_pb"], w_o=f["w_o"],
            g2=norm2_g[l][None], w1=f["w_mlp1"], w2=f["w_mlp2"])

    params = [layer_params(l) for l in range(DEPTH)]

    relu2 = lambda acc: (acc, jnp.square(jnp.maximum(acc, 0.0)))
    add_res = lambda acc, r: (acc + r,)
    relu2_bwd = lambda acc, pre: (acc * (2.0 * jnp.maximum(pre, 0.0)),)

    saved = []
    xl = x0
    for l in range(DEPTH):
        p = params[l]
        tag = f"l{l}"
        h1 = rmsnorm_fwd(f"norm1_{tag}", xl, p["g1"])
        z5 = mm_in_proj(f"in_proj5_{tag}", h1, p["w_in"], p["b_in"], 0, N_BRANCH)
        z2 = mm_in_proj(f"in_proj2_{tag}", h1, p["w_in"], p["b_in"], N_BRANCH * d, N_SPLIT - N_BRANCH)
        pa, pb, hl = branch_fwd(f"branch_fwd_{tag}", z5, p["caw"], p["cab"], p["cbw"], p["wr"], p["wi"],
                                p["br"], p["bi"], p["lam"])
        oa = mm_plain(f"proj_a_{tag}", pa, p["w_pa"])
        ob = mm_plain(f"proj_b_{tag}", pb, p["w_pb"])
        mg = merge_fwd(f"merge_{tag}", z2, oa, ob)
        x1 = mm_plain(f"proj_o_{tag}", mg, p["w_o"], epilogue=add_res, extras=(xl,), extra_kinds=("full",))
        h2 = rmsnorm_fwd(f"norm2_{tag}", x1, p["g2"])
        pre, uu = mm_plain(f"mlp1_{tag}", h2, p["w1"], epilogue=relu2, out_dtypes=[F32, BF16])
        x2 = mm_plain(f"mlp2_{tag}", uu, p["w2"], epilogue=add_res, extras=(x1,), extra_kinds=("full",))
        saved.append(dict(x0=xl, h1=h1, z5=z5, z2=z2, pa=pa, pb=pb, hl=hl, oa=oa, ob=ob, mg=mg, x1=x1, h2=h2,
                          pre=pre, uu=uu))
        xl = x2

    dx, dxb, d_final_g, loss_cols = loss_head("loss_head", xl, final_g[None], target)
    loss = lax.psum(jnp.sum(loss_cols), ("x", "y", "c"))

    small_grads = {}
    recv = None
    for l in reversed(range(DEPTH)):
        p, sv = params[l], saved[l]
        tag = f"l{l}"
        dpre = mm_plain(f"mlp2_dx_{tag}", dxb, p["w2"], tb=True, out_dtype=BF16, epilogue=relu2_bwd,
                        extras=(sv["pre"],), extra_kinds=("full",))
        dw2 = mm_plain(f"mlp2_dw_{tag}", sv["uu"], dxb, ta=True, out_dtype=BF16)
        dw1 = mm_plain(f"mlp1_dw_{tag}", sv["h2"], dpre, ta=True, out_dtype=BF16)
        dh2 = mm_plain(f"mlp1_dx_{tag}", dpre, p["w1"], tb=True)
        dx1, dx1b, dg2 = rmsnorm_bwd(f"norm2_bwd_{tag}", dh2, sv["x1"], p["g2"], dx)
        dmg = mm_plain(f"proj_o_dx_{tag}", dx1b, p["w_o"], tb=True)
        dwo = mm_plain(f"proj_o_dw_{tag}", sv["mg"], dx1b, ta=True, out_dtype=BF16)
        doa, dob, dz2, dbias2 = merge_bwd(f"merge_bwd_{tag}", dmg, sv["z2"], sv["oa"], sv["ob"])
        dpa = mm_plain(f"proj_a_dx_{tag}", doa, p["w_pa"], tb=True)
        dpb = mm_plain(f"proj_b_dx_{tag}", dob, p["w_pb"], tb=True)
        dwpa = mm_plain(f"proj_a_dw_{tag}", sv["pa"], doa, ta=True, out_dtype=BF16)
        dwpb = mm_plain(f"proj_b_dw_{tag}", sv["pb"], dob, ta=True, out_dtype=BF16)
        (dz5, dbias5, dcaw, dcab, dcbw, dwr, dwi, dbr, dbi, dlam) = branch_bwd(
            f"branch_bwd_{tag}", sv["z5"], sv["hl"], dpa, dpb, p["caw"], p["cab"], p["cbw"], p["wr"], p["wi"],
            p["br"], p["bi"], p["lam"])
        dwin = mm_in_proj_dw(f"in_proj5_dw_{tag}", sv["h1"], dz5, 0, n_in)
        dwin = mm_in_proj_dw(f"in_proj2_dw_{tag}", sv["h1"], dz2, N_BRANCH * d, n_in, prev=dwin)
        dh1 = mm_in_proj_dh(f"in_proj5_dx_{tag}", dz5, p["w_in"], 0)
        dh1 = mm_in_proj_dh(f"in_proj2_dx_{tag}", dz2, p["w_in"], N_BRANCH * d, res=dh1)
        dx, dxb, dg1 = rmsnorm_bwd(f"norm1_bwd_{tag}", dh1, sv["x0"], p["g1"], dx1)

        layer_grads = [dwin, dwpa, dwpb, dwo, dw1, dw2, dwr.astype(BF16), dwi.astype(BF16)]
        layer_axes = [1, 0, 0, 0, 1, 0, 1, 1]
        halves = pair_exchange(f"pair_exchange_{tag}", layer_grads, layer_axes)
        sums = [chip_sum(f"chip_sum_{k}_{tag}", a, b) for k, a, b in zip(big, halves[:len(big)], halves[len(big):])]
        recv = chip_exchange(f"chip_exchange_{tag}", sums, l, prev=recv)
        small_grads[l] = dict(norm1_g=dg1, b_in=jnp.concatenate([dbias5.reshape(1, -1), dbias2.reshape(1, -1)], axis=1),
                              conv_a_b=dcab, lru_lam=dlam, norm2_g=dg2, conv_a_w=dcaw, conv_b_w=dcbw,
                              lru_br=dbr, lru_bi=dbi)
    grad_x = dx.reshape(x.shape)

    replicated = ["norm1_g", "b_in", "conv_a_b", "lru_lam", "norm2_g"]
    small_list = [jnp.stack([small_grads[l][k] for l in range(DEPTH)]) for k in replicated + small_sharded]
    small_list.append(d_final_g)
    packed_partial = _pack(small_list)
    partials = all_gather("gather_small_grads", [packed_partial[None]], [0])[0]
    summed = sum_partials("sum_small_grads", partials)
    pieces = _unpack(summed, [a.shape for a in small_list])
    small_sum = dict(zip(replicated + small_sharded + ["final_g"], pieces))

    grads = {}
    for k in replicated:
        grads[k] = small_sum[k].reshape(weights[k].shape)
    grads["final_g"] = small_sum["final_g"].reshape(weights["final_g"].shape)
    for k in small_sharded:
        shard_shape = weights[k].shape
        g = small_sum[k].reshape(shard_shape[0], shard_shape[1], N_DEV, shard_shape[2])
        grads[k] = lax.dynamic_index_in_dim(g, me, axis=2, keepdims=False)

    small_names = replicated + ["final_g"] + small_sharded
    small_shapes = [weights[k].shape for k in small_names]
    pk = lambda src: _pack([src[k] for k in small_names])
    sd, sm, sv2 = adamw_small("adamw_small", pk(grads), pk(weights), pk(mom1), pk(mom2))
    delta = dict(zip(small_names, _unpack(sd, small_shapes)))
    new_m = dict(zip(small_names, _unpack(sm, small_shapes)))
    new_v = dict(zip(small_names, _unpack(sv2, small_shapes)))

    for k, rc in zip(big, recv):
        shape = weights[k].shape
        c = shape[-1]
        flat = lambda a: a.reshape(-1, c)
        g, dl, m2, v2 = adamw_sharded(f"adamw_{k}", rc.reshape(N_CHIP, -1, c), flat(weights[k]), flat(mom1[k]),
                                      flat(mom2[k]))
        grads[k], delta[k], new_m[k], new_v[k] = (a.reshape(shape) for a in (g, dl, m2, v2))

    return (loss, grad_x, *[grads[k] for k in names], *[delta[k] for k in names],
            *[new_m[k] for k in names], *[new_v[k] for k in names])
```

```python
import functools
import math

import jax
import jax.numpy as jnp
from jax import lax
from jax.experimental import pallas as pl
from jax.experimental.pallas import tpu as pltpu

F32 = jnp.float32
BF16 = jnp.bfloat16

N_DEV = 8
DEPTH = 2
N_SPLIT = 7
N_BRANCH = 5
EPS = 1e-6
LRU_C = 8.0
CONV_A_WIDTH = 4
CONV_B_WIDTH = 3
GELU_C = math.sqrt(2.0 / math.pi)
GELU_A = 0.044715

ADAM_LR = 0.001
ADAM_B1 = 0.9
ADAM_B2 = 0.999
ADAM_EPS = 1e-08
ADAM_WD = 0.01
ADAM_STEP = 10

LANES = 128
SUBLANES = 8
VMEM_LIMIT_CAP = 56 * 2 ** 20
MESH = pl.DeviceIdType.MESH
ANY = pl.BlockSpec(memory_space=pl.ANY)


def _tile(n, target, align=LANES):
    if n <= target:
        return n
    t = (target // align) * align
    while t >= align:
        if n % t == 0:
            return t
        t -= align
    return n


def _nbytes(shape, dtype):
    return math.prod(shape) * jnp.dtype(dtype).itemsize


def _vmem_limit(block_bytes, scratch_bytes=0):
    est = 2 * block_bytes + scratch_bytes
    return int(min(max(2 * est, 32 * 2 ** 20), VMEM_LIMIT_CAP))


def _my_coords():
    return lax.axis_index("x"), lax.axis_index("y"), lax.axis_index("c")


def _flat(x, y, c):
    return 4 * x + 2 * y + c


def _peer(k, x, y, c):
    return (1 - x if k & 4 else x, 1 - y if k & 2 else y, 1 - c if k & 1 else c)


def _slab(ref, axis, start, size):
    idx = tuple(pl.ds(start, size) if d == axis else slice(None) for d in range(len(ref.shape)))
    return ref.at[idx]


def all_gather(name, shards, axes):
    n = len(shards)
    sizes = [s.shape[a] for s, a in zip(shards, axes)]
    out_shape = []
    for s, a in zip(shards, axes):
        full = list(s.shape)
        full[a] *= N_DEV
        out_shape.append(jax.ShapeDtypeStruct(tuple(full), s.dtype))

    def body(*refs):
        srcs, outs = refs[:n], refs[n:2 * n]
        send_sems, recv_sems, local_sems = refs[2 * n:]
        x, y, c = _my_coords()
        me = _flat(x, y, c)
        local = []
        for i in range(n):
            cp = pltpu.make_async_copy(srcs[i], _slab(outs[i], axes[i], me * sizes[i], sizes[i]), local_sems.at[i])
            cp.start()
            local.append(cp)
        remote = []
        for k in range(1, N_DEV):
            for i in range(n):
                mine = _slab(outs[i], axes[i], me * sizes[i], sizes[i])
                cp = pltpu.make_async_remote_copy(
                    src_ref=srcs[i], dst_ref=mine,
                    send_sem=send_sems.at[(k - 1) * n + i], recv_sem=recv_sems.at[(k - 1) * n + i],
                    device_id=_peer(k, x, y, c), device_id_type=MESH)
                cp.start()
                remote.append(cp)
        for cp in remote:
            cp.wait()
        for cp in local:
            cp.wait()

    return pl.pallas_call(
        body, name=name, out_shape=out_shape,
        in_specs=[ANY] * n, out_specs=[ANY] * n,
        scratch_shapes=[pltpu.SemaphoreType.DMA(((N_DEV - 1) * n,)),
                        pltpu.SemaphoreType.DMA(((N_DEV - 1) * n,)),
                        pltpu.SemaphoreType.DMA((n,))],
    )(*shards)


N_CHIP = 4
OTHER_CHIPS = ((1, 0), (0, 1), (1, 1))


HBM = pl.BlockSpec(memory_space=pltpu.HBM)
SEM = pl.BlockSpec(memory_space=pltpu.SEMAPHORE)
TOKEN = jax.ShapeDtypeStruct((SUBLANES, LANES), F32)


def _split_copy_params():
    return pltpu.CompilerParams(has_side_effects=pltpu.SideEffectType.DATAFLOW_SIDE_EFFECTING)


def _in_hbm(a):
    return pltpu.with_memory_space_constraint(a, pltpu.HBM)


def _other_chips(x, y):
    return [(x + dx - 2 * x * dx, y + dy - 2 * y * dy) for dx, dy in OTHER_CHIPS]


def place_shard(name, w, layer, axis, me):
    shard = w.shape[1:]
    if len(shard) == 2 and axis == 1:
        r, cs = shard
        tr = _tile(r, max(SUBLANES, (512 * 1024) // cs), 16)
        grid, full = (r // tr,), (r, N_DEV * cs)
        in_spec = pl.BlockSpec((None, tr, cs), lambda i, me_ref: (layer, i, 0))
        out_spec = pl.BlockSpec((tr, cs), lambda i, me_ref: (i, me_ref[0]))
    elif len(shard) == 2 and axis == 0:
        rs, cs = shard
        tr = _tile(rs, max(SUBLANES, (512 * 1024) // cs), 16)
        per = rs // tr
        grid, full = (per,), (N_DEV * rs, cs)
        in_spec = pl.BlockSpec((None, tr, cs), lambda i, me_ref: (layer, i, 0))
        out_spec = pl.BlockSpec((tr, cs), lambda i, me_ref: (me_ref[0] * per + i, 0))
    else:
        assert len(shard) == 3 and axis == 1
        lead, rs, cs = shard
        grid, full = (1,), (lead, N_DEV * rs, cs)
        in_spec = pl.BlockSpec((None, lead, rs, cs), lambda i, me_ref: (layer, 0, 0, 0))
        out_spec = pl.BlockSpec((lead, rs, cs), lambda i, me_ref: (0, me_ref[0], 0))

    def body(me_ref, w_ref, o_ref):
        o_ref[...] = w_ref[...].astype(BF16)

    return pl.pallas_call(
        body, name=name, out_shape=jax.ShapeDtypeStruct(full, BF16),
        grid_spec=pltpu.PrefetchScalarGridSpec(num_scalar_prefetch=1, grid=grid, in_specs=[in_spec],
                                               out_specs=out_spec),
        compiler_params=pltpu.CompilerParams(dimension_semantics=("parallel",)),
    )(me, w)


class _GatherCopies:
    def __init__(self, land, axes):
        self.land, self.axes, self.n = land, axes, len(land)
        self.sizes = [r.shape[a] // N_DEV for r, a in zip(land, axes)]
        x, y, c = _my_coords()
        self.c = c
        self.me, self.sibling = (x, y, c), (x, y, 1 - c)
        self.chips = _other_chips(x, y)

    def copy(self, sems, slot, i, block, to):
        rows = _slab(self.land[i], self.axes[i], _flat(*block) * self.sizes[i], self.sizes[i])
        return pltpu.make_async_remote_copy(
            src_ref=rows, dst_ref=rows, send_sem=sems[0].at[slot * self.n + i], recv_sem=sems[1].at[slot * self.n + i],
            device_id=to, device_id_type=MESH)

    def own(self, sems):
        out = []
        for i in range(self.n):
            out.append(self.copy(sems, 0, i, self.me, self.sibling))
            out += [self.copy(sems, 1 + j, i, self.me, (*chip, self.c)) for j, chip in enumerate(self.chips)]
        return out

    def landed_over_ici(self, sems):
        return [self.copy(sems, 1 + j, i, (*chip, self.c), self.me)
                for j, chip in enumerate(self.chips) for i in range(self.n)]

    def passed_on(self, sems):
        return [self.copy(sems, j, i, (*chip, self.c), self.sibling)
                for j, chip in enumerate(self.chips) for i in range(self.n)]

    def from_sibling(self, own_sems, pass_sems):
        out = [self.copy(own_sems, 0, i, self.sibling, self.me) for i in range(self.n)]
        out += [self.copy(pass_sems, j, i, (*chip, 1 - self.c), self.me)
                for j, chip in enumerate(self.chips) for i in range(self.n)]
        return out


def gather_start(name, land, axes, after=None):
    n = len(land)
    order = [] if after is None else [after]
    n_in = n + len(order)

    def body(*refs):
        own_sems, token = (refs[n_in], refs[n_in + 1]), refs[n_in + n + 2]
        for cp in _GatherCopies(refs[:n], axes).own(own_sems):
            cp.start()
        token[...] = jnp.zeros_like(token)

    out = pl.pallas_call(
        body, name=name,
        out_shape=(pltpu.SemaphoreType.DMA((4 * n,)), pltpu.SemaphoreType.DMA((4 * n,)),
                   *[pltpu.HBM(a.shape, a.dtype) for a in land], TOKEN),
        in_specs=[HBM] * n + [ANY] * len(order),
        out_specs=(SEM, SEM, *[HBM] * n, pl.BlockSpec(memory_space=pltpu.VMEM)),
        input_output_aliases={i: 2 + i for i in range(n)}, compiler_params=_split_copy_params(),
    )(*[_in_hbm(a) for a in land], *order)
    return (out[0], out[1]), list(out[2:2 + n]), out[-1]


def gather_pass_on(name, land, axes, own_sems, after):
    n = len(land)

    def body(*refs):
        sems_in, pass_sems = (refs[n], refs[n + 1]), (refs[n + 3], refs[n + 4])
        copies = _GatherCopies(refs[:n], axes)
        for arrived, onward in zip(copies.landed_over_ici(sems_in), copies.passed_on(pass_sems)):
            arrived.wait_recv()
            onward.start()

    out = pl.pallas_call(
        body, name=name,
        out_shape=(pltpu.SemaphoreType.DMA((3 * n,)), pltpu.SemaphoreType.DMA((3 * n,)),
                   *[pltpu.HBM(a.shape, a.dtype) for a in land]),
        in_specs=[HBM] * n + [SEM, SEM, ANY], out_specs=(SEM, SEM, *[HBM] * n),
        input_output_aliases={i: 2 + i for i in range(n)}, compiler_params=_split_copy_params(),
    )(*land, *own_sems, after)
    return (out[0], out[1]), list(out[2:])


def gather_finish(name, land, axes, own_sems, pass_sems, after):
    n = len(land)

    def body(*refs):
        own, passed = (refs[n], refs[n + 1]), (refs[n + 2], refs[n + 3])
        copies = _GatherCopies(refs[:n], axes)
        for cp in copies.from_sibling(own, passed):
            cp.wait_recv()
        for cp in copies.own(own) + copies.passed_on(passed):
            cp.wait_send()

    return list(pl.pallas_call(
        body, name=name, out_shape=tuple(pltpu.HBM(a.shape, a.dtype) for a in land),
        in_specs=[HBM] * n + [SEM] * 4 + [ANY], out_specs=tuple([HBM] * n),
        input_output_aliases={i: i for i in range(n)}, compiler_params=_split_copy_params(),
    )(*land, *own_sems, *pass_sems, after))


def pair_exchange(name, grads, axes):
    n = len(grads)
    sizes = [g.shape[a] // N_DEV for g, a in zip(grads, axes)]
    out_shape = []
    for g, a, sz in zip(grads, axes, sizes):
        shard = list(g.shape)
        shard[a] = sz
        out_shape.append(jax.ShapeDtypeStruct((N_CHIP, *shard), g.dtype))

    def body(*refs):
        srcs, theirs = refs[:n], refs[n:2 * n]
        send_sems, recv_sems = refs[2 * n:]
        x, y, c = _my_coords()
        remote = []
        for i in range(n):
            for q in range(N_CHIP):
                k = i * N_CHIP + q
                give = _slab(srcs[i], axes[i], (2 * q + 1 - c) * sizes[i], sizes[i])
                remote.append(pltpu.make_async_remote_copy(
                    src_ref=give, dst_ref=theirs[i].at[q], send_sem=send_sems.at[k], recv_sem=recv_sems.at[k],
                    device_id=(x, y, 1 - c), device_id_type=MESH))
        for cp in remote:
            cp.start()
        for cp in remote:
            cp.wait()

    return pl.pallas_call(
        body, name=name, out_shape=out_shape,
        in_specs=[ANY] * n, out_specs=[ANY] * n,
        scratch_shapes=[pltpu.SemaphoreType.DMA((N_CHIP * n,)), pltpu.SemaphoreType.DMA((N_CHIP * n,))],
    )(*grads)


def chip_sum(name, grad, axis, theirs, core):
    if grad.ndim == 2 and axis == 1:
        r, cs = grad.shape[0], grad.shape[1] // N_DEV
        tr = _tile(r, max(SUBLANES, (512 * 1024) // cs), 16)
        grid = (N_CHIP, r // tr)
        g_spec = pl.BlockSpec((tr, cs), lambda q, i, core_ref: (i, 2 * q + core_ref[0]))
        t_spec = pl.BlockSpec((None, tr, cs), lambda q, i, core_ref: (q, i, 0))
    elif grad.ndim == 2 and axis == 0:
        rs, cs = grad.shape[0] // N_DEV, grad.shape[1]
        tr = _tile(rs, max(SUBLANES, (512 * 1024) // cs), 16)
        per = rs // tr
        grid = (N_CHIP, per)
        g_spec = pl.BlockSpec((tr, cs), lambda q, i, core_ref: ((2 * q + core_ref[0]) * per + i, 0))
        t_spec = pl.BlockSpec((None, tr, cs), lambda q, i, core_ref: (q, i, 0))
    else:
        assert grad.ndim == 3 and axis == 1
        lead, rs, cs = grad.shape[0], grad.shape[1] // N_DEV, grad.shape[2]
        grid = (N_CHIP, 1)
        g_spec = pl.BlockSpec((lead, rs, cs), lambda q, i, core_ref: (0, 2 * q + core_ref[0], 0))
        t_spec = pl.BlockSpec((None, lead, rs, cs), lambda q, i, core_ref: (q, 0, 0, 0))

    def body(core_ref, g_ref, t_ref, o_ref):
        o_ref[...] = (g_ref[...].astype(F32) + t_ref[...].astype(F32)).astype(o_ref.dtype)

    return pl.pallas_call(
        body, name=name, out_shape=jax.ShapeDtypeStruct(theirs.shape, theirs.dtype),
        grid_spec=pltpu.PrefetchScalarGridSpec(num_scalar_prefetch=1, grid=grid, in_specs=[g_spec, t_spec],
                                               out_specs=t_spec),
        compiler_params=pltpu.CompilerParams(dimension_semantics=("parallel", "parallel")),
    )(core, grad, theirs)


def _chip_copies(sums, recv, sems):
    n = len(sums)
    x, y, c = _my_coords()
    out = []
    for j, (tx, ty) in enumerate(_other_chips(x, y)):
        for i in range(n):
            out.append(pltpu.make_async_remote_copy(
                src_ref=sums[i].at[2 * tx + ty], dst_ref=recv[i].at[2 * x + y],
                send_sem=sems[0].at[j * n + i], recv_sem=sems[1].at[j * n + i],
                device_id=(tx, ty, c), device_id_type=MESH))
    return out


def chip_exchange_start(name, sums):
    n = len(sums)

    def body(*refs):
        sems, token = (refs[2 * n], refs[2 * n + 1]), refs[4 * n + 2]
        for cp in _chip_copies(refs[:n], refs[n:2 * n], sems):
            cp.start()
        token[...] = jnp.zeros_like(token)

    both = [pltpu.HBM(a.shape, a.dtype) for a in sums]
    out = pl.pallas_call(
        body, name=name,
        out_shape=(pltpu.SemaphoreType.DMA((3 * n,)), pltpu.SemaphoreType.DMA((3 * n,)), *both, *both, TOKEN),
        in_specs=[HBM] * (2 * n), out_specs=(SEM, SEM, *[HBM] * (2 * n), pl.BlockSpec(memory_space=pltpu.VMEM)),
        input_output_aliases={i: 2 + i for i in range(2 * n)}, compiler_params=_split_copy_params(),
    )(*[_in_hbm(a) for a in sums], *[_in_hbm(lax.empty(a.shape, a.dtype)) for a in sums])
    return (out[0], out[1]), list(out[2:2 + n]), list(out[2 + n:2 + 2 * n]), out[-1]


def chip_exchange_finish(name, sums, recv, sems, after):
    n = len(sums)

    def body(*refs):
        for cp in _chip_copies(refs[:n], refs[n:2 * n], (refs[2 * n], refs[2 * n + 1])):
            cp.wait_send()
            cp.wait_recv()

    both = tuple(pltpu.HBM(a.shape, a.dtype) for a in sums)
    out = pl.pallas_call(
        body, name=name, out_shape=both + both,
        in_specs=[HBM] * (2 * n) + [SEM, SEM, ANY], out_specs=tuple([HBM] * (2 * n)),
        input_output_aliases={i: i for i in range(2 * n)}, compiler_params=_split_copy_params(),
    )(*sums, *recv, *sems, after)
    return list(out[:n]), list(out[n:])


def _mm(name, a, b, *, ta, tb, grid, tm, tn, tk, a_spec, b_spec, outs, extras=(), epilogue=None, alias=None,
        after=None):
    n_extra, n_out = len(extras), len(outs)
    nk = grid[2]
    dn = (((0,) if ta else (1,), (1,) if tb else (0,)), ((), ()))

    def finish(acc, extra_refs, out_refs):
        vals = epilogue(acc, *[r[...] for r in extra_refs]) if epilogue is not None else (acc,)
        for o_ref, v in zip(out_refs, vals):
            o_ref[...] = v.astype(o_ref.dtype)

    def body(*refs):
        a_ref, b_ref = refs[0], refs[1]
        extra_refs = refs[2:2 + n_extra]
        first_out = 2 + n_extra + (1 if alias is not None else 0) + (1 if after is not None else 0)
        out_refs = refs[first_out:first_out + n_out]
        part = lax.dot_general(a_ref[...], b_ref[...], dn, preferred_element_type=F32)
        if nk == 1:
            finish(part, extra_refs, out_refs)
            return
        acc_ref = refs[-1]
        k = pl.program_id(2)

        @pl.when(k == 0)
        def _():
            acc_ref[...] = part

        @pl.when(k > 0)
        def _():
            acc_ref[...] += part

        @pl.when(k == nk - 1)
        def _():
            finish(acc_ref[...], extra_refs, out_refs)

    operands = [a, b] + [e for e, _ in extras]
    in_specs = [a_spec, b_spec] + [s for _, s in extras]
    aliases = {}
    if alias is not None:
        aliases = {len(operands): 0}
        operands.append(alias)
        in_specs.append(ANY)
    if after is not None:
        operands.append(after)
        in_specs.append(ANY)
    blocks = _nbytes((tm, tk), a.dtype) + _nbytes((tk, tn), b.dtype)
    blocks += sum(_nbytes((tm, tn), s.dtype) for s, _ in outs)
    blocks += sum(_nbytes((e.shape[-2] if e.shape[-2] < tm else tm, tn), e.dtype) for e, _ in extras)
    scratch = [] if nk == 1 else [pltpu.VMEM((tm, tn), F32)]
    return pl.pallas_call(
        body, name=name, grid=grid,
        out_shape=[s for s, _ in outs], in_specs=in_specs, out_specs=[s for _, s in outs],
        input_output_aliases=aliases, scratch_shapes=scratch,
        compiler_params=pltpu.CompilerParams(
            dimension_semantics=("parallel", "parallel", "arbitrary"),
            vmem_limit_bytes=_vmem_limit(blocks, 0 if nk == 1 else 3 * tm * tn * 4)),
    )(*operands)


def _mm_tiles(m, n, k):
    return _tile(m, 512), _tile(n, 1024), _tile(k, 2048)


def mm_plain(name, a, b, *, ta=False, tb=False, out_dtype=F32, epilogue=None, extras=(), extra_kinds=(),
             out_dtypes=None, after=None):
    m = a.shape[1] if ta else a.shape[0]
    k = a.shape[0] if ta else a.shape[1]
    n = b.shape[0] if tb else b.shape[1]
    tm, tn, tk = _mm_tiles(m, n, k)
    grid = (m // tm, n // tn, k // tk)
    a_spec = pl.BlockSpec((tk, tm), lambda i, j, kk: (kk, i)) if ta else pl.BlockSpec((tm, tk), lambda i, j, kk: (i, kk))
    b_spec = pl.BlockSpec((tn, tk), lambda i, j, kk: (j, kk)) if tb else pl.BlockSpec((tk, tn), lambda i, j, kk: (kk, j))
    ex = []
    for e, kind in zip(extras, extra_kinds):
        if kind == "row":
            ex.append((e, pl.BlockSpec((1, tn), lambda i, j, kk: (0, j))))
        else:
            ex.append((e, pl.BlockSpec((tm, tn), lambda i, j, kk: (i, j))))
    out_dtypes = out_dtypes or [out_dtype]
    outs = [(jax.ShapeDtypeStruct((m, n), dt), pl.BlockSpec((tm, tn), lambda i, j, kk: (i, j))) for dt in out_dtypes]
    res = _mm(name, a, b, ta=ta, tb=tb, grid=grid, tm=tm, tn=tn, tk=tk, a_spec=a_spec, b_spec=b_spec,
              outs=outs, extras=ex, epilogue=epilogue, after=after)
    return res if len(res) > 1 else res[0]


def mm_in_proj(name, h, w, bias, col0, parts):
    t, d = h.shape
    tm, tn, tk = _mm_tiles(t, d, d)
    per = d // tn
    off = col0 // tn
    grid = (t // tm, parts * per, d // tk)
    outs = [(jax.ShapeDtypeStruct((parts, t, d), F32),
             pl.BlockSpec((None, tm, tn), lambda i, j, kk: (j // per, i, j % per)))]
    ex = [(bias, pl.BlockSpec((1, tn), lambda i, j, kk: (0, j + off)))]
    return _mm(name, h, w, ta=False, tb=False, grid=grid, tm=tm, tn=tn, tk=tk,
               a_spec=pl.BlockSpec((tm, tk), lambda i, j, kk: (i, kk)),
               b_spec=pl.BlockSpec((tk, tn), lambda i, j, kk: (kk, j + off)),
               outs=outs, extras=ex, epilogue=lambda acc, bb: (acc + bb,))[0]


def mm_in_proj_dw(name, h, dz, col0, n_total, prev=None):
    parts, t, d = dz.shape
    tm, tn, tk = _mm_tiles(d, d, t)
    per = d // tn
    off = col0 // tn
    grid = (d // tm, parts * per, t // tk)
    outs = [(jax.ShapeDtypeStruct((d, n_total), BF16), pl.BlockSpec((tm, tn), lambda i, j, kk: (i, j + off)))]
    return _mm(name, h, dz, ta=True, tb=False, grid=grid, tm=tm, tn=tn, tk=tk,
               a_spec=pl.BlockSpec((tk, tm), lambda i, j, kk: (kk, i)),
               b_spec=pl.BlockSpec((None, tk, tn), lambda i, j, kk: (j // per, kk, j % per)),
               outs=outs, alias=prev)[0]


def mm_in_proj_dh(name, dz, w, col0, res=None):
    parts, t, d = dz.shape
    tm, tn, tk = _mm_tiles(t, d, d)
    per = d // tk
    off = col0 // tk
    grid = (t // tm, d // tn, parts * per)
    outs = [(jax.ShapeDtypeStruct((t, d), F32), pl.BlockSpec((tm, tn), lambda i, j, kk: (i, j)))]
    ex, epi = [], None
    if res is not None:
        ex = [(res, pl.BlockSpec((tm, tn), lambda i, j, kk: (i, j)))]
        epi = lambda acc, r: (acc + r,)
    return _mm(name, dz, w, ta=False, tb=True, grid=grid, tm=tm, tn=tn, tk=tk,
               a_spec=pl.BlockSpec((None, tm, tk), lambda i, j, kk: (kk // per, i, kk % per)),
               b_spec=pl.BlockSpec((tn, tk), lambda i, j, kk: (j, kk + off)),
               outs=outs, extras=ex, epilogue=epi)[0]


ROW_TILE = 256


def rmsnorm_fwd(name, x, g, after=None):
    t, d = x.shape
    tr = _tile(t, ROW_TILE, SUBLANES)
    order = [] if after is None else [after]

    def body(x_ref, g_ref, *rest):
        h_ref = rest[-1]
        xv = x_ref[...]
        rstd = lax.rsqrt(jnp.mean(xv * xv, axis=-1, keepdims=True) + EPS)
        h_ref[...] = (xv * rstd * g_ref[...]).astype(BF16)

    return pl.pallas_call(
        body, name=name, grid=(t // tr,),
        out_shape=jax.ShapeDtypeStruct((t, d), BF16),
        in_specs=[pl.BlockSpec((tr, d), lambda i: (i, 0)), pl.BlockSpec((1, d), lambda i: (0, 0))] + [ANY] * len(order),
        out_specs=pl.BlockSpec((tr, d), lambda i: (i, 0)),
        compiler_params=pltpu.CompilerParams(dimension_semantics=("parallel",)),
    )(x, g, *order)


def _rmsnorm_bwd_math(xv, gv, dout):
    rstd = lax.rsqrt(jnp.mean(xv * xv, axis=-1, keepdims=True) + EPS)
    xhat = xv * rstd
    dy = dout * gv
    dx = rstd * (dy - xhat * jnp.mean(dy * xhat, axis=-1, keepdims=True))
    dg = jnp.sum(dout * xhat, axis=0, keepdims=True)
    return dx, dg


def rmsnorm_bwd(name, dh, x, g, dres):
    t, d = x.shape
    tr = _tile(t, ROW_TILE, SUBLANES)

    def body(dh_ref, x_ref, g_ref, dres_ref, dx_ref, dxb_ref, dg_ref):
        dx, dg = _rmsnorm_bwd_math(x_ref[...], g_ref[...], dh_ref[...])
        dx = dx + dres_ref[...]
        dx_ref[...] = dx
        dxb_ref[...] = dx.astype(BF16)

        @pl.when(pl.program_id(0) == 0)
        def _():
            dg_ref[...] = dg

        @pl.when(pl.program_id(0) > 0)
        def _():
            dg_ref[...] += dg

    row = pl.BlockSpec((tr, d), lambda i: (i, 0))
    vec = pl.BlockSpec((1, d), lambda i: (0, 0))
    return pl.pallas_call(
        body, name=name, grid=(t // tr,),
        out_shape=[jax.ShapeDtypeStruct((t, d), F32), jax.ShapeDtypeStruct((t, d), BF16),
                   jax.ShapeDtypeStruct((1, d), F32)],
        in_specs=[row, row, vec, row], out_specs=[row, row, vec],
        compiler_params=pltpu.CompilerParams(dimension_semantics=("arbitrary",)),
    )(dh, x, g, dres)


def loss_head(name, x, g, target):
    t, d = x.shape
    tr = _tile(t, ROW_TILE, SUBLANES)

    def body(x_ref, g_ref, tgt_ref, dx_ref, dxb_ref, dg_ref, loss_ref):
        xv, gv = x_ref[...], g_ref[...]
        rstd = lax.rsqrt(jnp.mean(xv * xv, axis=-1, keepdims=True) + EPS)
        err = xv * rstd * gv - tgt_ref[...]
        lsum = (0.5 / d) * jnp.sum(err * err, axis=0, keepdims=True)
        dx, dg = _rmsnorm_bwd_math(xv, gv, err * (1.0 / d))
        dx_ref[...] = dx
        dxb_ref[...] = dx.astype(BF16)

        @pl.when(pl.program_id(0) == 0)
        def _():
            dg_ref[...] = dg
            loss_ref[...] = lsum

        @pl.when(pl.program_id(0) > 0)
        def _():
            dg_ref[...] += dg
            loss_ref[...] += lsum

    row = pl.BlockSpec((tr, d), lambda i: (i, 0))
    vec = pl.BlockSpec((1, d), lambda i: (0, 0))
    return pl.pallas_call(
        body, name=name, grid=(t // tr,),
        out_shape=[jax.ShapeDtypeStruct((t, d), F32), jax.ShapeDtypeStruct((t, d), BF16),
                   jax.ShapeDtypeStruct((1, d), F32), jax.ShapeDtypeStruct((1, d), F32)],
        in_specs=[row, vec, row], out_specs=[row, row, vec, vec],
        compiler_params=pltpu.CompilerParams(dimension_semantics=("arbitrary",)),
    )(x, g, target)


TIME_TILE = 256


def _sigmoid(v):
    return 1.0 / (1.0 + jnp.exp(-v))


def _gelu(v):
    return 0.5 * v * (1.0 + jnp.tanh(GELU_C * (v + GELU_A * v * v * v)))


def _gelu_grad(v):
    th = jnp.tanh(GELU_C * (v + GELU_A * v * v * v))
    return 0.5 * (1.0 + th) + 0.5 * v * (1.0 - th * th) * GELU_C * (1.0 + 3.0 * GELU_A * v * v)


def _one_minus_exp(v):
    series = -v * (1.0 + v * (0.5 + v * (1.0 / 6.0 + v * (1.0 / 24.0))))
    return jnp.where(v > -0.05, series, 1.0 - jnp.exp(v))


def _back(cur, halo, s, row):
    tt, cb = cur.shape
    pad = jnp.concatenate([pltpu.roll(halo, s, 0), jnp.zeros((tt - SUBLANES, cb), F32)], axis=0)
    return jnp.where(row < s, pad, pltpu.roll(cur, s, 0))


def _ahead(cur, halo, s, row):
    tt, cb = cur.shape
    pad = jnp.concatenate([jnp.zeros((tt - SUBLANES, cb), F32), pltpu.roll(halo, SUBLANES - s, 0)], axis=0)
    return jnp.where(row >= tt - s, pad, pltpu.roll(cur, tt - s, 0))


def _lru_gates(u, wr, wi, br, bi, lam):
    ub = u.astype(BF16)
    r = _sigmoid(jnp.dot(ub, wr, preferred_element_type=F32) + br)
    i = _sigmoid(jnp.dot(ub, wi, preferred_element_type=F32) + bi)
    sp = jnp.maximum(-lam, 0.0) + jnp.log1p(jnp.exp(-jnp.abs(lam)))
    log_a = -LRU_C * r * sp
    a = jnp.exp(log_a)
    mult = jnp.sqrt(_one_minus_exp(2.0 * log_a))
    return ub, r, i, a, mult


def _branch_specs(tt, cb, nt, time_of):
    per = tt // SUBLANES

    def tile(parts):
        if parts is None:
            return pl.BlockSpec((tt, cb), lambda n, s: (time_of(s), n))
        return pl.BlockSpec((parts, tt, cb), lambda n, s: (0, time_of(s), n))

    def halo_before(parts):
        if parts is None:
            return pl.BlockSpec((SUBLANES, cb), lambda n, s: (jnp.maximum(time_of(s) * per - 1, 0), n))
        return pl.BlockSpec((parts, SUBLANES, cb), lambda n, s: (0, jnp.maximum(time_of(s) * per - 1, 0), n))

    def rows(k):
        return pl.BlockSpec((k, cb), lambda n, s: (0, n))

    gate_w = pl.BlockSpec((None, cb, cb), lambda n, s: (n, 0, 0))
    return tile, halo_before, rows, gate_w


def branch_fwd(name, z5, caw, cab, cbw, wr, wi, br, bi, lam):
    _, t, d = z5.shape
    cb = wr.shape[-1]
    tt = _tile(t, TIME_TILE, SUBLANES)
    nt = t // tt
    tile, halo_before, rows, gate_w = _branch_specs(tt, cb, nt, lambda s: s)

    def body(z_ref, zh_ref, caw_ref, cab_ref, cbw_ref, wr_ref, wi_ref, br_ref, bi_ref, lam_ref,
             pa_ref, pb_ref, h_ref, carry_ref, acum_ref, bcum_ref):
        s = pl.program_id(1)
        row = lax.broadcasted_iota(jnp.int32, (tt, cb), 0)
        keep = jnp.where(s > 0, 1.0, 0.0)

        @pl.when(s == 0)
        def _():
            carry_ref[...] = jnp.zeros_like(carry_ref)

        xa, ya, cbv, ccv, cxv = (z_ref[p] for p in range(N_BRANCH))
        xa_h = zh_ref[0] * keep
        caw_v = caw_ref[...]
        u = caw_v[3:4] * xa + cab_ref[...]
        for j in range(1, CONV_A_WIDTH):
            u = u + caw_v[3 - j:4 - j] * _back(xa, xa_h, j, row)
        _, _, gi, a, mult = _lru_gates(u, wr_ref[...], wi_ref[...], br_ref[...], bi_ref[...], lam_ref[...])
        b = mult * (gi * u)

        rm = row & (SUBLANES - 1)
        for sh in (1, 2, 4):
            a_prev = jnp.where(rm >= sh, pltpu.roll(a, sh, 0), 1.0)
            b_prev = jnp.where(rm >= sh, pltpu.roll(b, sh, 0), 0.0)
            b = a * b_prev + b
            a = a * a_prev
        acum_ref[...] = a
        bcum_ref[...] = b

        def group(g, h_in):
            r0 = pl.multiple_of(g * SUBLANES, SUBLANES)
            hg = acum_ref[pl.ds(r0, SUBLANES), :] * h_in + bcum_ref[pl.ds(r0, SUBLANES), :]
            h_ref[pl.ds(r0, SUBLANES), :] = hg
            return jnp.broadcast_to(hg[SUBLANES - 1:SUBLANES, :], (SUBLANES, cb))

        carry_ref[...] = lax.fori_loop(0, tt // SUBLANES, group, carry_ref[...], unroll=4)
        pa_ref[...] = (h_ref[...] * _gelu(ya)).astype(BF16)

        q = ccv * cxv
        q_h = zh_ref[3] * zh_ref[4] * keep
        cbw_v = cbw_ref[...]
        v = cbw_v[2:3] * q
        for j in range(1, CONV_B_WIDTH):
            v = v + cbw_v[2 - j:3 - j] * _back(q, q_h, j, row)
        pb_ref[...] = (cbv * v).astype(BF16)

    act = jax.ShapeDtypeStruct((t, d), BF16)
    blocks = _nbytes((2 * N_BRANCH, tt, cb), F32) + _nbytes((2, cb, cb), BF16) + _nbytes((4, tt, cb), F32)
    return pl.pallas_call(
        body, name=name, grid=(d // cb, nt),
        out_shape=[act, act, jax.ShapeDtypeStruct((t, d), F32)],
        in_specs=[tile(N_BRANCH), halo_before(N_BRANCH), rows(CONV_A_WIDTH), rows(1), rows(CONV_B_WIDTH),
                  gate_w, gate_w, rows(1), rows(1), rows(1)],
        out_specs=[tile(None), tile(None), tile(None)],
        scratch_shapes=[pltpu.VMEM((SUBLANES, cb), F32), pltpu.VMEM((tt, cb), F32), pltpu.VMEM((tt, cb), F32)],
        compiler_params=pltpu.CompilerParams(dimension_semantics=("parallel", "arbitrary"),
                                             vmem_limit_bytes=_vmem_limit(blocks, 40 * tt * cb * 4)),
    )(z5, z5, caw, cab, cbw, wr, wi, br, bi, lam)


def branch_bwd(name, z5, hl, dpa, dpb, caw, cab, cbw, wr, wi, br, bi, lam):
    _, t, d = z5.shape
    cb = wr.shape[-1]
    tt = _tile(t, TIME_TILE, SUBLANES)
    nt = t // tt
    tile, halo_before, rows, gate_w = _branch_specs(tt, cb, nt, lambda s: nt - 1 - s)

    def body(z_ref, zh_ref, h_ref, hh_ref, dpa_ref, dpb_ref, caw_ref, cab_ref, cbw_ref, wr_ref, wi_ref,
             br_ref, bi_ref, lam_ref,
             dz_ref, dbias_ref, dcaw_ref, dcab_ref, dcbw_ref, dwr_ref, dwi_ref, dbr_ref, dbi_ref, dlam_ref,
             g_carry_ref, a_next_ref, du_next_ref, dv_next_ref, acum_ref, bcum_ref, gout_ref):
        s = pl.program_id(1)
        row = lax.broadcasted_iota(jnp.int32, (tt, cb), 0)
        keep = jnp.where(s < nt - 1, 1.0, 0.0)

        @pl.when(s == 0)
        def _():
            for r in (g_carry_ref, a_next_ref, du_next_ref, dv_next_ref, dbias_ref, dcaw_ref, dcab_ref, dcbw_ref,
                      dwr_ref, dwi_ref, dbr_ref, dbi_ref, dlam_ref):
                r[...] = jnp.zeros_like(r)

        xa, ya, cbv, ccv, cxv = (z_ref[p] for p in range(N_BRANCH))
        xa_h = zh_ref[0] * keep
        caw_v = caw_ref[...]
        wr_v, wi_v, lam_v = wr_ref[...], wi_ref[...], lam_ref[...]
        xa_back = [xa] + [_back(xa, xa_h, j, row) for j in range(1, CONV_A_WIDTH)]
        u = caw_v[3:4] * xa + cab_ref[...]
        for j in range(1, CONV_A_WIDTH):
            u = u + caw_v[3 - j:4 - j] * xa_back[j]
        ub, gr, gi, a, mult = _lru_gates(u, wr_v, wi_v, br_ref[...], bi_ref[...], lam_v)

        hv = h_ref[...]
        dpa_v = dpa_ref[...]
        dya = dpa_v * hv * _gelu_grad(ya)
        dh = dpa_v * _gelu(ya)

        a_up = _ahead(a, a_next_ref[...], 1, row)
        rm = row & (SUBLANES - 1)
        ca, cg = a_up, dh
        for sh in (1, 2, 4):
            a_nxt = jnp.where(rm + sh < SUBLANES, pltpu.roll(ca, tt - sh, 0), 1.0)
            g_nxt = jnp.where(rm + sh < SUBLANES, pltpu.roll(cg, tt - sh, 0), 0.0)
            cg = ca * g_nxt + cg
            ca = ca * a_nxt
        acum_ref[...] = ca
        bcum_ref[...] = cg
        n_groups = tt // SUBLANES

        def group(k, g_in):
            r0 = pl.multiple_of((n_groups - 1 - k) * SUBLANES, SUBLANES)
            gg = acum_ref[pl.ds(r0, SUBLANES), :] * g_in + bcum_ref[pl.ds(r0, SUBLANES), :]
            gout_ref[pl.ds(r0, SUBLANES), :] = gg
            return jnp.broadcast_to(gg[0:1, :], (SUBLANES, cb))

        g_carry_ref[...] = lax.fori_loop(0, n_groups, group, g_carry_ref[...], unroll=4)
        a_next_ref[...] = a[0:SUBLANES, :]
        gv = gout_ref[...]

        h_prev = _back(hv, hh_ref[...] * keep, 1, row)
        da = gv * h_prev
        iu = gi * u
        dmult = gv * iu
        di = gv * mult * u
        du = gv * mult * gi
        dla = da * a - dmult * (a * a) / mult
        sp = jnp.maximum(-lam_v, 0.0) + jnp.log1p(jnp.exp(-jnp.abs(lam_v)))
        dr = dla * (-LRU_C * sp)
        dlam_ref[...] += jnp.sum(dla * gr, axis=0, keepdims=True) * (LRU_C * _sigmoid(-lam_v))
        drp = dr * gr * (1.0 - gr)
        dip = di * gi * (1.0 - gi)
        dbr_ref[...] += jnp.sum(drp, axis=0, keepdims=True)
        dbi_ref[...] += jnp.sum(dip, axis=0, keepdims=True)
        drp_b, dip_b = drp.astype(BF16), dip.astype(BF16)
        tn_dims = (((0,), (0,)), ((), ()))
        nt_dims = (((1,), (1,)), ((), ()))
        dwr_ref[...] += lax.dot_general(ub, drp_b, tn_dims, preferred_element_type=F32)
        dwi_ref[...] += lax.dot_general(ub, dip_b, tn_dims, preferred_element_type=F32)
        du = du + lax.dot_general(drp_b, wr_v, nt_dims, preferred_element_type=F32)
        du = du + lax.dot_general(dip_b, wi_v, nt_dims, preferred_element_type=F32)

        du_h = du_next_ref[...]
        dxa = caw_v[3:4] * du
        dcaw_ref[3:4, :] += jnp.sum(du * xa, axis=0, keepdims=True)
        for j in range(1, CONV_A_WIDTH):
            dxa = dxa + caw_v[3 - j:4 - j] * _ahead(du, du_h, j, row)
            dcaw_ref[3 - j:4 - j, :] += jnp.sum(du * xa_back[j], axis=0, keepdims=True)
        dcab_ref[...] += jnp.sum(du, axis=0, keepdims=True)
        du_next_ref[...] = du[0:SUBLANES, :]

        q = ccv * cxv
        q_h = zh_ref[3] * zh_ref[4] * keep
        cbw_v = cbw_ref[...]
        q_back = [q] + [_back(q, q_h, j, row) for j in range(1, CONV_B_WIDTH)]
        v = cbw_v[2:3] * q
        for j in range(1, CONV_B_WIDTH):
            v = v + cbw_v[2 - j:3 - j] * q_back[j]
        dpb_v = dpb_ref[...]
        dcb = dpb_v * v
        dv = dpb_v * cbv
        dv_h = dv_next_ref[...]
        dq = cbw_v[2:3] * dv
        dcbw_ref[2:3, :] += jnp.sum(dv * q, axis=0, keepdims=True)
        for j in range(1, CONV_B_WIDTH):
            dq = dq + cbw_v[2 - j:3 - j] * _ahead(dv, dv_h, j, row)
            dcbw_ref[2 - j:3 - j, :] += jnp.sum(dv * q_back[j], axis=0, keepdims=True)
        dv_next_ref[...] = dv[0:SUBLANES, :]
        dcc = dq * cxv
        dcx = dq * ccv

        for p, val in enumerate((dxa, dya, dcb, dcc, dcx)):
            dz_ref[p] = val.astype(BF16)
            dbias_ref[p:p + 1, :] += jnp.sum(val, axis=0, keepdims=True)

    def acc_rows(k):
        return jax.ShapeDtypeStruct((k, d), F32)

    nb = d // cb
    gate_grad = jax.ShapeDtypeStruct((nb, cb, cb), F32)
    small = pltpu.VMEM((SUBLANES, cb), F32)
    big = pltpu.VMEM((tt, cb), F32)
    blocks = _nbytes((2 * (N_BRANCH + 3), tt, cb), F32) + _nbytes((4, cb, cb), F32)
    return pl.pallas_call(
        body, name=name, grid=(nb, nt),
        out_shape=[jax.ShapeDtypeStruct((N_BRANCH, t, d), BF16), acc_rows(N_BRANCH), acc_rows(CONV_A_WIDTH),
                   acc_rows(1), acc_rows(CONV_B_WIDTH), gate_grad, gate_grad, acc_rows(1), acc_rows(1), acc_rows(1)],
        in_specs=[tile(N_BRANCH), halo_before(N_BRANCH), tile(None), halo_before(None), tile(None), tile(None),
                  rows(CONV_A_WIDTH), rows(1), rows(CONV_B_WIDTH), gate_w, gate_w, rows(1), rows(1), rows(1)],
        out_specs=[tile(N_BRANCH), rows(N_BRANCH), rows(CONV_A_WIDTH), rows(1), rows(CONV_B_WIDTH),
                   gate_w, gate_w, rows(1), rows(1), rows(1)],
        scratch_shapes=[small, small, small, small, big, big, big],
        compiler_params=pltpu.CompilerParams(dimension_semantics=("parallel", "arbitrary"),
                                             vmem_limit_bytes=_vmem_limit(blocks, 60 * tt * cb * 4)),
    )(z5, z5, hl, hl, dpa, dpb, caw, cab, cbw, wr, wi, br, bi, lam)


MERGE_COLS = 512


def merge_fwd(name, z2, oa, ob):
    t, d = oa.shape
    tr, tc = _tile(t, ROW_TILE, SUBLANES), _tile(d, MERGE_COLS)

    def body(z_ref, oa_ref, ob_ref, m_ref):
        m_ref[...] = (_sigmoid(z_ref[0]) * oa_ref[...] + _sigmoid(z_ref[1]) * ob_ref[...]).astype(BF16)

    blk = pl.BlockSpec((tr, tc), lambda i, j: (i, j))
    return pl.pallas_call(
        body, name=name, grid=(t // tr, d // tc),
        out_shape=jax.ShapeDtypeStruct((t, d), BF16),
        in_specs=[pl.BlockSpec((2, tr, tc), lambda i, j: (0, i, j)), blk, blk], out_specs=blk,
        compiler_params=pltpu.CompilerParams(dimension_semantics=("parallel", "parallel")),
    )(z2, oa, ob)


def merge_bwd(name, dm, z2, oa, ob):
    t, d = oa.shape
    tr, tc = _tile(t, ROW_TILE, SUBLANES), _tile(d, MERGE_COLS)

    def body(dm_ref, z_ref, oa_ref, ob_ref, doa_ref, dob_ref, dz_ref, dbias_ref):
        dmv = dm_ref[...]
        sa, sb = _sigmoid(z_ref[0]), _sigmoid(z_ref[1])
        doa_ref[...] = (dmv * sa).astype(BF16)
        dob_ref[...] = (dmv * sb).astype(BF16)
        dga = dmv * oa_ref[...] * sa * (1.0 - sa)
        dgb = dmv * ob_ref[...] * sb * (1.0 - sb)
        dz_ref[0] = dga.astype(BF16)
        dz_ref[1] = dgb.astype(BF16)

        @pl.when(pl.program_id(1) == 0)
        def _():
            dbias_ref[...] = jnp.zeros_like(dbias_ref)

        dbias_ref[0:1, :] += jnp.sum(dga, axis=0, keepdims=True)
        dbias_ref[1:2, :] += jnp.sum(dgb, axis=0, keepdims=True)

    blk = pl.BlockSpec((tr, tc), lambda j, i: (i, j))
    pair = pl.BlockSpec((2, tr, tc), lambda j, i: (0, i, j))
    act = jax.ShapeDtypeStruct((t, d), BF16)
    return pl.pallas_call(
        body, name=name, grid=(d // tc, t // tr),
        out_shape=[act, act, jax.ShapeDtypeStruct((2, t, d), BF16), jax.ShapeDtypeStruct((2, d), F32)],
        in_specs=[blk, pair, blk, blk],
        out_specs=[blk, blk, pair, pl.BlockSpec((2, tc), lambda j, i: (0, j))],
        compiler_params=pltpu.CompilerParams(dimension_semantics=("parallel", "arbitrary")),
    )(dm, z2, oa, ob)


def _adamw_math(w, g, m, v):
    m = ADAM_B1 * m + (1.0 - ADAM_B1) * g
    v = ADAM_B2 * v + (1.0 - ADAM_B2) * (g * g)
    m_hat = m / (1.0 - ADAM_B1 ** ADAM_STEP)
    v_hat = v / (1.0 - ADAM_B2 ** ADAM_STEP)
    delta = -ADAM_LR * (m_hat / (jnp.sqrt(v_hat) + ADAM_EPS) + ADAM_WD * w)
    return delta, m, v


def _sum_in_order(ref):
    g = ref[0].astype(F32)
    for p in range(1, ref.shape[0]):
        g = g + ref[p].astype(F32)
    return g


def adamw_sharded(name, recv, own, chip, w, m, v):
    r, c = w.shape
    rows = r // DEPTH
    tr = _tile(rows, max(SUBLANES, (256 * 1024) // c), 16)
    per = rows // tr

    def body(chip_ref, *refs):
        recv_refs, own_refs = refs[:DEPTH], refs[DEPTH:2 * DEPTH]
        w_ref, m_ref, v_ref, g_ref, d_ref, mo_ref, vo_ref = refs[2 * DEPTH:]
        step = pl.program_id(0)
        for l in range(DEPTH):
            @pl.when((step >= l * per) & (step < (l + 1) * per))
            def _(l=l):
                g = None
                for q in range(N_CHIP):
                    part = jnp.where(chip_ref[0] == q, own_refs[l][...], recv_refs[l][q]).astype(F32)
                    g = part if g is None else g + part
                delta, m2, v2 = _adamw_math(w_ref[...], g, m_ref[...], v_ref[...])
                g_ref[...] = g
                d_ref[...] = delta
                mo_ref[...] = m2
                vo_ref[...] = v2

    def in_layer(l, i):
        return jnp.clip(i - l * per, 0, per - 1)

    recv_specs = [pl.BlockSpec((N_CHIP, tr, c), lambda i, chip_ref, l=l: (0, in_layer(l, i), 0)) for l in range(DEPTH)]
    own_specs = [pl.BlockSpec((None, tr, c), lambda i, chip_ref, l=l: (chip_ref[0], in_layer(l, i), 0))
                 for l in range(DEPTH)]
    blk = pl.BlockSpec((tr, c), lambda i, chip_ref: (i, 0))
    out = jax.ShapeDtypeStruct((r, c), F32)
    blocks = DEPTH * _nbytes((N_CHIP + 1, tr, c), BF16) + 7 * _nbytes((tr, c), F32)
    return pl.pallas_call(
        body, name=name, out_shape=[out, out, out, out],
        grid_spec=pltpu.PrefetchScalarGridSpec(
            num_scalar_prefetch=1, grid=(DEPTH * per,), in_specs=recv_specs + own_specs + [blk, blk, blk],
            out_specs=[blk, blk, blk, blk]),
        compiler_params=pltpu.CompilerParams(dimension_semantics=("parallel",),
                                             vmem_limit_bytes=_vmem_limit(blocks)),
    )(chip, *recv, *own, w, m, v)


def sum_partials(name, parts):
    _, r, c = parts.shape

    def body(p_ref, o_ref):
        o_ref[...] = _sum_in_order(p_ref)

    return pl.pallas_call(body, name=name, out_shape=jax.ShapeDtypeStruct((r, c), F32))(parts)


def adamw_small(name, g, w, m, v):
    def body(g_ref, w_ref, m_ref, v_ref, d_ref, mo_ref, vo_ref):
        delta, m2, v2 = _adamw_math(w_ref[...], g_ref[...], m_ref[...], v_ref[...])
        d_ref[...] = delta
        mo_ref[...] = m2
        vo_ref[...] = v2

    out = jax.ShapeDtypeStruct(w.shape, F32)
    return pl.pallas_call(body, name=name, out_shape=[out, out, out])(g, w, m, v)


def _pack(arrays):
    flat = jnp.concatenate([a.reshape(-1) for a in arrays])
    n = flat.shape[0]
    rows = -(-n // (SUBLANES * LANES)) * SUBLANES
    flat = jnp.pad(flat, (0, rows * LANES - n))
    return flat.reshape(rows, LANES)


def _unpack(packed, shapes):
    flat = packed.reshape(-1)
    out, off = [], 0
    for s in shapes:
        n = math.prod(s)
        out.append(flat[off:off + n].reshape(s))
        off += n
    return out


def kernel(x, norm1_g, w_in, b_in, conv_a_w, conv_a_b, lru_wr, lru_br, lru_wi, lru_bi, lru_lam, conv_b_w, w_pa, w_pb, w_o, norm2_g, w_mlp1, w_mlp2, final_g, loss_target, m_norm1_g, m_w_in, m_b_in, m_conv_a_w, m_conv_a_b, m_lru_wr, m_lru_br, m_lru_wi, m_lru_bi, m_lru_lam, m_conv_b_w, m_w_pa, m_w_pb, m_w_o, m_norm2_g, m_w_mlp1, m_w_mlp2, m_final_g, v_norm1_g, v_w_in, v_b_in, v_conv_a_w, v_conv_a_b, v_lru_wr, v_lru_br, v_lru_wi, v_lru_bi, v_lru_lam, v_conv_b_w, v_w_pa, v_w_pb, v_w_o, v_norm2_g, v_w_mlp1, v_w_mlp2, v_final_g):
    weights = dict(norm1_g=norm1_g, w_in=w_in, b_in=b_in, conv_a_w=conv_a_w, conv_a_b=conv_a_b, lru_wr=lru_wr,
                   lru_br=lru_br, lru_wi=lru_wi, lru_bi=lru_bi, lru_lam=lru_lam, conv_b_w=conv_b_w, w_pa=w_pa,
                   w_pb=w_pb, w_o=w_o, norm2_g=norm2_g, w_mlp1=w_mlp1, w_mlp2=w_mlp2, final_g=final_g)
    mom1 = dict(norm1_g=m_norm1_g, w_in=m_w_in, b_in=m_b_in, conv_a_w=m_conv_a_w, conv_a_b=m_conv_a_b,
                lru_wr=m_lru_wr, lru_br=m_lru_br, lru_wi=m_lru_wi, lru_bi=m_lru_bi, lru_lam=m_lru_lam,
                conv_b_w=m_conv_b_w, w_pa=m_w_pa, w_pb=m_w_pb, w_o=m_w_o, norm2_g=m_norm2_g, w_mlp1=m_w_mlp1,
                w_mlp2=m_w_mlp2, final_g=m_final_g)
    mom2 = dict(norm1_g=v_norm1_g, w_in=v_w_in, b_in=v_b_in, conv_a_w=v_conv_a_w, conv_a_b=v_conv_a_b,
                lru_wr=v_lru_wr, lru_br=v_lru_br, lru_wi=v_lru_wi, lru_bi=v_lru_bi, lru_lam=v_lru_lam,
                conv_b_w=v_conv_b_w, w_pa=v_w_pa, w_pb=v_w_pb, w_o=v_w_o, norm2_g=v_norm2_g, w_mlp1=v_w_mlp1,
                w_mlp2=v_w_mlp2, final_g=v_final_g)
    names = list(weights)

    t, d = x.shape[1], x.shape[2]
    n_in = b_in.shape[1]
    nb, bw = lru_wr.shape[1], lru_wr.shape[3]
    mx, my, mc = _my_coords()
    me = _flat(mx, my, mc)
    me_index = me.astype(jnp.int32).reshape(1)
    my_core = mc.astype(jnp.int32).reshape(1)
    my_chip = (2 * mx + my).astype(jnp.int32).reshape(1)
    x0 = x.reshape(t, d)
    target = loss_target.reshape(t, d)

    big = ["w_in", "w_pa", "w_pb", "w_o", "w_mlp1", "w_mlp2", "lru_wr", "lru_wi"]
    big_axis = dict(w_in=1, w_pa=0, w_pb=0, w_o=0, w_mlp1=1, w_mlp2=0, lru_wr=1, lru_wi=1)
    small_sharded = ["conv_a_w", "conv_b_w", "lru_br", "lru_bi"]
    small_packed = _pack([weights[k] for k in small_sharded])
    small_rows = all_gather("gather_small_params", [small_packed[None]], [0])[0].reshape(N_DEV, -1)
    big_axes = [big_axis[k] for k in big]
    land = [[place_shard(f"place_{k}_l{l}", weights[k], l, big_axis[k], me_index) for k in big] for l in range(DEPTH)]
    own_sems, flying, token = gather_start("gather_start_l0", land[0], big_axes)
    pass_sems, flying = gather_pass_on("gather_pass_on_l0", flying, big_axes, own_sems, token)
    full = [dict(zip(big, gather_finish("gather_finish_l0", flying, big_axes, own_sems, pass_sems, token))), None]
    own_sems, flying, token = gather_start("gather_start_l1", land[1], big_axes, after=full[0]["w_in"])
    small_full = {}
    off = 0
    for k in small_sharded:
        shard_shape = weights[k].shape
        n = math.prod(shard_shape)
        g = small_rows[:, off:off + n].reshape(N_DEV, *shard_shape)
        small_full[k] = jnp.moveaxis(g, 0, 2).reshape(shard_shape[0], shard_shape[1], N_DEV * shard_shape[2])
        off += n

    def layer_params(l):
        f = full[l]
        return dict(
            g1=norm1_g[l][None], w_in=f["w_in"], b_in=b_in[l][None], caw=small_full["conv_a_w"][l],
            cab=conv_a_b[l][None], wr=f["lru_wr"], wi=f["lru_wi"],
            br=small_full["lru_br"][l].reshape(1, d), bi=small_full["lru_bi"][l].reshape(1, d),
            lam=lru_lam[l][None], cbw=small_full["conv_b_w"][l], w_pa=f["w_pa"], w_pb=f["w_pb"], w_o=f["w_o"],
            g2=norm2_g[l][None], w1=f["w_mlp1"], w2=f["w_mlp2"])

    params = [layer_params(0), None]

    relu2 = lambda acc: (acc, jnp.square(jnp.maximum(acc, 0.0)))
    add_res = lambda acc, r: (acc + r,)
    relu2_bwd = lambda acc, pre: (acc * (2.0 * jnp.maximum(pre, 0.0)),)

    saved = []
    xl = x0
    for l in range(DEPTH):
        p = params[l]
        tag = f"l{l}"
        h1 = rmsnorm_fwd(f"norm1_{tag}", xl, p["g1"], after=token if l == 0 else None)
        z5 = mm_in_proj(f"in_proj5_{tag}", h1, p["w_in"], p["b_in"], 0, N_BRANCH)
        z2 = mm_in_proj(f"in_proj2_{tag}", h1, p["w_in"], p["b_in"], N_BRANCH * d, N_SPLIT - N_BRANCH)
        pa, pb, hl = branch_fwd(f"branch_fwd_{tag}", z5, p["caw"], p["cab"], p["cbw"], p["wr"], p["wi"],
                                p["br"], p["bi"], p["lam"])
        oa = mm_plain(f"proj_a_{tag}", pa, p["w_pa"])
        ob = mm_plain(f"proj_b_{tag}", pb, p["w_pb"])
        mg = merge_fwd(f"merge_{tag}", z2, oa, ob)
        x1 = mm_plain(f"proj_o_{tag}", mg, p["w_o"], epilogue=add_res, extras=(xl,), extra_kinds=("full",))
        h2 = rmsnorm_fwd(f"norm2_{tag}", x1, p["g2"])
        pre, uu = mm_plain(f"mlp1_{tag}", h2, p["w1"], epilogue=relu2, out_dtypes=[F32, BF16])
        if l == 0:
            pass_sems, flying = gather_pass_on("gather_pass_on_l1", flying, big_axes, own_sems, pre)
        x2 = mm_plain(f"mlp2_{tag}", uu, p["w2"], epilogue=add_res, extras=(x1,), extra_kinds=("full",))
        if l == 0:
            full[1] = dict(zip(big, gather_finish("gather_finish_l1", flying, big_axes, own_sems, pass_sems, x2)))
            params[1] = layer_params(1)
        saved.append(dict(x0=xl, h1=h1, z5=z5, z2=z2, pa=pa, pb=pb, hl=hl, oa=oa, ob=ob, mg=mg, x1=x1, h2=h2,
                          pre=pre, uu=uu))
        xl = x2

    dx, dxb, d_final_g, loss_cols = loss_head("loss_head", xl, final_g[None], target)
    loss = lax.psum(jnp.sum(loss_cols), ("x", "y", "c"))

    small_grads = {}
    exchanges = {}
    token = None
    for l in reversed(range(DEPTH)):
        p, sv = params[l], saved[l]
        tag = f"l{l}"
        dpre = mm_plain(f"mlp2_dx_{tag}", dxb, p["w2"], tb=True, out_dtype=BF16, epilogue=relu2_bwd,
                        extras=(sv["pre"],), extra_kinds=("full",), after=token)
        dw2 = mm_plain(f"mlp2_dw_{tag}", sv["uu"], dxb, ta=True, out_dtype=BF16)
        dw1 = mm_plain(f"mlp1_dw_{tag}", sv["h2"], dpre, ta=True, out_dtype=BF16)
        dh2 = mm_plain(f"mlp1_dx_{tag}", dpre, p["w1"], tb=True)
        dx1, dx1b, dg2 = rmsnorm_bwd(f"norm2_bwd_{tag}", dh2, sv["x1"], p["g2"], dx)
        dmg = mm_plain(f"proj_o_dx_{tag}", dx1b, p["w_o"], tb=True)
        dwo = mm_plain(f"proj_o_dw_{tag}", sv["mg"], dx1b, ta=True, out_dtype=BF16)
        doa, dob, dz2, dbias2 = merge_bwd(f"merge_bwd_{tag}", dmg, sv["z2"], sv["oa"], sv["ob"])
        dpa = mm_plain(f"proj_a_dx_{tag}", doa, p["w_pa"], tb=True)
        dpb = mm_plain(f"proj_b_dx_{tag}", dob, p["w_pb"], tb=True)
        dwpa = mm_plain(f"proj_a_dw_{tag}", sv["pa"], doa, ta=True, out_dtype=BF16)
        dwpb = mm_plain(f"proj_b_dw_{tag}", sv["pb"], dob, ta=True, out_dtype=BF16)
        (dz5, dbias5, dcaw, dcab, dcbw, dwr, dwi, dbr, dbi, dlam) = branch_bwd(
            f"branch_bwd_{tag}", sv["z5"], sv["hl"], dpa, dpb, p["caw"], p["cab"], p["cbw"], p["wr"], p["wi"],
            p["br"], p["bi"], p["lam"])
        dwin = mm_in_proj_dw(f"in_proj5_dw_{tag}", sv["h1"], dz5, 0, n_in)
        dwin = mm_in_proj_dw(f"in_proj2_dw_{tag}", sv["h1"], dz2, N_BRANCH * d, n_in, prev=dwin)
        dh1 = mm_in_proj_dh(f"in_proj5_dx_{tag}", dz5, p["w_in"], 0)
        dh1 = mm_in_proj_dh(f"in_proj2_dx_{tag}", dz2, p["w_in"], N_BRANCH * d, res=dh1)
        dx, dxb, dg1 = rmsnorm_bwd(f"norm1_bwd_{tag}", dh1, sv["x0"], p["g1"], dx1)

        layer_grads = [dwin, dwpa, dwpb, dwo, dw1, dw2, dwr.astype(BF16), dwi.astype(BF16)]
        layer_axes = [1, 0, 0, 0, 1, 0, 1, 1]
        theirs = pair_exchange(f"pair_exchange_{tag}", layer_grads, layer_axes)
        sums = [chip_sum(f"chip_sum_{k}_{tag}", g, a, th, my_core)
                for k, g, a, th in zip(big, layer_grads, layer_axes, theirs)]
        ex_sems, sums, recv, token = chip_exchange_start(f"chip_exchange_start_{tag}", sums)
        exchanges[l] = (ex_sems, sums, recv)
        small_grads[l] = dict(norm1_g=dg1, b_in=jnp.concatenate([dbias5.reshape(1, -1), dbias2.reshape(1, -1)], axis=1),
                              conv_a_b=dcab, lru_lam=dlam, norm2_g=dg2, conv_a_w=dcaw, conv_b_w=dcbw,
                              lru_br=dbr, lru_bi=dbi)
    grad_x = dx.reshape(x.shape)
    own_sums, recv_sums = {}, {}
    for l in reversed(range(DEPTH)):
        ex_sems, sums, recv = exchanges[l]
        own_sums[l], recv_sums[l] = chip_exchange_finish(f"chip_exchange_finish_l{l}", sums, recv, ex_sems, dx)

    replicated = ["norm1_g", "b_in", "conv_a_b", "lru_lam", "norm2_g"]
    small_list = [jnp.stack([small_grads[l][k] for l in range(DEPTH)]) for k in replicated + small_sharded]
    small_list.append(d_final_g)
    packed_partial = _pack(small_list)
    partials = all_gather("gather_small_grads", [packed_partial[None]], [0])[0]
    summed = sum_partials("sum_small_grads", partials)
    pieces = _unpack(summed, [a.shape for a in small_list])
    small_sum = dict(zip(replicated + small_sharded + ["final_g"], pieces))

    grads = {}
    for k in replicated:
        grads[k] = small_sum[k].reshape(weights[k].shape)
    grads["final_g"] = small_sum["final_g"].reshape(weights["final_g"].shape)
    for k in small_sharded:
        shard_shape = weights[k].shape
        g = small_sum[k].reshape(shard_shape[0], shard_shape[1], N_DEV, shard_shape[2])
        grads[k] = lax.dynamic_index_in_dim(g, me, axis=2, keepdims=False)

    small_names = replicated + ["final_g"] + small_sharded
    small_shapes = [weights[k].shape for k in small_names]
    pk = lambda src: _pack([src[k] for k in small_names])
    sd, sm, sv2 = adamw_small("adamw_small", pk(grads), pk(weights), pk(mom1), pk(mom2))
    delta = dict(zip(small_names, _unpack(sd, small_shapes)))
    new_m = dict(zip(small_names, _unpack(sm, small_shapes)))
    new_v = dict(zip(small_names, _unpack(sv2, small_shapes)))

    for i, k in enumerate(big):
        shape = weights[k].shape
        c = shape[-1]
        flat = lambda a: a.reshape(-1, c)
        per_chip = lambda src: [src[l][i].reshape(N_CHIP, -1, c) for l in range(DEPTH)]
        g, dl, m2, v2 = adamw_sharded(f"adamw_{k}", per_chip(recv_sums), per_chip(own_sums), my_chip,
                                      flat(weights[k]), flat(mom1[k]), flat(mom2[k]))
        grads[k], delta[k], new_m[k], new_v[k] = (a.reshape(shape) for a in (g, dl, m2, v2))

    return (loss, grad_x, *[grads[k] for k in names], *[delta[k] for k in names],
            *[new_m[k] for k in names], *[new_v[k] for k in names])
```

```python
import functools
import math

import jax
import jax.numpy as jnp
from jax import lax
from jax.experimental import pallas as pl
from jax.experimental.pallas import tpu as pltpu

F32 = jnp.float32
BF16 = jnp.bfloat16

N_DEV = 8
DEPTH = 2
N_SPLIT = 7
N_BRANCH = 5
EPS = 1e-6
LRU_C = 8.0
CONV_A_WIDTH = 4
CONV_B_WIDTH = 3
GELU_C = math.sqrt(2.0 / math.pi)
GELU_A = 0.044715

ADAM_LR = 0.001
ADAM_B1 = 0.9
ADAM_B2 = 0.999
ADAM_EPS = 1e-08
ADAM_WD = 0.01
ADAM_STEP = 10

LANES = 128
SUBLANES = 8
VMEM_LIMIT_CAP = 56 * 2 ** 20
MESH = pl.DeviceIdType.MESH
ANY = pl.BlockSpec(memory_space=pl.ANY)


def _tile(n, target, align=LANES):
    if n <= target:
        return n
    t = (target // align) * align
    while t >= align:
        if n % t == 0:
            return t
        t -= align
    return n


def _nbytes(shape, dtype):
    return math.prod(shape) * jnp.dtype(dtype).itemsize


def _vmem_limit(block_bytes, scratch_bytes=0):
    est = 2 * block_bytes + scratch_bytes
    return int(min(max(2 * est, 32 * 2 ** 20), VMEM_LIMIT_CAP))


def _my_coords():
    return lax.axis_index("x"), lax.axis_index("y"), lax.axis_index("c")


def _flat(x, y, c):
    return 4 * x + 2 * y + c


def _peer(k, x, y, c):
    return (1 - x if k & 4 else x, 1 - y if k & 2 else y, 1 - c if k & 1 else c)


def _slab(ref, axis, start, size):
    idx = tuple(pl.ds(start, size) if d == axis else slice(None) for d in range(len(ref.shape)))
    return ref.at[idx]


def all_gather(name, shards, axes):
    n = len(shards)
    sizes = [s.shape[a] for s, a in zip(shards, axes)]
    out_shape = []
    for s, a in zip(shards, axes):
        full = list(s.shape)
        full[a] *= N_DEV
        out_shape.append(jax.ShapeDtypeStruct(tuple(full), s.dtype))

    def body(*refs):
        srcs, outs = refs[:n], refs[n:2 * n]
        send_sems, recv_sems, local_sems = refs[2 * n:]
        x, y, c = _my_coords()
        me = _flat(x, y, c)
        local = []
        for i in range(n):
            cp = pltpu.make_async_copy(srcs[i], _slab(outs[i], axes[i], me * sizes[i], sizes[i]), local_sems.at[i])
            cp.start()
            local.append(cp)
        remote = []
        for k in range(1, N_DEV):
            for i in range(n):
                mine = _slab(outs[i], axes[i], me * sizes[i], sizes[i])
                cp = pltpu.make_async_remote_copy(
                    src_ref=srcs[i], dst_ref=mine,
                    send_sem=send_sems.at[(k - 1) * n + i], recv_sem=recv_sems.at[(k - 1) * n + i],
                    device_id=_peer(k, x, y, c), device_id_type=MESH)
                cp.start()
                remote.append(cp)
        for cp in remote:
            cp.wait()
        for cp in local:
            cp.wait()

    return pl.pallas_call(
        body, name=name, out_shape=out_shape,
        in_specs=[ANY] * n, out_specs=[ANY] * n,
        scratch_shapes=[pltpu.SemaphoreType.DMA(((N_DEV - 1) * n,)),
                        pltpu.SemaphoreType.DMA(((N_DEV - 1) * n,)),
                        pltpu.SemaphoreType.DMA((n,))],
    )(*shards)


N_CHIP = 4
OTHER_CHIPS = ((1, 0), (0, 1), (1, 1))


HBM = pl.BlockSpec(memory_space=pltpu.HBM)
SEM = pl.BlockSpec(memory_space=pltpu.SEMAPHORE)
TOKEN = jax.ShapeDtypeStruct((SUBLANES, LANES), F32)


def _split_copy_params():
    return pltpu.CompilerParams(has_side_effects=pltpu.SideEffectType.DATAFLOW_SIDE_EFFECTING)


def _in_hbm(a):
    return pltpu.with_memory_space_constraint(a, pltpu.HBM)


def _other_chips(x, y):
    return [(x + dx - 2 * x * dx, y + dy - 2 * y * dy) for dx, dy in OTHER_CHIPS]


def place_shard(name, w, layer, axis, me):
    shard = w.shape[1:]
    if len(shard) == 2 and axis == 1:
        r, cs = shard
        tr = _tile(r, max(SUBLANES, (512 * 1024) // cs), 16)
        grid, full = (r // tr,), (r, N_DEV * cs)
        in_spec = pl.BlockSpec((None, tr, cs), lambda i, me_ref: (layer, i, 0))
        out_spec = pl.BlockSpec((tr, cs), lambda i, me_ref: (i, me_ref[0]))
    elif len(shard) == 2 and axis == 0:
        rs, cs = shard
        tr = _tile(rs, max(SUBLANES, (512 * 1024) // cs), 16)
        per = rs // tr
        grid, full = (per,), (N_DEV * rs, cs)
        in_spec = pl.BlockSpec((None, tr, cs), lambda i, me_ref: (layer, i, 0))
        out_spec = pl.BlockSpec((tr, cs), lambda i, me_ref: (me_ref[0] * per + i, 0))
    else:
        assert len(shard) == 3 and axis == 1
        lead, rs, cs = shard
        grid, full = (1,), (lead, N_DEV * rs, cs)
        in_spec = pl.BlockSpec((None, lead, rs, cs), lambda i, me_ref: (layer, 0, 0, 0))
        out_spec = pl.BlockSpec((lead, rs, cs), lambda i, me_ref: (0, me_ref[0], 0))

    def body(me_ref, w_ref, o_ref):
        o_ref[...] = w_ref[...].astype(BF16)

    return pl.pallas_call(
        body, name=name, out_shape=jax.ShapeDtypeStruct(full, BF16),
        grid_spec=pltpu.PrefetchScalarGridSpec(num_scalar_prefetch=1, grid=grid, in_specs=[in_spec],
                                               out_specs=out_spec),
        compiler_params=pltpu.CompilerParams(dimension_semantics=("parallel",)),
    )(me, w)


class _GatherCopies:
    def __init__(self, land, axes):
        self.land, self.axes, self.n = land, axes, len(land)
        self.sizes = [r.shape[a] // N_DEV for r, a in zip(land, axes)]
        x, y, c = _my_coords()
        self.c = c
        self.me, self.sibling = (x, y, c), (x, y, 1 - c)
        self.chips = _other_chips(x, y)

    def copy(self, sems, slot, i, block, to):
        rows = _slab(self.land[i], self.axes[i], _flat(*block) * self.sizes[i], self.sizes[i])
        return pltpu.make_async_remote_copy(
            src_ref=rows, dst_ref=rows, send_sem=sems[0].at[slot * self.n + i], recv_sem=sems[1].at[slot * self.n + i],
            device_id=to, device_id_type=MESH)

    def own(self, sems):
        out = []
        for i in range(self.n):
            out.append(self.copy(sems, 0, i, self.me, self.sibling))
            out += [self.copy(sems, 1 + j, i, self.me, (*chip, self.c)) for j, chip in enumerate(self.chips)]
        return out

    def landed_over_ici(self, sems):
        return [self.copy(sems, 1 + j, i, (*chip, self.c), self.me)
                for j, chip in enumerate(self.chips) for i in range(self.n)]

    def passed_on(self, sems):
        return [self.copy(sems, j, i, (*chip, self.c), self.sibling)
                for j, chip in enumerate(self.chips) for i in range(self.n)]

    def from_sibling(self, own_sems, pass_sems):
        out = [self.copy(own_sems, 0, i, self.sibling, self.me) for i in range(self.n)]
        out += [self.copy(pass_sems, j, i, (*chip, 1 - self.c), self.me)
                for j, chip in enumerate(self.chips) for i in range(self.n)]
        return out


def gather_start(name, land, axes, after=None):
    n = len(land)
    order = [] if after is None else [after]
    n_in = n + len(order)

    def body(*refs):
        own_sems, token = (refs[n_in], refs[n_in + 1]), refs[n_in + n + 2]
        for cp in _GatherCopies(refs[:n], axes).own(own_sems):
            cp.start()
        token[...] = jnp.zeros_like(token)

    out = pl.pallas_call(
        body, name=name,
        out_shape=(pltpu.SemaphoreType.DMA((4 * n,)), pltpu.SemaphoreType.DMA((4 * n,)),
                   *[pltpu.HBM(a.shape, a.dtype) for a in land], TOKEN),
        in_specs=[HBM] * n + [ANY] * len(order),
        out_specs=(SEM, SEM, *[HBM] * n, pl.BlockSpec(memory_space=pltpu.VMEM)),
        input_output_aliases={i: 2 + i for i in range(n)}, compiler_params=_split_copy_params(),
    )(*[_in_hbm(a) for a in land], *order)
    return (out[0], out[1]), list(out[2:2 + n]), out[-1]


def gather_pass_on(name, land, axes, own_sems, after):
    n = len(land)

    def body(*refs):
        sems_in, pass_sems = (refs[n], refs[n + 1]), (refs[n + 3], refs[n + 4])
        copies = _GatherCopies(refs[:n], axes)
        for arrived, onward in zip(copies.landed_over_ici(sems_in), copies.passed_on(pass_sems)):
            arrived.wait_recv()
            onward.start()

    out = pl.pallas_call(
        body, name=name,
        out_shape=(pltpu.SemaphoreType.DMA((3 * n,)), pltpu.SemaphoreType.DMA((3 * n,)),
                   *[pltpu.HBM(a.shape, a.dtype) for a in land]),
        in_specs=[HBM] * n + [SEM, SEM, ANY], out_specs=(SEM, SEM, *[HBM] * n),
        input_output_aliases={i: 2 + i for i in range(n)}, compiler_params=_split_copy_params(),
    )(*land, *own_sems, after)
    return (out[0], out[1]), list(out[2:])


def gather_finish(name, land, axes, own_sems, pass_sems, after):
    n = len(land)

    def body(*refs):
        own, passed = (refs[n], refs[n + 1]), (refs[n + 2], refs[n + 3])
        copies = _GatherCopies(refs[:n], axes)
        for cp in copies.from_sibling(own, passed):
            cp.wait_recv()
        for cp in copies.own(own) + copies.passed_on(passed):
            cp.wait_send()

    return list(pl.pallas_call(
        body, name=name, out_shape=tuple(pltpu.HBM(a.shape, a.dtype) for a in land),
        in_specs=[HBM] * n + [SEM] * 4 + [ANY], out_specs=tuple([HBM] * n),
        input_output_aliases={i: i for i in range(n)}, compiler_params=_split_copy_params(),
    )(*land, *own_sems, *pass_sems, after))


def _pair_copies(grads, axes, theirs, sems):
    x, y, c = _my_coords()
    out = []
    for i, (g, a) in enumerate(zip(grads, axes)):
        size = g.shape[a] // N_DEV
        for q in range(N_CHIP):
            k = i * N_CHIP + q
            out.append(pltpu.make_async_remote_copy(
                src_ref=_slab(g, a, (2 * q + 1 - c) * size, size), dst_ref=theirs[i].at[q],
                send_sem=sems[0].at[k], recv_sem=sems[1].at[k], device_id=(x, y, 1 - c), device_id_type=MESH))
    return out


def pair_exchange_start(name, grads, axes):
    n = len(grads)
    theirs = []
    for g, a in zip(grads, axes):
        shard = list(g.shape)
        shard[a] //= N_DEV
        theirs.append(lax.empty((N_CHIP, *shard), g.dtype))

    def body(*refs):
        sems, token = (refs[2 * n], refs[2 * n + 1]), refs[4 * n + 2]
        for cp in _pair_copies(refs[:n], axes, refs[n:2 * n], sems):
            cp.start()
        token[...] = jnp.zeros_like(token)

    both = [pltpu.HBM(a.shape, a.dtype) for a in list(grads) + theirs]
    out = pl.pallas_call(
        body, name=name,
        out_shape=(pltpu.SemaphoreType.DMA((N_CHIP * n,)), pltpu.SemaphoreType.DMA((N_CHIP * n,)), *both, TOKEN),
        in_specs=[HBM] * (2 * n), out_specs=(SEM, SEM, *[HBM] * (2 * n), pl.BlockSpec(memory_space=pltpu.VMEM)),
        input_output_aliases={i: 2 + i for i in range(2 * n)}, compiler_params=_split_copy_params(),
    )(*[_in_hbm(a) for a in grads], *[_in_hbm(a) for a in theirs])
    return (out[0], out[1]), list(out[2:2 + n]), list(out[2 + n:2 + 2 * n]), out[-1]


def pair_exchange_finish(name, grads, theirs, axes, sems, after):
    n = len(grads)

    def body(*refs):
        for cp in _pair_copies(refs[:n], axes, refs[n:2 * n], (refs[2 * n], refs[2 * n + 1])):
            cp.wait_send()
            cp.wait_recv()

    both = tuple(pltpu.HBM(a.shape, a.dtype) for a in list(grads) + list(theirs))
    out = pl.pallas_call(
        body, name=name, out_shape=both,
        in_specs=[HBM] * (2 * n) + [SEM, SEM, ANY], out_specs=tuple([HBM] * (2 * n)),
        input_output_aliases={i: i for i in range(2 * n)}, compiler_params=_split_copy_params(),
    )(*grads, *theirs, *sems, after)
    return list(out[:n]), list(out[n:])


def chip_sum(name, grad, axis, theirs, core):
    if grad.ndim == 2 and axis == 1:
        r, cs = grad.shape[0], grad.shape[1] // N_DEV
        tr = _tile(r, max(SUBLANES, (512 * 1024) // cs), 16)
        grid = (N_CHIP, r // tr)
        g_spec = pl.BlockSpec((tr, cs), lambda q, i, core_ref: (i, 2 * q + core_ref[0]))
        t_spec = pl.BlockSpec((None, tr, cs), lambda q, i, core_ref: (q, i, 0))
    elif grad.ndim == 2 and axis == 0:
        rs, cs = grad.shape[0] // N_DEV, grad.shape[1]
        tr = _tile(rs, max(SUBLANES, (512 * 1024) // cs), 16)
        per = rs // tr
        grid = (N_CHIP, per)
        g_spec = pl.BlockSpec((tr, cs), lambda q, i, core_ref: ((2 * q + core_ref[0]) * per + i, 0))
        t_spec = pl.BlockSpec((None, tr, cs), lambda q, i, core_ref: (q, i, 0))
    else:
        assert grad.ndim == 3 and axis == 1
        lead, rs, cs = grad.shape[0], grad.shape[1] // N_DEV, grad.shape[2]
        grid = (N_CHIP, 1)
        g_spec = pl.BlockSpec((lead, rs, cs), lambda q, i, core_ref: (0, 2 * q + core_ref[0], 0))
        t_spec = pl.BlockSpec((None, lead, rs, cs), lambda q, i, core_ref: (q, 0, 0, 0))

    def body(core_ref, g_ref, t_ref, o_ref):
        o_ref[...] = (g_ref[...].astype(F32) + t_ref[...].astype(F32)).astype(o_ref.dtype)

    return pl.pallas_call(
        body, name=name, out_shape=jax.ShapeDtypeStruct(theirs.shape, theirs.dtype),
        grid_spec=pltpu.PrefetchScalarGridSpec(num_scalar_prefetch=1, grid=grid, in_specs=[g_spec, t_spec],
                                               out_specs=t_spec),
        compiler_params=pltpu.CompilerParams(dimension_semantics=("parallel", "parallel")),
    )(core, grad, theirs)


def _chip_copies(sums, recv, sems):
    n = len(sums)
    x, y, c = _my_coords()
    out = []
    for j, (tx, ty) in enumerate(_other_chips(x, y)):
        for i in range(n):
            out.append(pltpu.make_async_remote_copy(
                src_ref=sums[i].at[2 * tx + ty], dst_ref=recv[i].at[2 * x + y],
                send_sem=sems[0].at[j * n + i], recv_sem=sems[1].at[j * n + i],
                device_id=(tx, ty, c), device_id_type=MESH))
    return out


def chip_exchange_start(name, sums):
    n = len(sums)

    def body(*refs):
        sems, token = (refs[2 * n], refs[2 * n + 1]), refs[4 * n + 2]
        for cp in _chip_copies(refs[:n], refs[n:2 * n], sems):
            cp.start()
        token[...] = jnp.zeros_like(token)

    both = [pltpu.HBM(a.shape, a.dtype) for a in sums]
    out = pl.pallas_call(
        body, name=name,
        out_shape=(pltpu.SemaphoreType.DMA((3 * n,)), pltpu.SemaphoreType.DMA((3 * n,)), *both, *both, TOKEN),
        in_specs=[HBM] * (2 * n), out_specs=(SEM, SEM, *[HBM] * (2 * n), pl.BlockSpec(memory_space=pltpu.VMEM)),
        input_output_aliases={i: 2 + i for i in range(2 * n)}, compiler_params=_split_copy_params(),
    )(*[_in_hbm(a) for a in sums], *[_in_hbm(lax.empty(a.shape, a.dtype)) for a in sums])
    return (out[0], out[1]), list(out[2:2 + n]), list(out[2 + n:2 + 2 * n]), out[-1]


def chip_exchange_finish(name, sums, recv, sems, after):
    n = len(sums)

    def body(*refs):
        for cp in _chip_copies(refs[:n], refs[n:2 * n], (refs[2 * n], refs[2 * n + 1])):
            cp.wait_send()
            cp.wait_recv()

    both = tuple(pltpu.HBM(a.shape, a.dtype) for a in sums)
    out = pl.pallas_call(
        body, name=name, out_shape=both + both,
        in_specs=[HBM] * (2 * n) + [SEM, SEM, ANY], out_specs=tuple([HBM] * (2 * n)),
        input_output_aliases={i: i for i in range(2 * n)}, compiler_params=_split_copy_params(),
    )(*sums, *recv, *sems, after)
    return list(out[:n]), list(out[n:])


def _mm(name, a, b, *, ta, tb, grid, tm, tn, tk, a_spec, b_spec, outs, extras=(), epilogue=None, alias=None,
        after=None):
    n_extra, n_out = len(extras), len(outs)
    nk = grid[2]
    dn = (((0,) if ta else (1,), (1,) if tb else (0,)), ((), ()))

    def finish(acc, extra_refs, out_refs):
        vals = epilogue(acc, *[r[...] for r in extra_refs]) if epilogue is not None else (acc,)
        for o_ref, v in zip(out_refs, vals):
            o_ref[...] = v.astype(o_ref.dtype)

    def body(*refs):
        a_ref, b_ref = refs[0], refs[1]
        extra_refs = refs[2:2 + n_extra]
        first_out = 2 + n_extra + (1 if alias is not None else 0) + (1 if after is not None else 0)
        out_refs = refs[first_out:first_out + n_out]
        part = lax.dot_general(a_ref[...], b_ref[...], dn, preferred_element_type=F32)
        if nk == 1:
            finish(part, extra_refs, out_refs)
            return
        acc_ref = refs[-1]
        k = pl.program_id(2)

        @pl.when(k == 0)
        def _():
            acc_ref[...] = part

        @pl.when(k > 0)
        def _():
            acc_ref[...] += part

        @pl.when(k == nk - 1)
        def _():
            finish(acc_ref[...], extra_refs, out_refs)

    operands = [a, b] + [e for e, _ in extras]
    in_specs = [a_spec, b_spec] + [s for _, s in extras]
    aliases = {}
    if alias is not None:
        aliases = {len(operands): 0}
        operands.append(alias)
        in_specs.append(ANY)
    if after is not None:
        operands.append(after)
        in_specs.append(ANY)
    blocks = _nbytes((tm, tk), a.dtype) + _nbytes((tk, tn), b.dtype)
    blocks += sum(_nbytes((tm, tn), s.dtype) for s, _ in outs)
    blocks += sum(_nbytes((e.shape[-2] if e.shape[-2] < tm else tm, tn), e.dtype) for e, _ in extras)
    scratch = [] if nk == 1 else [pltpu.VMEM((tm, tn), F32)]
    return pl.pallas_call(
        body, name=name, grid=grid,
        out_shape=[s for s, _ in outs], in_specs=in_specs, out_specs=[s for _, s in outs],
        input_output_aliases=aliases, scratch_shapes=scratch,
        compiler_params=pltpu.CompilerParams(
            dimension_semantics=("parallel", "parallel", "arbitrary"),
            vmem_limit_bytes=_vmem_limit(blocks, 0 if nk == 1 else 3 * tm * tn * 4)),
    )(*operands)


def _mm_tiles(m, n, k):
    return _tile(m, 512), _tile(n, 1024), _tile(k, 2048)


def mm_plain(name, a, b, *, ta=False, tb=False, out_dtype=F32, epilogue=None, extras=(), extra_kinds=(),
             out_dtypes=None, after=None):
    m = a.shape[1] if ta else a.shape[0]
    k = a.shape[0] if ta else a.shape[1]
    n = b.shape[0] if tb else b.shape[1]
    tm, tn, tk = _mm_tiles(m, n, k)
    grid = (m // tm, n // tn, k // tk)
    a_spec = pl.BlockSpec((tk, tm), lambda i, j, kk: (kk, i)) if ta else pl.BlockSpec((tm, tk), lambda i, j, kk: (i, kk))
    b_spec = pl.BlockSpec((tn, tk), lambda i, j, kk: (j, kk)) if tb else pl.BlockSpec((tk, tn), lambda i, j, kk: (kk, j))
    ex = []
    for e, kind in zip(extras, extra_kinds):
        if kind == "row":
            ex.append((e, pl.BlockSpec((1, tn), lambda i, j, kk: (0, j))))
        else:
            ex.append((e, pl.BlockSpec((tm, tn), lambda i, j, kk: (i, j))))
    out_dtypes = out_dtypes or [out_dtype]
    outs = [(jax.ShapeDtypeStruct((m, n), dt), pl.BlockSpec((tm, tn), lambda i, j, kk: (i, j))) for dt in out_dtypes]
    res = _mm(name, a, b, ta=ta, tb=tb, grid=grid, tm=tm, tn=tn, tk=tk, a_spec=a_spec, b_spec=b_spec,
              outs=outs, extras=ex, epilogue=epilogue, after=after)
    return res if len(res) > 1 else res[0]


def mm_in_proj(name, h, w, bias, col0, parts):
    t, d = h.shape
    tm, tn, tk = _mm_tiles(t, d, d)
    per = d // tn
    off = col0 // tn
    grid = (t // tm, parts * per, d // tk)
    outs = [(jax.ShapeDtypeStruct((parts, t, d), F32),
             pl.BlockSpec((None, tm, tn), lambda i, j, kk: (j // per, i, j % per)))]
    ex = [(bias, pl.BlockSpec((1, tn), lambda i, j, kk: (0, j + off)))]
    return _mm(name, h, w, ta=False, tb=False, grid=grid, tm=tm, tn=tn, tk=tk,
               a_spec=pl.BlockSpec((tm, tk), lambda i, j, kk: (i, kk)),
               b_spec=pl.BlockSpec((tk, tn), lambda i, j, kk: (kk, j + off)),
               outs=outs, extras=ex, epilogue=lambda acc, bb: (acc + bb,))[0]


def mm_in_proj_dw(name, h, dz, col0, n_total, prev=None, after=None):
    parts, t, d = dz.shape
    tm, tn, tk = _mm_tiles(d, d, t)
    per = d // tn
    off = col0 // tn
    grid = (d // tm, parts * per, t // tk)
    outs = [(jax.ShapeDtypeStruct((d, n_total), BF16), pl.BlockSpec((tm, tn), lambda i, j, kk: (i, j + off)))]
    return _mm(name, h, dz, ta=True, tb=False, grid=grid, tm=tm, tn=tn, tk=tk,
               a_spec=pl.BlockSpec((tk, tm), lambda i, j, kk: (kk, i)),
               b_spec=pl.BlockSpec((None, tk, tn), lambda i, j, kk: (j // per, kk, j % per)),
               outs=outs, alias=prev, after=after)[0]


def mm_in_proj_dh(name, dz, w, col0, res=None, after=None):
    parts, t, d = dz.shape
    tm, tn, tk = _mm_tiles(t, d, d)
    per = d // tk
    off = col0 // tk
    grid = (t // tm, d // tn, parts * per)
    outs = [(jax.ShapeDtypeStruct((t, d), F32), pl.BlockSpec((tm, tn), lambda i, j, kk: (i, j)))]
    ex, epi = [], None
    if res is not None:
        ex = [(res, pl.BlockSpec((tm, tn), lambda i, j, kk: (i, j)))]
        epi = lambda acc, r: (acc + r,)
    return _mm(name, dz, w, ta=False, tb=True, grid=grid, tm=tm, tn=tn, tk=tk,
               a_spec=pl.BlockSpec((None, tm, tk), lambda i, j, kk: (kk // per, i, kk % per)),
               b_spec=pl.BlockSpec((tn, tk), lambda i, j, kk: (j, kk + off)),
               outs=outs, extras=ex, epilogue=epi, after=after)[0]


ROW_TILE = 256


def rmsnorm_fwd(name, x, g, after=None):
    t, d = x.shape
    tr = _tile(t, ROW_TILE, SUBLANES)
    order = [] if after is None else [after]

    def body(x_ref, g_ref, *rest):
        h_ref = rest[-1]
        xv = x_ref[...]
        rstd = lax.rsqrt(jnp.mean(xv * xv, axis=-1, keepdims=True) + EPS)
        h_ref[...] = (xv * rstd * g_ref[...]).astype(BF16)

    return pl.pallas_call(
        body, name=name, grid=(t // tr,),
        out_shape=jax.ShapeDtypeStruct((t, d), BF16),
        in_specs=[pl.BlockSpec((tr, d), lambda i: (i, 0)), pl.BlockSpec((1, d), lambda i: (0, 0))] + [ANY] * len(order),
        out_specs=pl.BlockSpec((tr, d), lambda i: (i, 0)),
        compiler_params=pltpu.CompilerParams(dimension_semantics=("parallel",)),
    )(x, g, *order)


def _rmsnorm_bwd_math(xv, gv, dout):
    rstd = lax.rsqrt(jnp.mean(xv * xv, axis=-1, keepdims=True) + EPS)
    xhat = xv * rstd
    dy = dout * gv
    dx = rstd * (dy - xhat * jnp.mean(dy * xhat, axis=-1, keepdims=True))
    dg = jnp.sum(dout * xhat, axis=0, keepdims=True)
    return dx, dg


def rmsnorm_bwd(name, dh, x, g, dres):
    t, d = x.shape
    tr = _tile(t, ROW_TILE, SUBLANES)

    def body(dh_ref, x_ref, g_ref, dres_ref, dx_ref, dxb_ref, dg_ref):
        dx, dg = _rmsnorm_bwd_math(x_ref[...], g_ref[...], dh_ref[...])
        dx = dx + dres_ref[...]
        dx_ref[...] = dx
        dxb_ref[...] = dx.astype(BF16)

        @pl.when(pl.program_id(0) == 0)
        def _():
            dg_ref[...] = dg

        @pl.when(pl.program_id(0) > 0)
        def _():
            dg_ref[...] += dg

    row = pl.BlockSpec((tr, d), lambda i: (i, 0))
    vec = pl.BlockSpec((1, d), lambda i: (0, 0))
    return pl.pallas_call(
        body, name=name, grid=(t // tr,),
        out_shape=[jax.ShapeDtypeStruct((t, d), F32), jax.ShapeDtypeStruct((t, d), BF16),
                   jax.ShapeDtypeStruct((1, d), F32)],
        in_specs=[row, row, vec, row], out_specs=[row, row, vec],
        compiler_params=pltpu.CompilerParams(dimension_semantics=("arbitrary",)),
    )(dh, x, g, dres)


def loss_head(name, x, g, target):
    t, d = x.shape
    tr = _tile(t, ROW_TILE, SUBLANES)

    def body(x_ref, g_ref, tgt_ref, dx_ref, dxb_ref, dg_ref, loss_ref):
        xv, gv = x_ref[...], g_ref[...]
        rstd = lax.rsqrt(jnp.mean(xv * xv, axis=-1, keepdims=True) + EPS)
        err = xv * rstd * gv - tgt_ref[...]
        lsum = (0.5 / d) * jnp.sum(err * err, axis=0, keepdims=True)
        dx, dg = _rmsnorm_bwd_math(xv, gv, err * (1.0 / d))
        dx_ref[...] = dx
        dxb_ref[...] = dx.astype(BF16)

        @pl.when(pl.program_id(0) == 0)
        def _():
            dg_ref[...] = dg
            loss_ref[...] = lsum

        @pl.when(pl.program_id(0) > 0)
        def _():
            dg_ref[...] += dg
            loss_ref[...] += lsum

    row = pl.BlockSpec((tr, d), lambda i: (i, 0))
    vec = pl.BlockSpec((1, d), lambda i: (0, 0))
    return pl.pallas_call(
        body, name=name, grid=(t // tr,),
        out_shape=[jax.ShapeDtypeStruct((t, d), F32), jax.ShapeDtypeStruct((t, d), BF16),
                   jax.ShapeDtypeStruct((1, d), F32), jax.ShapeDtypeStruct((1, d), F32)],
        in_specs=[row, vec, row], out_specs=[row, row, vec, vec],
        compiler_params=pltpu.CompilerParams(dimension_semantics=("arbitrary",)),
    )(x, g, target)


TIME_TILE = 256


def _sigmoid(v):
    return 1.0 / (1.0 + jnp.exp(-v))


def _gelu(v):
    return 0.5 * v * (1.0 + jnp.tanh(GELU_C * (v + GELU_A * v * v * v)))


def _gelu_grad(v):
    th = jnp.tanh(GELU_C * (v + GELU_A * v * v * v))
    return 0.5 * (1.0 + th) + 0.5 * v * (1.0 - th * th) * GELU_C * (1.0 + 3.0 * GELU_A * v * v)


def _one_minus_exp(v):
    series = -v * (1.0 + v * (0.5 + v * (1.0 / 6.0 + v * (1.0 / 24.0))))
    return jnp.where(v > -0.05, series, 1.0 - jnp.exp(v))


def _back(cur, halo, s, row):
    tt, cb = cur.shape
    pad = jnp.concatenate([pltpu.roll(halo, s, 0), jnp.zeros((tt - SUBLANES, cb), F32)], axis=0)
    return jnp.where(row < s, pad, pltpu.roll(cur, s, 0))


def _ahead(cur, halo, s, row):
    tt, cb = cur.shape
    pad = jnp.concatenate([jnp.zeros((tt - SUBLANES, cb), F32), pltpu.roll(halo, SUBLANES - s, 0)], axis=0)
    return jnp.where(row >= tt - s, pad, pltpu.roll(cur, tt - s, 0))


def _lru_gates(u, wr, wi, br, bi, lam):
    ub = u.astype(BF16)
    r = _sigmoid(jnp.dot(ub, wr, preferred_element_type=F32) + br)
    i = _sigmoid(jnp.dot(ub, wi, preferred_element_type=F32) + bi)
    sp = jnp.maximum(-lam, 0.0) + jnp.log1p(jnp.exp(-jnp.abs(lam)))
    log_a = -LRU_C * r * sp
    a = jnp.exp(log_a)
    mult = jnp.sqrt(_one_minus_exp(2.0 * log_a))
    return ub, r, i, a, mult


def _branch_specs(tt, cb, nt, time_of):
    per = tt // SUBLANES

    def tile(parts):
        if parts is None:
            return pl.BlockSpec((tt, cb), lambda n, s: (time_of(s), n))
        return pl.BlockSpec((parts, tt, cb), lambda n, s: (0, time_of(s), n))

    def halo_before(parts):
        if parts is None:
            return pl.BlockSpec((SUBLANES, cb), lambda n, s: (jnp.maximum(time_of(s) * per - 1, 0), n))
        return pl.BlockSpec((parts, SUBLANES, cb), lambda n, s: (0, jnp.maximum(time_of(s) * per - 1, 0), n))

    def rows(k):
        return pl.BlockSpec((k, cb), lambda n, s: (0, n))

    gate_w = pl.BlockSpec((None, cb, cb), lambda n, s: (n, 0, 0))
    return tile, halo_before, rows, gate_w


def branch_fwd(name, z5, caw, cab, cbw, wr, wi, br, bi, lam):
    _, t, d = z5.shape
    cb = wr.shape[-1]
    tt = _tile(t, TIME_TILE, SUBLANES)
    nt = t // tt
    tile, halo_before, rows, gate_w = _branch_specs(tt, cb, nt, lambda s: s)

    def body(z_ref, zh_ref, caw_ref, cab_ref, cbw_ref, wr_ref, wi_ref, br_ref, bi_ref, lam_ref,
             pa_ref, pb_ref, h_ref, carry_ref, acum_ref, bcum_ref):
        s = pl.program_id(1)
        row = lax.broadcasted_iota(jnp.int32, (tt, cb), 0)
        keep = jnp.where(s > 0, 1.0, 0.0)

        @pl.when(s == 0)
        def _():
            carry_ref[...] = jnp.zeros_like(carry_ref)

        xa, ya, cbv, ccv, cxv = (z_ref[p] for p in range(N_BRANCH))
        xa_h = zh_ref[0] * keep
        caw_v = caw_ref[...]
        u = caw_v[3:4] * xa + cab_ref[...]
        for j in range(1, CONV_A_WIDTH):
            u = u + caw_v[3 - j:4 - j] * _back(xa, xa_h, j, row)
        _, _, gi, a, mult = _lru_gates(u, wr_ref[...], wi_ref[...], br_ref[...], bi_ref[...], lam_ref[...])
        b = mult * (gi * u)

        rm = row & (SUBLANES - 1)
        for sh in (1, 2, 4):
            a_prev = jnp.where(rm >= sh, pltpu.roll(a, sh, 0), 1.0)
            b_prev = jnp.where(rm >= sh, pltpu.roll(b, sh, 0), 0.0)
            b = a * b_prev + b
            a = a * a_prev
        acum_ref[...] = a
        bcum_ref[...] = b

        def group(g, h_in):
            r0 = pl.multiple_of(g * SUBLANES, SUBLANES)
            hg = acum_ref[pl.ds(r0, SUBLANES), :] * h_in + bcum_ref[pl.ds(r0, SUBLANES), :]
            h_ref[pl.ds(r0, SUBLANES), :] = hg
            return jnp.broadcast_to(hg[SUBLANES - 1:SUBLANES, :], (SUBLANES, cb))

        carry_ref[...] = lax.fori_loop(0, tt // SUBLANES, group, carry_ref[...], unroll=4)
        pa_ref[...] = (h_ref[...] * _gelu(ya)).astype(BF16)

        q = ccv * cxv
        q_h = zh_ref[3] * zh_ref[4] * keep
        cbw_v = cbw_ref[...]
        v = cbw_v[2:3] * q
        for j in range(1, CONV_B_WIDTH):
            v = v + cbw_v[2 - j:3 - j] * _back(q, q_h, j, row)
        pb_ref[...] = (cbv * v).astype(BF16)

    act = jax.ShapeDtypeStruct((t, d), BF16)
    blocks = _nbytes((2 * N_BRANCH, tt, cb), F32) + _nbytes((2, cb, cb), BF16) + _nbytes((4, tt, cb), F32)
    return pl.pallas_call(
        body, name=name, grid=(d // cb, nt),
        out_shape=[act, act, jax.ShapeDtypeStruct((t, d), F32)],
        in_specs=[tile(N_BRANCH), halo_before(N_BRANCH), rows(CONV_A_WIDTH), rows(1), rows(CONV_B_WIDTH),
                  gate_w, gate_w, rows(1), rows(1), rows(1)],
        out_specs=[tile(None), tile(None), tile(None)],
        scratch_shapes=[pltpu.VMEM((SUBLANES, cb), F32), pltpu.VMEM((tt, cb), F32), pltpu.VMEM((tt, cb), F32)],
        compiler_params=pltpu.CompilerParams(dimension_semantics=("parallel", "arbitrary"),
                                             vmem_limit_bytes=_vmem_limit(blocks, 40 * tt * cb * 4)),
    )(z5, z5, caw, cab, cbw, wr, wi, br, bi, lam)


def branch_bwd(name, z5, hl, dpa, dpb, caw, cab, cbw, wr, wi, br, bi, lam):
    _, t, d = z5.shape
    cb = wr.shape[-1]
    tt = _tile(t, TIME_TILE, SUBLANES)
    nt = t // tt
    tile, halo_before, rows, gate_w = _branch_specs(tt, cb, nt, lambda s: nt - 1 - s)

    def body(z_ref, zh_ref, h_ref, hh_ref, dpa_ref, dpb_ref, caw_ref, cab_ref, cbw_ref, wr_ref, wi_ref,
             br_ref, bi_ref, lam_ref,
             dz_ref, dbias_ref, dcaw_ref, dcab_ref, dcbw_ref, dwr_ref, dwi_ref, dbr_ref, dbi_ref, dlam_ref,
             g_carry_ref, a_next_ref, du_next_ref, dv_next_ref, acum_ref, bcum_ref, gout_ref):
        s = pl.program_id(1)
        row = lax.broadcasted_iota(jnp.int32, (tt, cb), 0)
        keep = jnp.where(s < nt - 1, 1.0, 0.0)

        @pl.when(s == 0)
        def _():
            for r in (g_carry_ref, a_next_ref, du_next_ref, dv_next_ref, dbias_ref, dcaw_ref, dcab_ref, dcbw_ref,
                      dwr_ref, dwi_ref, dbr_ref, dbi_ref, dlam_ref):
                r[...] = jnp.zeros_like(r)

        xa, ya, cbv, ccv, cxv = (z_ref[p] for p in range(N_BRANCH))
        xa_h = zh_ref[0] * keep
        caw_v = caw_ref[...]
        wr_v, wi_v, lam_v = wr_ref[...], wi_ref[...], lam_ref[...]
        xa_back = [xa] + [_back(xa, xa_h, j, row) for j in range(1, CONV_A_WIDTH)]
        u = caw_v[3:4] * xa + cab_ref[...]
        for j in range(1, CONV_A_WIDTH):
            u = u + caw_v[3 - j:4 - j] * xa_back[j]
        ub, gr, gi, a, mult = _lru_gates(u, wr_v, wi_v, br_ref[...], bi_ref[...], lam_v)

        hv = h_ref[...]
        dpa_v = dpa_ref[...]
        dya = dpa_v * hv * _gelu_grad(ya)
        dh = dpa_v * _gelu(ya)

        a_up = _ahead(a, a_next_ref[...], 1, row)
        rm = row & (SUBLANES - 1)
        ca, cg = a_up, dh
        for sh in (1, 2, 4):
            a_nxt = jnp.where(rm + sh < SUBLANES, pltpu.roll(ca, tt - sh, 0), 1.0)
            g_nxt = jnp.where(rm + sh < SUBLANES, pltpu.roll(cg, tt - sh, 0), 0.0)
            cg = ca * g_nxt + cg
            ca = ca * a_nxt
        acum_ref[...] = ca
        bcum_ref[...] = cg
        n_groups = tt // SUBLANES

        def group(k, g_in):
            r0 = pl.multiple_of((n_groups - 1 - k) * SUBLANES, SUBLANES)
            gg = acum_ref[pl.ds(r0, SUBLANES), :] * g_in + bcum_ref[pl.ds(r0, SUBLANES), :]
            gout_ref[pl.ds(r0, SUBLANES), :] = gg
            return jnp.broadcast_to(gg[0:1, :], (SUBLANES, cb))

        g_carry_ref[...] = lax.fori_loop(0, n_groups, group, g_carry_ref[...], unroll=4)
        a_next_ref[...] = a[0:SUBLANES, :]
        gv = gout_ref[...]

        h_prev = _back(hv, hh_ref[...] * keep, 1, row)
        da = gv * h_prev
        iu = gi * u
        dmult = gv * iu
        di = gv * mult * u
        du = gv * mult * gi
        dla = da * a - dmult * (a * a) / mult
        sp = jnp.maximum(-lam_v, 0.0) + jnp.log1p(jnp.exp(-jnp.abs(lam_v)))
        dr = dla * (-LRU_C * sp)
        dlam_ref[...] += jnp.sum(dla * gr, axis=0, keepdims=True) * (LRU_C * _sigmoid(-lam_v))
        drp = dr * gr * (1.0 - gr)
        dip = di * gi * (1.0 - gi)
        dbr_ref[...] += jnp.sum(drp, axis=0, keepdims=True)
        dbi_ref[...] += jnp.sum(dip, axis=0, keepdims=True)
        drp_b, dip_b = drp.astype(BF16), dip.astype(BF16)
        tn_dims = (((0,), (0,)), ((), ()))
        nt_dims = (((1,), (1,)), ((), ()))
        dwr_ref[...] += lax.dot_general(ub, drp_b, tn_dims, preferred_element_type=F32)
        dwi_ref[...] += lax.dot_general(ub, dip_b, tn_dims, preferred_element_type=F32)
        du = du + lax.dot_general(drp_b, wr_v, nt_dims, preferred_element_type=F32)
        du = du + lax.dot_general(dip_b, wi_v, nt_dims, preferred_element_type=F32)

        du_h = du_next_ref[...]
        dxa = caw_v[3:4] * du
        dcaw_ref[3:4, :] += jnp.sum(du * xa, axis=0, keepdims=True)
        for j in range(1, CONV_A_WIDTH):
            dxa = dxa + caw_v[3 - j:4 - j] * _ahead(du, du_h, j, row)
            dcaw_ref[3 - j:4 - j, :] += jnp.sum(du * xa_back[j], axis=0, keepdims=True)
        dcab_ref[...] += jnp.sum(du, axis=0, keepdims=True)
        du_next_ref[...] = du[0:SUBLANES, :]

        q = ccv * cxv
        q_h = zh_ref[3] * zh_ref[4] * keep
        cbw_v = cbw_ref[...]
        q_back = [q] + [_back(q, q_h, j, row) for j in range(1, CONV_B_WIDTH)]
        v = cbw_v[2:3] * q
        for j in range(1, CONV_B_WIDTH):
            v = v + cbw_v[2 - j:3 - j] * q_back[j]
        dpb_v = dpb_ref[...]
        dcb = dpb_v * v
        dv = dpb_v * cbv
        dv_h = dv_next_ref[...]
        dq = cbw_v[2:3] * dv
        dcbw_ref[2:3, :] += jnp.sum(dv * q, axis=0, keepdims=True)
        for j in range(1, CONV_B_WIDTH):
            dq = dq + cbw_v[2 - j:3 - j] * _ahead(dv, dv_h, j, row)
            dcbw_ref[2 - j:3 - j, :] += jnp.sum(dv * q_back[j], axis=0, keepdims=True)
        dv_next_ref[...] = dv[0:SUBLANES, :]
        dcc = dq * cxv
        dcx = dq * ccv

        for p, val in enumerate((dxa, dya, dcb, dcc, dcx)):
            dz_ref[p] = val.astype(BF16)
            dbias_ref[p:p + 1, :] += jnp.sum(val, axis=0, keepdims=True)

    def acc_rows(k):
        return jax.ShapeDtypeStruct((k, d), F32)

    nb = d // cb
    gate_grad = jax.ShapeDtypeStruct((nb, cb, cb), F32)
    small = pltpu.VMEM((SUBLANES, cb), F32)
    big = pltpu.VMEM((tt, cb), F32)
    blocks = _nbytes((2 * (N_BRANCH + 3), tt, cb), F32) + _nbytes((4, cb, cb), F32)
    return pl.pallas_call(
        body, name=name, grid=(nb, nt),
        out_shape=[jax.ShapeDtypeStruct((N_BRANCH, t, d), BF16), acc_rows(N_BRANCH), acc_rows(CONV_A_WIDTH),
                   acc_rows(1), acc_rows(CONV_B_WIDTH), gate_grad, gate_grad, acc_rows(1), acc_rows(1), acc_rows(1)],
        in_specs=[tile(N_BRANCH), halo_before(N_BRANCH), tile(None), halo_before(None), tile(None), tile(None),
                  rows(CONV_A_WIDTH), rows(1), rows(CONV_B_WIDTH), gate_w, gate_w, rows(1), rows(1), rows(1)],
        out_specs=[tile(N_BRANCH), rows(N_BRANCH), rows(CONV_A_WIDTH), rows(1), rows(CONV_B_WIDTH),
                   gate_w, gate_w, rows(1), rows(1), rows(1)],
        scratch_shapes=[small, small, small, small, big, big, big],
        compiler_params=pltpu.CompilerParams(dimension_semantics=("parallel", "arbitrary"),
                                             vmem_limit_bytes=_vmem_limit(blocks, 60 * tt * cb * 4)),
    )(z5, z5, hl, hl, dpa, dpb, caw, cab, cbw, wr, wi, br, bi, lam)


MERGE_COLS = 512


def merge_fwd(name, z2, oa, ob):
    t, d = oa.shape
    tr, tc = _tile(t, ROW_TILE, SUBLANES), _tile(d, MERGE_COLS)

    def body(z_ref, oa_ref, ob_ref, m_ref):
        m_ref[...] = (_sigmoid(z_ref[0]) * oa_ref[...] + _sigmoid(z_ref[1]) * ob_ref[...]).astype(BF16)

    blk = pl.BlockSpec((tr, tc), lambda i, j: (i, j))
    return pl.pallas_call(
        body, name=name, grid=(t // tr, d // tc),
        out_shape=jax.ShapeDtypeStruct((t, d), BF16),
        in_specs=[pl.BlockSpec((2, tr, tc), lambda i, j: (0, i, j)), blk, blk], out_specs=blk,
        compiler_params=pltpu.CompilerParams(dimension_semantics=("parallel", "parallel")),
    )(z2, oa, ob)


def merge_bwd(name, dm, z2, oa, ob):
    t, d = oa.shape
    tr, tc = _tile(t, ROW_TILE, SUBLANES), _tile(d, MERGE_COLS)

    def body(dm_ref, z_ref, oa_ref, ob_ref, doa_ref, dob_ref, dz_ref, dbias_ref):
        dmv = dm_ref[...]
        sa, sb = _sigmoid(z_ref[0]), _sigmoid(z_ref[1])
        doa_ref[...] = (dmv * sa).astype(BF16)
        dob_ref[...] = (dmv * sb).astype(BF16)
        dga = dmv * oa_ref[...] * sa * (1.0 - sa)
        dgb = dmv * ob_ref[...] * sb * (1.0 - sb)
        dz_ref[0] = dga.astype(BF16)
        dz_ref[1] = dgb.astype(BF16)

        @pl.when(pl.program_id(1) == 0)
        def _():
            dbias_ref[...] = jnp.zeros_like(dbias_ref)

        dbias_ref[0:1, :] += jnp.sum(dga, axis=0, keepdims=True)
        dbias_ref[1:2, :] += jnp.sum(dgb, axis=0, keepdims=True)

    blk = pl.BlockSpec((tr, tc), lambda j, i: (i, j))
    pair = pl.BlockSpec((2, tr, tc), lambda j, i: (0, i, j))
    act = jax.ShapeDtypeStruct((t, d), BF16)
    return pl.pallas_call(
        body, name=name, grid=(d // tc, t // tr),
        out_shape=[act, act, jax.ShapeDtypeStruct((2, t, d), BF16), jax.ShapeDtypeStruct((2, d), F32)],
        in_specs=[blk, pair, blk, blk],
        out_specs=[blk, blk, pair, pl.BlockSpec((2, tc), lambda j, i: (0, j))],
        compiler_params=pltpu.CompilerParams(dimension_semantics=("parallel", "arbitrary")),
    )(dm, z2, oa, ob)


def _adamw_math(w, g, m, v):
    m = ADAM_B1 * m + (1.0 - ADAM_B1) * g
    v = ADAM_B2 * v + (1.0 - ADAM_B2) * (g * g)
    m_hat = m / (1.0 - ADAM_B1 ** ADAM_STEP)
    v_hat = v / (1.0 - ADAM_B2 ** ADAM_STEP)
    delta = -ADAM_LR * (m_hat / (jnp.sqrt(v_hat) + ADAM_EPS) + ADAM_WD * w)
    return delta, m, v


def _sum_in_order(ref):
    g = ref[0].astype(F32)
    for p in range(1, ref.shape[0]):
        g = g + ref[p].astype(F32)
    return g


def adamw_sharded(name, recv, own, chip, w, m, v):
    r, c = w.shape
    rows = r // DEPTH
    tr = _tile(rows, max(SUBLANES, (256 * 1024) // c), 16)
    per = rows // tr

    def body(chip_ref, *refs):
        recv_refs, own_refs = refs[:DEPTH], refs[DEPTH:2 * DEPTH]
        w_ref, m_ref, v_ref, g_ref, d_ref, mo_ref, vo_ref = refs[2 * DEPTH:]
        step = pl.program_id(0)
        for l in range(DEPTH):
            @pl.when((step >= l * per) & (step < (l + 1) * per))
            def _(l=l):
                g = None
                for q in range(N_CHIP):
                    part = jnp.where(chip_ref[0] == q, own_refs[l][...], recv_refs[l][q]).astype(F32)
                    g = part if g is None else g + part
                delta, m2, v2 = _adamw_math(w_ref[...], g, m_ref[...], v_ref[...])
                g_ref[...] = g
                d_ref[...] = delta
                mo_ref[...] = m2
                vo_ref[...] = v2

    def in_layer(l, i):
        return jnp.clip(i - l * per, 0, per - 1)

    recv_specs = [pl.BlockSpec((N_CHIP, tr, c), lambda i, chip_ref, l=l: (0, in_layer(l, i), 0)) for l in range(DEPTH)]
    own_specs = [pl.BlockSpec((None, tr, c), lambda i, chip_ref, l=l: (chip_ref[0], in_layer(l, i), 0))
                 for l in range(DEPTH)]
    blk = pl.BlockSpec((tr, c), lambda i, chip_ref: (i, 0))
    out = jax.ShapeDtypeStruct((r, c), F32)
    blocks = DEPTH * _nbytes((N_CHIP + 1, tr, c), BF16) + 7 * _nbytes((tr, c), F32)
    return pl.pallas_call(
        body, name=name, out_shape=[out, out, out, out],
        grid_spec=pltpu.PrefetchScalarGridSpec(
            num_scalar_prefetch=1, grid=(DEPTH * per,), in_specs=recv_specs + own_specs + [blk, blk, blk],
            out_specs=[blk, blk, blk, blk]),
        compiler_params=pltpu.CompilerParams(dimension_semantics=("parallel",),
                                             vmem_limit_bytes=_vmem_limit(blocks)),
    )(chip, *recv, *own, w, m, v)


def sum_partials(name, parts):
    _, r, c = parts.shape

    def body(p_ref, o_ref):
        o_ref[...] = _sum_in_order(p_ref)

    return pl.pallas_call(body, name=name, out_shape=jax.ShapeDtypeStruct((r, c), F32))(parts)


def adamw_small(name, g, w, m, v):
    def body(g_ref, w_ref, m_ref, v_ref, d_ref, mo_ref, vo_ref):
        delta, m2, v2 = _adamw_math(w_ref[...], g_ref[...], m_ref[...], v_ref[...])
        d_ref[...] = delta
        mo_ref[...] = m2
        vo_ref[...] = v2

    out = jax.ShapeDtypeStruct(w.shape, F32)
    return pl.pallas_call(body, name=name, out_shape=[out, out, out])(g, w, m, v)


def _pack(arrays):
    flat = jnp.concatenate([a.reshape(-1) for a in arrays])
    n = flat.shape[0]
    rows = -(-n // (SUBLANES * LANES)) * SUBLANES
    flat = jnp.pad(flat, (0, rows * LANES - n))
    return flat.reshape(rows, LANES)


def _unpack(packed, shapes):
    flat = packed.reshape(-1)
    out, off = [], 0
    for s in shapes:
        n = math.prod(s)
        out.append(flat[off:off + n].reshape(s))
        off += n
    return out


def kernel(x, norm1_g, w_in, b_in, conv_a_w, conv_a_b, lru_wr, lru_br, lru_wi, lru_bi, lru_lam, conv_b_w, w_pa, w_pb, w_o, norm2_g, w_mlp1, w_mlp2, final_g, loss_target, m_norm1_g, m_w_in, m_b_in, m_conv_a_w, m_conv_a_b, m_lru_wr, m_lru_br, m_lru_wi, m_lru_bi, m_lru_lam, m_conv_b_w, m_w_pa, m_w_pb, m_w_o, m_norm2_g, m_w_mlp1, m_w_mlp2, m_final_g, v_norm1_g, v_w_in, v_b_in, v_conv_a_w, v_conv_a_b, v_lru_wr, v_lru_br, v_lru_wi, v_lru_bi, v_lru_lam, v_conv_b_w, v_w_pa, v_w_pb, v_w_o, v_norm2_g, v_w_mlp1, v_w_mlp2, v_final_g):
    weights = dict(norm1_g=norm1_g, w_in=w_in, b_in=b_in, conv_a_w=conv_a_w, conv_a_b=conv_a_b, lru_wr=lru_wr,
                   lru_br=lru_br, lru_wi=lru_wi, lru_bi=lru_bi, lru_lam=lru_lam, conv_b_w=conv_b_w, w_pa=w_pa,
                   w_pb=w_pb, w_o=w_o, norm2_g=norm2_g, w_mlp1=w_mlp1, w_mlp2=w_mlp2, final_g=final_g)
    mom1 = dict(norm1_g=m_norm1_g, w_in=m_w_in, b_in=m_b_in, conv_a_w=m_conv_a_w, conv_a_b=m_conv_a_b,
                lru_wr=m_lru_wr, lru_br=m_lru_br, lru_wi=m_lru_wi, lru_bi=m_lru_bi, lru_lam=m_lru_lam,
                conv_b_w=m_conv_b_w, w_pa=m_w_pa, w_pb=m_w_pb, w_o=m_w_o, norm2_g=m_norm2_g, w_mlp1=m_w_mlp1,
                w_mlp2=m_w_mlp2, final_g=m_final_g)
    mom2 = dict(norm1_g=v_norm1_g, w_in=v_w_in, b_in=v_b_in, conv_a_w=v_conv_a_w, conv_a_b=v_conv_a_b,
                lru_wr=v_lru_wr, lru_br=v_lru_br, lru_wi=v_lru_wi, lru_bi=v_lru_bi, lru_lam=v_lru_lam,
                conv_b_w=v_conv_b_w, w_pa=v_w_pa, w_pb=v_w_pb, w_o=v_w_o, norm2_g=v_norm2_g, w_mlp1=v_w_mlp1,
                w_mlp2=v_w_mlp2, final_g=v_final_g)
    names = list(weights)

    t, d = x.shape[1], x.shape[2]
    n_in = b_in.shape[1]
    nb, bw = lru_wr.shape[1], lru_wr.shape[3]
    mx, my, mc = _my_coords()
    me = _flat(mx, my, mc)
    me_index = me.astype(jnp.int32).reshape(1)
    my_core = mc.astype(jnp.int32).reshape(1)
    my_chip = (2 * mx + my).astype(jnp.int32).reshape(1)
    x0 = x.reshape(t, d)
    target = loss_target.reshape(t, d)

    big = ["w_in", "w_pa", "w_pb", "w_o", "w_mlp1", "w_mlp2", "lru_wr", "lru_wi"]
    big_axis = dict(w_in=1, w_pa=0, w_pb=0, w_o=0, w_mlp1=1, w_mlp2=0, lru_wr=1, lru_wi=1)
    small_sharded = ["conv_a_w", "conv_b_w", "lru_br", "lru_bi"]
    small_packed = _pack([weights[k] for k in small_sharded])
    small_rows = all_gather("gather_small_params", [small_packed[None]], [0])[0].reshape(N_DEV, -1)
    groups = (("w_in", "lru_wr", "lru_wi"), ("w_pa", "w_pb", "w_o"), ("w_mlp1", "w_mlp2"))
    gathers, full = {}, [{} for _ in range(DEPTH)]
    token = small_rows
    for l in range(DEPTH):
        for gi, group in enumerate(groups):
            axes = [big_axis[k] for k in group]
            lands = [place_shard(f"place_{k}_l{l}", weights[k], l, big_axis[k], me_index) for k in group]
            own_sems, flying, token = gather_start(f"gather_start_l{l}g{gi}", lands, axes, after=token)
            gathers[l, gi] = dict(axes=axes, own=own_sems, flying=flying)

    def pass_on(l, gi, after):
        st = gathers[l, gi]
        st["passed"], st["flying"] = gather_pass_on(f"gather_pass_on_l{l}g{gi}", st["flying"], st["axes"], st["own"], after)

    def finish(l, gi, after):
        st = gathers[l, gi]
        out = gather_finish(f"gather_finish_l{l}g{gi}", st["flying"], st["axes"], st["own"], st["passed"], after)
        full[l].update(zip(groups[gi], out))

    pass_on(0, 0, token)
    finish(0, 0, token)
    small_full = {}
    off = 0
    for k in small_sharded:
        shard_shape = weights[k].shape
        n = math.prod(shard_shape)
        g = small_rows[:, off:off + n].reshape(N_DEV, *shard_shape)
        small_full[k] = jnp.moveaxis(g, 0, 2).reshape(shard_shape[0], shard_shape[1], N_DEV * shard_shape[2])
        off += n

    def layer_params(l):
        return dict(
            g1=norm1_g[l][None], b_in=b_in[l][None], caw=small_full["conv_a_w"][l], cab=conv_a_b[l][None],
            br=small_full["lru_br"][l].reshape(1, d), bi=small_full["lru_bi"][l].reshape(1, d),
            lam=lru_lam[l][None], cbw=small_full["conv_b_w"][l], g2=norm2_g[l][None])

    params = [layer_params(l) for l in range(DEPTH)]

    relu2 = lambda acc: (acc, jnp.square(jnp.maximum(acc, 0.0)))
    add_res = lambda acc, r: (acc + r,)
    relu2_bwd = lambda acc, pre: (acc * (2.0 * jnp.maximum(pre, 0.0)),)

    saved = []
    xl = x0
    for l in range(DEPTH):
        p, w = params[l], full[l]
        tag = f"l{l}"
        h1 = rmsnorm_fwd(f"norm1_{tag}", xl, p["g1"], after=token if l == 0 else None)
        z5 = mm_in_proj(f"in_proj5_{tag}", h1, w["w_in"], p["b_in"], 0, N_BRANCH)
        z2 = mm_in_proj(f"in_proj2_{tag}", h1, w["w_in"], p["b_in"], N_BRANCH * d, N_SPLIT - N_BRANCH)
        pass_on(l, 1, z2)
        pa, pb, hl = branch_fwd(f"branch_fwd_{tag}", z5, p["caw"], p["cab"], p["cbw"], w["lru_wr"], w["lru_wi"],
                                p["br"], p["bi"], p["lam"])
        finish(l, 1, pa)
        oa = mm_plain(f"proj_a_{tag}", pa, w["w_pa"])
        ob = mm_plain(f"proj_b_{tag}", pb, w["w_pb"])
        pass_on(l, 2, ob)
        mg = merge_fwd(f"merge_{tag}", z2, oa, ob)
        x1 = mm_plain(f"proj_o_{tag}", mg, w["w_o"], epilogue=add_res, extras=(xl,), extra_kinds=("full",))
        h2 = rmsnorm_fwd(f"norm2_{tag}", x1, p["g2"])
        finish(l, 2, h2)
        pre, uu = mm_plain(f"mlp1_{tag}", h2, w["w_mlp1"], epilogue=relu2, out_dtypes=[F32, BF16])
        if l + 1 < DEPTH:
            pass_on(l + 1, 0, pre)
        x2 = mm_plain(f"mlp2_{tag}", uu, w["w_mlp2"], epilogue=add_res, extras=(x1,), extra_kinds=("full",))
        if l + 1 < DEPTH:
            finish(l + 1, 0, x2)
        saved.append(dict(x0=xl, h1=h1, z5=z5, z2=z2, pa=pa, pb=pb, hl=hl, oa=oa, ob=ob, mg=mg, x1=x1, h2=h2,
                          pre=pre, uu=uu))
        xl = x2

    dx, dxb, d_final_g, loss_cols = loss_head("loss_head", xl, final_g[None], target)
    loss = lax.psum(jnp.sum(loss_cols), ("x", "y", "c"))

    small_grads = {}
    exchanges = {}

    def pair_start(l, gi, grads_g):
        axes = [big_axis[k] for k in groups[gi]]
        sems, grads_g, theirs, tok = pair_exchange_start(f"pair_start_l{l}g{gi}", grads_g, axes)
        exchanges[l, gi] = dict(axes=axes, pair_sems=sems, grads=grads_g, theirs=theirs)
        return tok

    def chip_start(l, gi, after):
        st = exchanges[l, gi]
        grads_g, theirs = pair_exchange_finish(f"pair_finish_l{l}g{gi}", st["grads"], st["theirs"], st["axes"],
                                               st["pair_sems"], after)
        sums = [chip_sum(f"chip_sum_{k}_l{l}", g, a, th, my_core)
                for k, g, a, th in zip(groups[gi], grads_g, st["axes"], theirs)]
        st["chip_sems"], st["sums"], st["recv"], tok = chip_exchange_start(f"chip_start_l{l}g{gi}", sums)
        return tok

    token = None
    for l in reversed(range(DEPTH)):
        p, w, sv = params[l], full[l], saved[l]
        tag = f"l{l}"
        dpre = mm_plain(f"mlp2_dx_{tag}", dxb, w["w_mlp2"], tb=True, out_dtype=BF16, epilogue=relu2_bwd,
                        extras=(sv["pre"],), extra_kinds=("full",), after=token)
        dw2 = mm_plain(f"mlp2_dw_{tag}", sv["uu"], dxb, ta=True, out_dtype=BF16)
        dw1 = mm_plain(f"mlp1_dw_{tag}", sv["h2"], dpre, ta=True, out_dtype=BF16)
        token = pair_start(l, 2, [dw1, dw2])
        dh2 = mm_plain(f"mlp1_dx_{tag}", dpre, w["w_mlp1"], tb=True, after=token)
        dx1, dx1b, dg2 = rmsnorm_bwd(f"norm2_bwd_{tag}", dh2, sv["x1"], p["g2"], dx)
        token = chip_start(l, 2, dx1b)
        dmg = mm_plain(f"proj_o_dx_{tag}", dx1b, w["w_o"], tb=True, after=token)
        dwo = mm_plain(f"proj_o_dw_{tag}", sv["mg"], dx1b, ta=True, out_dtype=BF16)
        doa, dob, dz2, dbias2 = merge_bwd(f"merge_bwd_{tag}", dmg, sv["z2"], sv["oa"], sv["ob"])
        dpa = mm_plain(f"proj_a_dx_{tag}", doa, w["w_pa"], tb=True)
        dpb = mm_plain(f"proj_b_dx_{tag}", dob, w["w_pb"], tb=True)
        dwpa = mm_plain(f"proj_a_dw_{tag}", sv["pa"], doa, ta=True, out_dtype=BF16)
        dwpb = mm_plain(f"proj_b_dw_{tag}", sv["pb"], dob, ta=True, out_dtype=BF16)
        token = pair_start(l, 1, [dwpa, dwpb, dwo])
        (dz5, dbias5, dcaw, dcab, dcbw, dwr, dwi, dbr, dbi, dlam) = branch_bwd(
            f"branch_bwd_{tag}", sv["z5"], sv["hl"], dpa, dpb, p["caw"], p["cab"], p["cbw"], w["lru_wr"], w["lru_wi"],
            p["br"], p["bi"], p["lam"])
        token = chip_start(l, 1, dz5)
        dwin = mm_in_proj_dw(f"in_proj5_dw_{tag}", sv["h1"], dz5, 0, n_in, after=token)
        dwin = mm_in_proj_dw(f"in_proj2_dw_{tag}", sv["h1"], dz2, N_BRANCH * d, n_in, prev=dwin)
        token = pair_start(l, 0, [dwin, dwr.astype(BF16), dwi.astype(BF16)])
        token = chip_start(l, 0, token)
        dh1 = mm_in_proj_dh(f"in_proj5_dx_{tag}", dz5, w["w_in"], 0, after=token)
        dh1 = mm_in_proj_dh(f"in_proj2_dx_{tag}", dz2, w["w_in"], N_BRANCH * d, res=dh1)
        dx, dxb, dg1 = rmsnorm_bwd(f"norm1_bwd_{tag}", dh1, sv["x0"], p["g1"], dx1)
        token = None
        small_grads[l] = dict(norm1_g=dg1, b_in=jnp.concatenate([dbias5.reshape(1, -1), dbias2.reshape(1, -1)], axis=1),
                              conv_a_b=dcab, lru_lam=dlam, norm2_g=dg2, conv_a_w=dcaw, conv_b_w=dcbw,
                              lru_br=dbr, lru_bi=dbi)
    grad_x = dx.reshape(x.shape)


    replicated = ["norm1_g", "b_in", "conv_a_b", "lru_lam", "norm2_g"]
    small_list = [jnp.stack([small_grads[l][k] for l in range(DEPTH)]) for k in replicated + small_sharded]
    small_list.append(d_final_g)
    packed_partial = _pack(small_list)
    partials = all_gather("gather_small_grads", [packed_partial[None]], [0])[0]
    summed = sum_partials("sum_small_grads", partials)
    pieces = _unpack(summed, [a.shape for a in small_list])
    small_sum = dict(zip(replicated + small_sharded + ["final_g"], pieces))

    grads = {}
    for k in replicated:
        grads[k] = small_sum[k].reshape(weights[k].shape)
    grads["final_g"] = small_sum["final_g"].reshape(weights["final_g"].shape)
    for k in small_sharded:
        shard_shape = weights[k].shape
        g = small_sum[k].reshape(shard_shape[0], shard_shape[1], N_DEV, shard_shape[2])
        grads[k] = lax.dynamic_index_in_dim(g, me, axis=2, keepdims=False)

    small_names = replicated + ["final_g"] + small_sharded
    small_shapes = [weights[k].shape for k in small_names]
    pk = lambda src: _pack([src[k] for k in small_names])
    sd, sm, sv2 = adamw_small("adamw_small", pk(grads), pk(weights), pk(mom1), pk(mom2))
    delta = dict(zip(small_names, _unpack(sd, small_shapes)))
    new_m = dict(zip(small_names, _unpack(sm, small_shapes)))
    new_v = dict(zip(small_names, _unpack(sv2, small_shapes)))

    after = dx
    for gi in reversed(range(len(groups))):
        for l in reversed(range(DEPTH)):
            st = exchanges[l, gi]
            st["own"], st["recv"] = chip_exchange_finish(f"chip_finish_l{l}g{gi}", st["sums"], st["recv"],
                                                         st["chip_sems"], after)
        for i, k in enumerate(groups[gi]):
            shape = weights[k].shape
            c = shape[-1]
            flat = lambda a: a.reshape(-1, c)
            per_chip = lambda key: [exchanges[l, gi][key][i].reshape(N_CHIP, -1, c) for l in range(DEPTH)]
            g, dl, m2, v2 = adamw_sharded(f"adamw_{k}", per_chip("recv"), per_chip("own"), my_chip,
                                          flat(weights[k]), flat(mom1[k]), flat(mom2[k]))
            grads[k], delta[k], new_m[k], new_v[k] = (a.reshape(shape) for a in (g, dl, m2, v2))
            after = g

    return (loss, grad_x, *[grads[k] for k in names], *[delta[k] for k in names],
            *[new_m[k] for k in names], *[new_v[k] for k in names])
```

```python
import functools
import math

import jax
import jax.numpy as jnp
from jax import lax
from jax.experimental import pallas as pl
from jax.experimental.pallas import tpu as pltpu

F32 = jnp.float32
BF16 = jnp.bfloat16

N_DEV = 8
DEPTH = 2
N_SPLIT = 7
N_BRANCH = 5
EPS = 1e-6
LRU_C = 8.0
CONV_A_WIDTH = 4
CONV_B_WIDTH = 3
GELU_C = math.sqrt(2.0 / math.pi)
GELU_A = 0.044715

ADAM_LR = 0.001
ADAM_B1 = 0.9
ADAM_B2 = 0.999
ADAM_EPS = 1e-08
ADAM_WD = 0.01
ADAM_STEP = 10

LANES = 128
SUBLANES = 8
VMEM_LIMIT_CAP = 56 * 2 ** 20
MESH = pl.DeviceIdType.MESH
ANY = pl.BlockSpec(memory_space=pl.ANY)


def _tile(n, target, align=LANES):
    if n <= target:
        return n
    t = (target // align) * align
    while t >= align:
        if n % t == 0:
            return t
        t -= align
    return n


def _nbytes(shape, dtype):
    return math.prod(shape) * jnp.dtype(dtype).itemsize


def _vmem_limit(block_bytes, scratch_bytes=0):
    est = 2 * block_bytes + scratch_bytes
    return int(min(max(2 * est, 32 * 2 ** 20), VMEM_LIMIT_CAP))


def _my_coords():
    return lax.axis_index("x"), lax.axis_index("y"), lax.axis_index("c")


def _flat(x, y, c):
    return 4 * x + 2 * y + c


def _peer(k, x, y, c):
    return (1 - x if k & 4 else x, 1 - y if k & 2 else y, 1 - c if k & 1 else c)


def _slab(ref, axis, start, size):
    idx = tuple(pl.ds(start, size) if d == axis else slice(None) for d in range(len(ref.shape)))
    return ref.at[idx]


def all_gather(name, shards, axes):
    n = len(shards)
    sizes = [s.shape[a] for s, a in zip(shards, axes)]
    out_shape = []
    for s, a in zip(shards, axes):
        full = list(s.shape)
        full[a] *= N_DEV
        out_shape.append(jax.ShapeDtypeStruct(tuple(full), s.dtype))

    def body(*refs):
        srcs, outs = refs[:n], refs[n:2 * n]
        send_sems, recv_sems, local_sems = refs[2 * n:]
        x, y, c = _my_coords()
        me = _flat(x, y, c)
        local = []
        for i in range(n):
            cp = pltpu.make_async_copy(srcs[i], _slab(outs[i], axes[i], me * sizes[i], sizes[i]), local_sems.at[i])
            cp.start()
            local.append(cp)
        remote = []
        for k in range(1, N_DEV):
            for i in range(n):
                mine = _slab(outs[i], axes[i], me * sizes[i], sizes[i])
                cp = pltpu.make_async_remote_copy(
                    src_ref=srcs[i], dst_ref=mine,
                    send_sem=send_sems.at[(k - 1) * n + i], recv_sem=recv_sems.at[(k - 1) * n + i],
                    device_id=_peer(k, x, y, c), device_id_type=MESH)
                cp.start()
                remote.append(cp)
        for cp in remote:
            cp.wait()
        for cp in local:
            cp.wait()

    return pl.pallas_call(
        body, name=name, out_shape=out_shape,
        in_specs=[ANY] * n, out_specs=[ANY] * n,
        scratch_shapes=[pltpu.SemaphoreType.DMA(((N_DEV - 1) * n,)),
                        pltpu.SemaphoreType.DMA(((N_DEV - 1) * n,)),
                        pltpu.SemaphoreType.DMA((n,))],
    )(*shards)


N_CHIP = 4
OTHER_CHIPS = ((1, 0), (0, 1), (1, 1))


HBM = pl.BlockSpec(memory_space=pltpu.HBM)
SEM = pl.BlockSpec(memory_space=pltpu.SEMAPHORE)
TOKEN = jax.ShapeDtypeStruct((SUBLANES, LANES), F32)


def _split_copy_params():
    return pltpu.CompilerParams(has_side_effects=pltpu.SideEffectType.DATAFLOW_SIDE_EFFECTING)


def _in_hbm(a):
    return pltpu.with_memory_space_constraint(a, pltpu.HBM)


def _other_chips(x, y):
    return [(x + dx - 2 * x * dx, y + dy - 2 * y * dy) for dx, dy in OTHER_CHIPS]


def place_shard(name, w, layer, axis, me):
    shard = w.shape[1:]
    if len(shard) == 2 and axis == 1:
        r, cs = shard
        tr = _tile(r, max(SUBLANES, (512 * 1024) // cs), 16)
        grid, full = (r // tr,), (r, N_DEV * cs)
        in_spec = pl.BlockSpec((None, tr, cs), lambda i, me_ref: (layer, i, 0))
        out_spec = pl.BlockSpec((tr, cs), lambda i, me_ref: (i, me_ref[0]))
    elif len(shard) == 2 and axis == 0:
        rs, cs = shard
        tr = _tile(rs, max(SUBLANES, (512 * 1024) // cs), 16)
        per = rs // tr
        grid, full = (per,), (N_DEV * rs, cs)
        in_spec = pl.BlockSpec((None, tr, cs), lambda i, me_ref: (layer, i, 0))
        out_spec = pl.BlockSpec((tr, cs), lambda i, me_ref: (me_ref[0] * per + i, 0))
    else:
        assert len(shard) == 3 and axis == 1
        lead, rs, cs = shard
        grid, full = (1,), (lead, N_DEV * rs, cs)
        in_spec = pl.BlockSpec((None, lead, rs, cs), lambda i, me_ref: (layer, 0, 0, 0))
        out_spec = pl.BlockSpec((lead, rs, cs), lambda i, me_ref: (0, me_ref[0], 0))

    def body(me_ref, w_ref, o_ref):
        o_ref[...] = w_ref[...].astype(BF16)

    return pl.pallas_call(
        body, name=name, out_shape=jax.ShapeDtypeStruct(full, BF16),
        grid_spec=pltpu.PrefetchScalarGridSpec(num_scalar_prefetch=1, grid=grid, in_specs=[in_spec],
                                               out_specs=out_spec),
        compiler_params=pltpu.CompilerParams(dimension_semantics=("parallel",)),
    )(me, w)


class _GatherCopies:
    def __init__(self, land, axes):
        self.land, self.axes, self.n = land, axes, len(land)
        self.sizes = [r.shape[a] // N_DEV for r, a in zip(land, axes)]
        x, y, c = _my_coords()
        self.c = c
        self.me, self.sibling = (x, y, c), (x, y, 1 - c)
        self.chips = _other_chips(x, y)

    def copy(self, sems, slot, i, block, to):
        rows = _slab(self.land[i], self.axes[i], _flat(*block) * self.sizes[i], self.sizes[i])
        return pltpu.make_async_remote_copy(
            src_ref=rows, dst_ref=rows, send_sem=sems[0].at[slot * self.n + i], recv_sem=sems[1].at[slot * self.n + i],
            device_id=to, device_id_type=MESH)

    def own(self, sems):
        out = []
        for i in range(self.n):
            out.append(self.copy(sems, 0, i, self.me, self.sibling))
            out += [self.copy(sems, 1 + j, i, self.me, (*chip, self.c)) for j, chip in enumerate(self.chips)]
        return out

    def landed_over_ici(self, sems):
        return [self.copy(sems, 1 + j, i, (*chip, self.c), self.me)
                for j, chip in enumerate(self.chips) for i in range(self.n)]

    def passed_on(self, sems):
        return [self.copy(sems, j, i, (*chip, self.c), self.sibling)
                for j, chip in enumerate(self.chips) for i in range(self.n)]

    def from_sibling(self, own_sems, pass_sems):
        out = [self.copy(own_sems, 0, i, self.sibling, self.me) for i in range(self.n)]
        out += [self.copy(pass_sems, j, i, (*chip, 1 - self.c), self.me)
                for j, chip in enumerate(self.chips) for i in range(self.n)]
        return out


def gather_start(name, land, axes, after=None):
    n = len(land)
    order = [] if after is None else [after]
    n_in = n + len(order)

    def body(*refs):
        own_sems, token = (refs[n_in], refs[n_in + 1]), refs[n_in + n + 2]
        for cp in _GatherCopies(refs[:n], axes).own(own_sems):
            cp.start()
        token[...] = jnp.zeros_like(token)

    out = pl.pallas_call(
        body, name=name,
        out_shape=(pltpu.SemaphoreType.DMA((4 * n,)), pltpu.SemaphoreType.DMA((4 * n,)),
                   *[pltpu.HBM(a.shape, a.dtype) for a in land], TOKEN),
        in_specs=[HBM] * n + [ANY] * len(order),
        out_specs=(SEM, SEM, *[HBM] * n, pl.BlockSpec(memory_space=pltpu.VMEM)),
        input_output_aliases={i: 2 + i for i in range(n)}, compiler_params=_split_copy_params(),
    )(*[_in_hbm(a) for a in land], *order)
    return (out[0], out[1]), list(out[2:2 + n]), out[-1]


def gather_pass_on(name, land, axes, own_sems, after):
    n = len(land)

    def body(*refs):
        sems_in, pass_sems = (refs[n], refs[n + 1]), (refs[n + 3], refs[n + 4])
        copies = _GatherCopies(refs[:n], axes)
        for arrived, onward in zip(copies.landed_over_ici(sems_in), copies.passed_on(pass_sems)):
            arrived.wait_recv()
            onward.start()

    out = pl.pallas_call(
        body, name=name,
        out_shape=(pltpu.SemaphoreType.DMA((3 * n,)), pltpu.SemaphoreType.DMA((3 * n,)),
                   *[pltpu.HBM(a.shape, a.dtype) for a in land]),
        in_specs=[HBM] * n + [SEM, SEM, ANY], out_specs=(SEM, SEM, *[HBM] * n),
        input_output_aliases={i: 2 + i for i in range(n)}, compiler_params=_split_copy_params(),
    )(*land, *own_sems, after)
    return (out[0], out[1]), list(out[2:])


def gather_finish(name, land, axes, own_sems, pass_sems, after):
    n = len(land)

    def body(*refs):
        own, passed = (refs[n], refs[n + 1]), (refs[n + 2], refs[n + 3])
        copies = _GatherCopies(refs[:n], axes)
        for cp in copies.from_sibling(own, passed):
            cp.wait_recv()
        for cp in copies.own(own) + copies.passed_on(passed):
            cp.wait_send()

    return list(pl.pallas_call(
        body, name=name, out_shape=tuple(pltpu.HBM(a.shape, a.dtype) for a in land),
        in_specs=[HBM] * n + [SEM] * 4 + [ANY], out_specs=tuple([HBM] * n),
        input_output_aliases={i: i for i in range(n)}, compiler_params=_split_copy_params(),
    )(*land, *own_sems, *pass_sems, after))


def _pair_copies(grads, axes, theirs, sems):
    x, y, c = _my_coords()
    out = []
    for i, (g, a) in enumerate(zip(grads, axes)):
        size = g.shape[a] // N_DEV
        for q in range(N_CHIP):
            k = i * N_CHIP + q
            out.append(pltpu.make_async_remote_copy(
                src_ref=_slab(g, a, (2 * q + 1 - c) * size, size), dst_ref=theirs[i].at[q],
                send_sem=sems[0].at[k], recv_sem=sems[1].at[k], device_id=(x, y, 1 - c), device_id_type=MESH))
    return out


def pair_exchange_start(name, grads, axes):
    n = len(grads)
    theirs = []
    for g, a in zip(grads, axes):
        shard = list(g.shape)
        shard[a] //= N_DEV
        theirs.append(lax.empty((N_CHIP, *shard), g.dtype))

    def body(*refs):
        sems, token = (refs[2 * n], refs[2 * n + 1]), refs[4 * n + 2]
        for cp in _pair_copies(refs[:n], axes, refs[n:2 * n], sems):
            cp.start()
        token[...] = jnp.zeros_like(token)

    both = [pltpu.HBM(a.shape, a.dtype) for a in list(grads) + theirs]
    out = pl.pallas_call(
        body, name=name,
        out_shape=(pltpu.SemaphoreType.DMA((N_CHIP * n,)), pltpu.SemaphoreType.DMA((N_CHIP * n,)), *both, TOKEN),
        in_specs=[HBM] * (2 * n), out_specs=(SEM, SEM, *[HBM] * (2 * n), pl.BlockSpec(memory_space=pltpu.VMEM)),
        input_output_aliases={i: 2 + i for i in range(2 * n)}, compiler_params=_split_copy_params(),
    )(*[_in_hbm(a) for a in grads], *[_in_hbm(a) for a in theirs])
    return (out[0], out[1]), list(out[2:2 + n]), list(out[2 + n:2 + 2 * n]), out[-1]


def pair_exchange_finish(name, grads, theirs, axes, sems, after):
    n = len(grads)

    def body(*refs):
        for cp in _pair_copies(refs[:n], axes, refs[n:2 * n], (refs[2 * n], refs[2 * n + 1])):
            cp.wait_send()
            cp.wait_recv()

    both = tuple(pltpu.HBM(a.shape, a.dtype) for a in list(grads) + list(theirs))
    out = pl.pallas_call(
        body, name=name, out_shape=both,
        in_specs=[HBM] * (2 * n) + [SEM, SEM, ANY], out_specs=tuple([HBM] * (2 * n)),
        input_output_aliases={i: i for i in range(2 * n)}, compiler_params=_split_copy_params(),
    )(*grads, *theirs, *sems, after)
    return list(out[:n]), list(out[n:])


def chip_sum(name, grad, axis, theirs, core):
    if grad.ndim == 2 and axis == 1:
        r, cs = grad.shape[0], grad.shape[1] // N_DEV
        tr = _tile(r, max(SUBLANES, (512 * 1024) // cs), 16)
        grid = (N_CHIP, r // tr)
        g_spec = pl.BlockSpec((tr, cs), lambda q, i, core_ref: (i, 2 * q + core_ref[0]))
        t_spec = pl.BlockSpec((None, tr, cs), lambda q, i, core_ref: (q, i, 0))
    elif grad.ndim == 2 and axis == 0:
        rs, cs = grad.shape[0] // N_DEV, grad.shape[1]
        tr = _tile(rs, max(SUBLANES, (512 * 1024) // cs), 16)
        per = rs // tr
        grid = (N_CHIP, per)
        g_spec = pl.BlockSpec((tr, cs), lambda q, i, core_ref: ((2 * q + core_ref[0]) * per + i, 0))
        t_spec = pl.BlockSpec((None, tr, cs), lambda q, i, core_ref: (q, i, 0))
    else:
        assert grad.ndim == 3 and axis == 1
        lead, rs, cs = grad.shape[0], grad.shape[1] // N_DEV, grad.shape[2]
        grid = (N_CHIP, 1)
        g_spec = pl.BlockSpec((lead, rs, cs), lambda q, i, core_ref: (0, 2 * q + core_ref[0], 0))
        t_spec = pl.BlockSpec((None, lead, rs, cs), lambda q, i, core_ref: (q, 0, 0, 0))

    def body(core_ref, g_ref, t_ref, o_ref):
        o_ref[...] = (g_ref[...].astype(F32) + t_ref[...].astype(F32)).astype(o_ref.dtype)

    return pl.pallas_call(
        body, name=name, out_shape=jax.ShapeDtypeStruct(theirs.shape, theirs.dtype),
        grid_spec=pltpu.PrefetchScalarGridSpec(num_scalar_prefetch=1, grid=grid, in_specs=[g_spec, t_spec],
                                               out_specs=t_spec),
        compiler_params=pltpu.CompilerParams(dimension_semantics=("parallel", "parallel")),
    )(core, grad, theirs)


def _chip_copies(sums, recv, sems):
    n = len(sums)
    x, y, c = _my_coords()
    out = []
    for j, (tx, ty) in enumerate(_other_chips(x, y)):
        for i in range(n):
            out.append(pltpu.make_async_remote_copy(
                src_ref=sums[i].at[2 * tx + ty], dst_ref=recv[i].at[2 * x + y],
                send_sem=sems[0].at[j * n + i], recv_sem=sems[1].at[j * n + i],
                device_id=(tx, ty, c), device_id_type=MESH))
    return out


def chip_exchange_start(name, sums):
    n = len(sums)

    def body(*refs):
        sems, token = (refs[2 * n], refs[2 * n + 1]), refs[4 * n + 2]
        for cp in _chip_copies(refs[:n], refs[n:2 * n], sems):
            cp.start()
        token[...] = jnp.zeros_like(token)

    both = [pltpu.HBM(a.shape, a.dtype) for a in sums]
    out = pl.pallas_call(
        body, name=name,
        out_shape=(pltpu.SemaphoreType.DMA((3 * n,)), pltpu.SemaphoreType.DMA((3 * n,)), *both, *both, TOKEN),
        in_specs=[HBM] * (2 * n), out_specs=(SEM, SEM, *[HBM] * (2 * n), pl.BlockSpec(memory_space=pltpu.VMEM)),
        input_output_aliases={i: 2 + i for i in range(2 * n)}, compiler_params=_split_copy_params(),
    )(*[_in_hbm(a) for a in sums], *[_in_hbm(lax.empty(a.shape, a.dtype)) for a in sums])
    return (out[0], out[1]), list(out[2:2 + n]), list(out[2 + n:2 + 2 * n]), out[-1]


def chip_exchange_finish(name, sums, recv, sems, after):
    n = len(sums)

    def body(*refs):
        for cp in _chip_copies(refs[:n], refs[n:2 * n], (refs[2 * n], refs[2 * n + 1])):
            cp.wait_send()
            cp.wait_recv()

    both = tuple(pltpu.HBM(a.shape, a.dtype) for a in sums)
    out = pl.pallas_call(
        body, name=name, out_shape=both + both,
        in_specs=[HBM] * (2 * n) + [SEM, SEM, ANY], out_specs=tuple([HBM] * (2 * n)),
        input_output_aliases={i: i for i in range(2 * n)}, compiler_params=_split_copy_params(),
    )(*sums, *recv, *sems, after)
    return list(out[:n]), list(out[n:])


def _mm(name, a, b, *, ta, tb, grid, tm, tn, tk, a_spec, b_spec, outs, extras=(), epilogue=None, alias=None,
        after=None):
    n_extra, n_out = len(extras), len(outs)
    nk = grid[2]
    dn = (((0,) if ta else (1,), (1,) if tb else (0,)), ((), ()))

    def finish(acc, extra_refs, out_refs):
        vals = epilogue(acc, *[r[...] for r in extra_refs]) if epilogue is not None else (acc,)
        for o_ref, v in zip(out_refs, vals):
            o_ref[...] = v.astype(o_ref.dtype)

    def body(*refs):
        a_ref, b_ref = refs[0], refs[1]
        extra_refs = refs[2:2 + n_extra]
        first_out = 2 + n_extra + (1 if alias is not None else 0) + (1 if after is not None else 0)
        out_refs = refs[first_out:first_out + n_out]
        part = lax.dot_general(a_ref[...], b_ref[...], dn, preferred_element_type=F32)
        if nk == 1:
            finish(part, extra_refs, out_refs)
            return
        acc_ref = refs[-1]
        k = pl.program_id(2)

        @pl.when(k == 0)
        def _():
            acc_ref[...] = part

        @pl.when(k > 0)
        def _():
            acc_ref[...] += part

        @pl.when(k == nk - 1)
        def _():
            finish(acc_ref[...], extra_refs, out_refs)

    operands = [a, b] + [e for e, _ in extras]
    in_specs = [a_spec, b_spec] + [s for _, s in extras]
    aliases = {}
    if alias is not None:
        aliases = {len(operands): 0}
        operands.append(alias)
        in_specs.append(ANY)
    if after is not None:
        operands.append(after)
        in_specs.append(ANY)
    blocks = _nbytes((tm, tk), a.dtype) + _nbytes((tk, tn), b.dtype)
    blocks += sum(_nbytes((tm, tn), s.dtype) for s, _ in outs)
    blocks += sum(_nbytes((e.shape[-2] if e.shape[-2] < tm else tm, tn), e.dtype) for e, _ in extras)
    scratch = [] if nk == 1 else [pltpu.VMEM((tm, tn), F32)]
    return pl.pallas_call(
        body, name=name, grid=grid,
        out_shape=[s for s, _ in outs], in_specs=in_specs, out_specs=[s for _, s in outs],
        input_output_aliases=aliases, scratch_shapes=scratch,
        compiler_params=pltpu.CompilerParams(
            dimension_semantics=("parallel", "parallel", "arbitrary"),
            vmem_limit_bytes=_vmem_limit(blocks, 0 if nk == 1 else 3 * tm * tn * 4)),
    )(*operands)


def _mm_tiles(m, n, k):
    return _tile(m, 1024), _tile(n, 1024), _tile(k, 2048)


def mm_plain(name, a, b, *, ta=False, tb=False, out_dtype=F32, epilogue=None, extras=(), extra_kinds=(),
             out_dtypes=None, after=None):
    m = a.shape[1] if ta else a.shape[0]
    k = a.shape[0] if ta else a.shape[1]
    n = b.shape[0] if tb else b.shape[1]
    tm, tn, tk = _mm_tiles(m, n, k)
    grid = (m // tm, n // tn, k // tk)
    a_spec = pl.BlockSpec((tk, tm), lambda i, j, kk: (kk, i)) if ta else pl.BlockSpec((tm, tk), lambda i, j, kk: (i, kk))
    b_spec = pl.BlockSpec((tn, tk), lambda i, j, kk: (j, kk)) if tb else pl.BlockSpec((tk, tn), lambda i, j, kk: (kk, j))
    ex = []
    for e, kind in zip(extras, extra_kinds):
        if kind == "row":
            ex.append((e, pl.BlockSpec((1, tn), lambda i, j, kk: (0, j))))
        else:
            ex.append((e, pl.BlockSpec((tm, tn), lambda i, j, kk: (i, j))))
    out_dtypes = out_dtypes or [out_dtype]
    outs = [(jax.ShapeDtypeStruct((m, n), dt), pl.BlockSpec((tm, tn), lambda i, j, kk: (i, j))) for dt in out_dtypes]
    res = _mm(name, a, b, ta=ta, tb=tb, grid=grid, tm=tm, tn=tn, tk=tk, a_spec=a_spec, b_spec=b_spec,
              outs=outs, extras=ex, epilogue=epilogue, after=after)
    return res if len(res) > 1 else res[0]


def mm_in_proj(name, h, w, bias, col0, parts):
    t, d = h.shape
    tm, tn, tk = _mm_tiles(t, d, d)
    per = d // tn
    off = col0 // tn
    grid = (t // tm, parts * per, d // tk)
    outs = [(jax.ShapeDtypeStruct((parts, t, d), F32),
             pl.BlockSpec((None, tm, tn), lambda i, j, kk: (j // per, i, j % per)))]
    ex = [(bias, pl.BlockSpec((1, tn), lambda i, j, kk: (0, j + off)))]
    return _mm(name, h, w, ta=False, tb=False, grid=grid, tm=tm, tn=tn, tk=tk,
               a_spec=pl.BlockSpec((tm, tk), lambda i, j, kk: (i, kk)),
               b_spec=pl.BlockSpec((tk, tn), lambda i, j, kk: (kk, j + off)),
               outs=outs, extras=ex, epilogue=lambda acc, bb: (acc + bb,))[0]


def mm_in_proj_dw(name, h, dz, col0, n_total, prev=None, after=None):
    parts, t, d = dz.shape
    tm, tn, tk = _mm_tiles(d, d, t)
    per = d // tn
    off = col0 // tn
    grid = (d // tm, parts * per, t // tk)
    outs = [(jax.ShapeDtypeStruct((d, n_total), BF16), pl.BlockSpec((tm, tn), lambda i, j, kk: (i, j + off)))]
    return _mm(name, h, dz, ta=True, tb=False, grid=grid, tm=tm, tn=tn, tk=tk,
               a_spec=pl.BlockSpec((tk, tm), lambda i, j, kk: (kk, i)),
               b_spec=pl.BlockSpec((None, tk, tn), lambda i, j, kk: (j // per, kk, j % per)),
               outs=outs, alias=prev, after=after)[0]


def mm_in_proj_dh(name, dz, w, col0, res=None, after=None):
    parts, t, d = dz.shape
    tm, tn, tk = _mm_tiles(t, d, d)
    per = d // tk
    off = col0 // tk
    grid = (t // tm, d // tn, parts * per)
    outs = [(jax.ShapeDtypeStruct((t, d), F32), pl.BlockSpec((tm, tn), lambda i, j, kk: (i, j)))]
    ex, epi = [], None
    if res is not None:
        ex = [(res, pl.BlockSpec((tm, tn), lambda i, j, kk: (i, j)))]
        epi = lambda acc, r: (acc + r,)
    return _mm(name, dz, w, ta=False, tb=True, grid=grid, tm=tm, tn=tn, tk=tk,
               a_spec=pl.BlockSpec((None, tm, tk), lambda i, j, kk: (kk // per, i, kk % per)),
               b_spec=pl.BlockSpec((tn, tk), lambda i, j, kk: (j, kk + off)),
               outs=outs, extras=ex, epilogue=epi, after=after)[0]


ROW_TILE = 256


def rmsnorm_fwd(name, x, g, after=None):
    t, d = x.shape
    tr = _tile(t, ROW_TILE, SUBLANES)
    order = [] if after is None else [after]

    def body(x_ref, g_ref, *rest):
        h_ref = rest[-1]
        xv = x_ref[...]
        rstd = lax.rsqrt(jnp.mean(xv * xv, axis=-1, keepdims=True) + EPS)
        h_ref[...] = (xv * rstd * g_ref[...]).astype(BF16)

    return pl.pallas_call(
        body, name=name, grid=(t // tr,),
        out_shape=jax.ShapeDtypeStruct((t, d), BF16),
        in_specs=[pl.BlockSpec((tr, d), lambda i: (i, 0)), pl.BlockSpec((1, d), lambda i: (0, 0))] + [ANY] * len(order),
        out_specs=pl.BlockSpec((tr, d), lambda i: (i, 0)),
        compiler_params=pltpu.CompilerParams(dimension_semantics=("parallel",)),
    )(x, g, *order)


def _rmsnorm_bwd_math(xv, gv, dout):
    rstd = lax.rsqrt(jnp.mean(xv * xv, axis=-1, keepdims=True) + EPS)
    xhat = xv * rstd
    dy = dout * gv
    dx = rstd * (dy - xhat * jnp.mean(dy * xhat, axis=-1, keepdims=True))
    dg = jnp.sum(dout * xhat, axis=0, keepdims=True)
    return dx, dg


def rmsnorm_bwd(name, dh, x, g, dres):
    t, d = x.shape
    tr = _tile(t, ROW_TILE, SUBLANES)

    def body(dh_ref, x_ref, g_ref, dres_ref, dx_ref, dxb_ref, dg_ref):
        dx, dg = _rmsnorm_bwd_math(x_ref[...], g_ref[...], dh_ref[...])
        dx = dx + dres_ref[...]
        dx_ref[...] = dx
        dxb_ref[...] = dx.astype(BF16)

        @pl.when(pl.program_id(0) == 0)
        def _():
            dg_ref[...] = dg

        @pl.when(pl.program_id(0) > 0)
        def _():
            dg_ref[...] += dg

    row = pl.BlockSpec((tr, d), lambda i: (i, 0))
    vec = pl.BlockSpec((1, d), lambda i: (0, 0))
    return pl.pallas_call(
        body, name=name, grid=(t // tr,),
        out_shape=[jax.ShapeDtypeStruct((t, d), F32), jax.ShapeDtypeStruct((t, d), BF16),
                   jax.ShapeDtypeStruct((1, d), F32)],
        in_specs=[row, row, vec, row], out_specs=[row, row, vec],
        compiler_params=pltpu.CompilerParams(dimension_semantics=("arbitrary",)),
    )(dh, x, g, dres)


def loss_head(name, x, g, target):
    t, d = x.shape
    tr = _tile(t, ROW_TILE, SUBLANES)

    def body(x_ref, g_ref, tgt_ref, dx_ref, dxb_ref, dg_ref, loss_ref):
        xv, gv = x_ref[...], g_ref[...]
        rstd = lax.rsqrt(jnp.mean(xv * xv, axis=-1, keepdims=True) + EPS)
        err = xv * rstd * gv - tgt_ref[...]
        lsum = (0.5 / d) * jnp.sum(err * err, axis=0, keepdims=True)
        dx, dg = _rmsnorm_bwd_math(xv, gv, err * (1.0 / d))
        dx_ref[...] = dx
        dxb_ref[...] = dx.astype(BF16)

        @pl.when(pl.program_id(0) == 0)
        def _():
            dg_ref[...] = dg
            loss_ref[...] = lsum

        @pl.when(pl.program_id(0) > 0)
        def _():
            dg_ref[...] += dg
            loss_ref[...] += lsum

    row = pl.BlockSpec((tr, d), lambda i: (i, 0))
    vec = pl.BlockSpec((1, d), lambda i: (0, 0))
    return pl.pallas_call(
        body, name=name, grid=(t // tr,),
        out_shape=[jax.ShapeDtypeStruct((t, d), F32), jax.ShapeDtypeStruct((t, d), BF16),
                   jax.ShapeDtypeStruct((1, d), F32), jax.ShapeDtypeStruct((1, d), F32)],
        in_specs=[row, vec, row], out_specs=[row, row, vec, vec],
        compiler_params=pltpu.CompilerParams(dimension_semantics=("arbitrary",)),
    )(x, g, target)


TIME_TILE = 256


def _sigmoid(v):
    return 1.0 / (1.0 + jnp.exp(-v))


def _gelu(v):
    return 0.5 * v * (1.0 + jnp.tanh(GELU_C * (v + GELU_A * v * v * v)))


def _gelu_grad(v):
    th = jnp.tanh(GELU_C * (v + GELU_A * v * v * v))
    return 0.5 * (1.0 + th) + 0.5 * v * (1.0 - th * th) * GELU_C * (1.0 + 3.0 * GELU_A * v * v)


def _one_minus_exp(v):
    series = -v * (1.0 + v * (0.5 + v * (1.0 / 6.0 + v * (1.0 / 24.0))))
    return jnp.where(v > -0.05, series, 1.0 - jnp.exp(v))


def _roll_in_groups(v, s):
    tt, cb = v.shape
    return pltpu.roll(v.reshape(tt // SUBLANES, SUBLANES, cb), s, 1).reshape(tt, cb)


def _back(cur, halo, s, row):
    tt, cb = cur.shape
    turned = _roll_in_groups(cur, s)
    group_before = jnp.concatenate([pltpu.roll(halo, s, 0), turned[:tt - SUBLANES]], axis=0)
    return jnp.where((row & (SUBLANES - 1)) < s, group_before, turned)


def _ahead(cur, halo, s, row):
    tt, cb = cur.shape
    turned = _roll_in_groups(cur, SUBLANES - s)
    group_after = jnp.concatenate([turned[SUBLANES:], pltpu.roll(halo, SUBLANES - s, 0)], axis=0)
    return jnp.where((row & (SUBLANES - 1)) >= SUBLANES - s, group_after, turned)


def _lru_gates(u, wr, wi, br, bi, lam):
    ub = u.astype(BF16)
    r = _sigmoid(jnp.dot(ub, wr, preferred_element_type=F32) + br)
    i = _sigmoid(jnp.dot(ub, wi, preferred_element_type=F32) + bi)
    sp = jnp.maximum(-lam, 0.0) + jnp.log1p(jnp.exp(-jnp.abs(lam)))
    log_a = -LRU_C * r * sp
    a = jnp.exp(log_a)
    mult = jnp.sqrt(_one_minus_exp(2.0 * log_a))
    return ub, r, i, a, mult


def _branch_specs(tt, cb, nt, time_of):
    per = tt // SUBLANES

    def tile(parts):
        if parts is None:
            return pl.BlockSpec((tt, cb), lambda n, s: (time_of(s), n))
        return pl.BlockSpec((parts, tt, cb), lambda n, s: (0, time_of(s), n))

    def halo_before(parts):
        if parts is None:
            return pl.BlockSpec((SUBLANES, cb), lambda n, s: (jnp.maximum(time_of(s) * per - 1, 0), n))
        return pl.BlockSpec((parts, SUBLANES, cb), lambda n, s: (0, jnp.maximum(time_of(s) * per - 1, 0), n))

    def rows(k):
        return pl.BlockSpec((k, cb), lambda n, s: (0, n))

    gate_w = pl.BlockSpec((None, cb, cb), lambda n, s: (n, 0, 0))
    return tile, halo_before, rows, gate_w


def branch_fwd(name, z5, caw, cab, cbw, wr, wi, br, bi, lam):
    _, t, d = z5.shape
    cb = wr.shape[-1]
    tt = _tile(t, TIME_TILE, SUBLANES)
    nt = t // tt
    tile, halo_before, rows, gate_w = _branch_specs(tt, cb, nt, lambda s: s)

    def body(z_ref, zh_ref, caw_ref, cab_ref, cbw_ref, wr_ref, wi_ref, br_ref, bi_ref, lam_ref,
             pa_ref, pb_ref, h_ref, carry_ref, acum_ref, bcum_ref):
        s = pl.program_id(1)
        row = lax.broadcasted_iota(jnp.int32, (tt, cb), 0)
        keep = jnp.where(s > 0, 1.0, 0.0)

        @pl.when(s == 0)
        def _():
            carry_ref[...] = jnp.zeros_like(carry_ref)

        xa, ya, cbv, ccv, cxv = (z_ref[p] for p in range(N_BRANCH))
        xa_h = zh_ref[0] * keep
        caw_v = caw_ref[...]
        u = caw_v[3:4] * xa + cab_ref[...]
        for j in range(1, CONV_A_WIDTH):
            u = u + caw_v[3 - j:4 - j] * _back(xa, xa_h, j, row)
        _, _, gi, a, mult = _lru_gates(u, wr_ref[...], wi_ref[...], br_ref[...], bi_ref[...], lam_ref[...])
        b = mult * (gi * u)

        rm = row & (SUBLANES - 1)
        for sh in (1, 2, 4):
            a_prev = jnp.where(rm >= sh, _roll_in_groups(a, sh), 1.0)
            b_prev = jnp.where(rm >= sh, _roll_in_groups(b, sh), 0.0)
            b = a * b_prev + b
            a = a * a_prev
        acum_ref[...] = a
        bcum_ref[...] = b

        def group(g, h_in):
            r0 = pl.multiple_of(g * SUBLANES, SUBLANES)
            hg = acum_ref[pl.ds(r0, SUBLANES), :] * h_in + bcum_ref[pl.ds(r0, SUBLANES), :]
            h_ref[pl.ds(r0, SUBLANES), :] = hg
            return jnp.broadcast_to(hg[SUBLANES - 1:SUBLANES, :], (SUBLANES, cb))

        carry_ref[...] = lax.fori_loop(0, tt // SUBLANES, group, carry_ref[...], unroll=4)
        pa_ref[...] = (h_ref[...] * _gelu(ya)).astype(BF16)

        q = ccv * cxv
        q_h = zh_ref[3] * zh_ref[4] * keep
        cbw_v = cbw_ref[...]
        v = cbw_v[2:3] * q
        for j in range(1, CONV_B_WIDTH):
            v = v + cbw_v[2 - j:3 - j] * _back(q, q_h, j, row)
        pb_ref[...] = (cbv * v).astype(BF16)

    act = jax.ShapeDtypeStruct((t, d), BF16)
    blocks = _nbytes((2 * N_BRANCH, tt, cb), F32) + _nbytes((2, cb, cb), BF16) + _nbytes((4, tt, cb), F32)
    return pl.pallas_call(
        body, name=name, grid=(d // cb, nt),
        out_shape=[act, act, jax.ShapeDtypeStruct((t, d), F32)],
        in_specs=[tile(N_BRANCH), halo_before(N_BRANCH), rows(CONV_A_WIDTH), rows(1), rows(CONV_B_WIDTH),
                  gate_w, gate_w, rows(1), rows(1), rows(1)],
        out_specs=[tile(None), tile(None), tile(None)],
        scratch_shapes=[pltpu.VMEM((SUBLANES, cb), F32), pltpu.VMEM((tt, cb), F32), pltpu.VMEM((tt, cb), F32)],
        compiler_params=pltpu.CompilerParams(dimension_semantics=("parallel", "arbitrary"),
                                             vmem_limit_bytes=_vmem_limit(blocks, 40 * tt * cb * 4)),
    )(z5, z5, caw, cab, cbw, wr, wi, br, bi, lam)


def branch_bwd(name, z5, hl, dpa, dpb, caw, cab, cbw, wr, wi, br, bi, lam):
    _, t, d = z5.shape
    cb = wr.shape[-1]
    tt = _tile(t, TIME_TILE, SUBLANES)
    nt = t // tt
    tile, halo_before, rows, gate_w = _branch_specs(tt, cb, nt, lambda s: nt - 1 - s)

    def body(z_ref, zh_ref, h_ref, hh_ref, dpa_ref, dpb_ref, caw_ref, cab_ref, cbw_ref, wr_ref, wi_ref,
             br_ref, bi_ref, lam_ref,
             dz_ref, dbias_ref, dcaw_ref, dcab_ref, dcbw_ref, dwr_ref, dwi_ref, dbr_ref, dbi_ref, dlam_ref,
             g_carry_ref, a_next_ref, du_next_ref, dv_next_ref, acum_ref, bcum_ref, gout_ref):
        s = pl.program_id(1)
        row = lax.broadcasted_iota(jnp.int32, (tt, cb), 0)
        keep = jnp.where(s < nt - 1, 1.0, 0.0)

        @pl.when(s == 0)
        def _():
            for r in (g_carry_ref, a_next_ref, du_next_ref, dv_next_ref, dbias_ref, dcaw_ref, dcab_ref, dcbw_ref,
                      dwr_ref, dwi_ref, dbr_ref, dbi_ref, dlam_ref):
                r[...] = jnp.zeros_like(r)

        xa, ya, cbv, ccv, cxv = (z_ref[p] for p in range(N_BRANCH))
        xa_h = zh_ref[0] * keep
        caw_v = caw_ref[...]
        wr_v, wi_v, lam_v = wr_ref[...], wi_ref[...], lam_ref[...]
        xa_back = [xa] + [_back(xa, xa_h, j, row) for j in range(1, CONV_A_WIDTH)]
        u = caw_v[3:4] * xa + cab_ref[...]
        for j in range(1, CONV_A_WIDTH):
            u = u + caw_v[3 - j:4 - j] * xa_back[j]
        ub, gr, gi, a, mult = _lru_gates(u, wr_v, wi_v, br_ref[...], bi_ref[...], lam_v)

        hv = h_ref[...]
        dpa_v = dpa_ref[...]
        dya = dpa_v * hv * _gelu_grad(ya)
        dh = dpa_v * _gelu(ya)

        a_up = _ahead(a, a_next_ref[...], 1, row)
        rm = row & (SUBLANES - 1)
        ca, cg = a_up, dh
        for sh in (1, 2, 4):
            a_nxt = jnp.where(rm + sh < SUBLANES, _roll_in_groups(ca, SUBLANES - sh), 1.0)
            g_nxt = jnp.where(rm + sh < SUBLANES, _roll_in_groups(cg, SUBLANES - sh), 0.0)
            cg = ca * g_nxt + cg
            ca = ca * a_nxt
        acum_ref[...] = ca
        bcum_ref[...] = cg
        n_groups = tt // SUBLANES

        def group(k, g_in):
            r0 = pl.multiple_of((n_groups - 1 - k) * SUBLANES, SUBLANES)
            gg = acum_ref[pl.ds(r0, SUBLANES), :] * g_in + bcum_ref[pl.ds(r0, SUBLANES), :]
            gout_ref[pl.ds(r0, SUBLANES), :] = gg
            return jnp.broadcast_to(gg[0:1, :], (SUBLANES, cb))

        g_carry_ref[...] = lax.fori_loop(0, n_groups, group, g_carry_ref[...], unroll=4)
        a_next_ref[...] = a[0:SUBLANES, :]
        gv = gout_ref[...]

        h_prev = _back(hv, hh_ref[...] * keep, 1, row)
        da = gv * h_prev
        iu = gi * u
        dmult = gv * iu
        di = gv * mult * u
        du = gv * mult * gi
        dla = da * a - dmult * (a * a) / mult
        sp = jnp.maximum(-lam_v, 0.0) + jnp.log1p(jnp.exp(-jnp.abs(lam_v)))
        dr = dla * (-LRU_C * sp)
        dlam_ref[...] += jnp.sum(dla * gr, axis=0, keepdims=True) * (LRU_C * _sigmoid(-lam_v))
        drp = dr * gr * (1.0 - gr)
        dip = di * gi * (1.0 - gi)
        dbr_ref[...] += jnp.sum(drp, axis=0, keepdims=True)
        dbi_ref[...] += jnp.sum(dip, axis=0, keepdims=True)
        drp_b, dip_b = drp.astype(BF16), dip.astype(BF16)
        tn_dims = (((0,), (0,)), ((), ()))
        nt_dims = (((1,), (1,)), ((), ()))
        dwr_ref[...] += lax.dot_general(ub, drp_b, tn_dims, preferred_element_type=F32)
        dwi_ref[...] += lax.dot_general(ub, dip_b, tn_dims, preferred_element_type=F32)
        du = du + lax.dot_general(drp_b, wr_v, nt_dims, preferred_element_type=F32)
        du = du + lax.dot_general(dip_b, wi_v, nt_dims, preferred_element_type=F32)

        du_h = du_next_ref[...]
        dxa = caw_v[3:4] * du
        dcaw_ref[3:4, :] += jnp.sum(du * xa, axis=0, keepdims=True)
        for j in range(1, CONV_A_WIDTH):
            dxa = dxa + caw_v[3 - j:4 - j] * _ahead(du, du_h, j, row)
            dcaw_ref[3 - j:4 - j, :] += jnp.sum(du * xa_back[j], axis=0, keepdims=True)
        dcab_ref[...] += jnp.sum(du, axis=0, keepdims=True)
        du_next_ref[...] = du[0:SUBLANES, :]

        q = ccv * cxv
        q_h = zh_ref[3] * zh_ref[4] * keep
        cbw_v = cbw_ref[...]
        q_back = [q] + [_back(q, q_h, j, row) for j in range(1, CONV_B_WIDTH)]
        v = cbw_v[2:3] * q
        for j in range(1, CONV_B_WIDTH):
            v = v + cbw_v[2 - j:3 - j] * q_back[j]
        dpb_v = dpb_ref[...]
        dcb = dpb_v * v
        dv = dpb_v * cbv
        dv_h = dv_next_ref[...]
        dq = cbw_v[2:3] * dv
        dcbw_ref[2:3, :] += jnp.sum(dv * q, axis=0, keepdims=True)
        for j in range(1, CONV_B_WIDTH):
            dq = dq + cbw_v[2 - j:3 - j] * _ahead(dv, dv_h, j, row)
            dcbw_ref[2 - j:3 - j, :] += jnp.sum(dv * q_back[j], axis=0, keepdims=True)
        dv_next_ref[...] = dv[0:SUBLANES, :]
        dcc = dq * cxv
        dcx = dq * ccv

        for p, val in enumerate((dxa, dya, dcb, dcc, dcx)):
            dz_ref[p] = val.astype(BF16)
            dbias_ref[p:p + 1, :] += jnp.sum(val, axis=0, keepdims=True)

    def acc_rows(k):
        return jax.ShapeDtypeStruct((k, d), F32)

    nb = d // cb
    gate_grad = jax.ShapeDtypeStruct((nb, cb, cb), F32)
    small = pltpu.VMEM((SUBLANES, cb), F32)
    big = pltpu.VMEM((tt, cb), F32)
    blocks = _nbytes((2 * (N_BRANCH + 3), tt, cb), F32) + _nbytes((4, cb, cb), F32)
    return pl.pallas_call(
        body, name=name, grid=(nb, nt),
        out_shape=[jax.ShapeDtypeStruct((N_BRANCH, t, d), BF16), acc_rows(N_BRANCH), acc_rows(CONV_A_WIDTH),
                   acc_rows(1), acc_rows(CONV_B_WIDTH), gate_grad, gate_grad, acc_rows(1), acc_rows(1), acc_rows(1)],
        in_specs=[tile(N_BRANCH), halo_before(N_BRANCH), tile(None), halo_before(None), tile(None), tile(None),
                  rows(CONV_A_WIDTH), rows(1), rows(CONV_B_WIDTH), gate_w, gate_w, rows(1), rows(1), rows(1)],
        out_specs=[tile(N_BRANCH), rows(N_BRANCH), rows(CONV_A_WIDTH), rows(1), rows(CONV_B_WIDTH),
                   gate_w, gate_w, rows(1), rows(1), rows(1)],
        scratch_shapes=[small, small, small, small, big, big, big],
        compiler_params=pltpu.CompilerParams(dimension_semantics=("parallel", "arbitrary"),
                                             vmem_limit_bytes=_vmem_limit(blocks, 60 * tt * cb * 4)),
    )(z5, z5, hl, hl, dpa, dpb, caw, cab, cbw, wr, wi, br, bi, lam)


MERGE_MM_ROWS, MERGE_MM_COLS = 1024, 512


def proj_ab_merge(name, pa, pb, w_pa, w_pb, z2):
    t, k = pa.shape
    n = w_pa.shape[1]
    tm, tn = _tile(t, MERGE_MM_ROWS, SUBLANES), _tile(n, MERGE_MM_COLS)

    def body(pa_ref, pb_ref, wa_ref, wb_ref, z_ref, oa_ref, ob_ref, mg_ref):
        oa = jnp.dot(pa_ref[...], wa_ref[...], preferred_element_type=F32)
        ob = jnp.dot(pb_ref[...], wb_ref[...], preferred_element_type=F32)
        oa_ref[...] = oa
        ob_ref[...] = ob
        mg_ref[...] = (_sigmoid(z_ref[0]) * oa + _sigmoid(z_ref[1]) * ob).astype(BF16)

    act = pl.BlockSpec((tm, k), lambda i, j: (i, 0))
    wgt = pl.BlockSpec((k, tn), lambda i, j: (0, j))
    blk = pl.BlockSpec((tm, tn), lambda i, j: (i, j))
    out = jax.ShapeDtypeStruct((t, n), F32)
    blocks = 2 * _nbytes((tm, k), BF16) + 2 * _nbytes((k, tn), BF16) + 4 * _nbytes((tm, tn), F32) + _nbytes((tm, tn), BF16)
    return pl.pallas_call(
        body, name=name, grid=(t // tm, n // tn),
        out_shape=[out, out, jax.ShapeDtypeStruct((t, n), BF16)],
        in_specs=[act, act, wgt, wgt, pl.BlockSpec((2, tm, tn), lambda i, j: (0, i, j))],
        out_specs=[blk, blk, blk],
        compiler_params=pltpu.CompilerParams(dimension_semantics=("parallel", "parallel"),
                                             vmem_limit_bytes=_vmem_limit(blocks)),
    )(pa, pb, w_pa, w_pb, z2)


def proj_o_dx_merge_bwd(name, dx, w_o, z2, oa, ob):
    t, k = dx.shape
    n = w_o.shape[0]
    tm, tn = _tile(t, MERGE_MM_ROWS, SUBLANES), _tile(n, MERGE_MM_COLS)

    def body(dx_ref, w_ref, z_ref, oa_ref, ob_ref, doa_ref, dob_ref, dz_ref, sums_ref):
        dm = lax.dot_general(dx_ref[...], w_ref[...], (((1,), (1,)), ((), ())), preferred_element_type=F32)
        sa, sb = _sigmoid(z_ref[0]), _sigmoid(z_ref[1])
        doa_ref[...] = (dm * sa).astype(BF16)
        dob_ref[...] = (dm * sb).astype(BF16)
        dga = dm * oa_ref[...] * sa * (1.0 - sa)
        dgb = dm * ob_ref[...] * sb * (1.0 - sb)
        dz_ref[0] = dga.astype(BF16)
        dz_ref[1] = dgb.astype(BF16)
        sums_ref[0:1, :] = jnp.sum(dga, axis=0, keepdims=True)
        sums_ref[1:2, :] = jnp.sum(dgb, axis=0, keepdims=True)

    blk = pl.BlockSpec((tm, tn), lambda i, j: (i, j))
    pair = pl.BlockSpec((2, tm, tn), lambda i, j: (0, i, j))
    act = jax.ShapeDtypeStruct((t, n), BF16)
    blocks = _nbytes((tm, k), BF16) + _nbytes((tn, k), BF16) + 4 * _nbytes((tm, tn), F32) + 4 * _nbytes((tm, tn), BF16)
    return pl.pallas_call(
        body, name=name, grid=(t // tm, n // tn),
        out_shape=[act, act, jax.ShapeDtypeStruct((2, t, n), BF16), jax.ShapeDtypeStruct((t // tm, 2, n), F32)],
        in_specs=[pl.BlockSpec((tm, k), lambda i, j: (i, 0)), pl.BlockSpec((tn, k), lambda i, j: (j, 0)), pair, blk, blk],
        out_specs=[blk, blk, pair, pl.BlockSpec((None, 2, tn), lambda i, j: (i, 0, j))],
        compiler_params=pltpu.CompilerParams(dimension_semantics=("parallel", "parallel"),
                                             vmem_limit_bytes=_vmem_limit(blocks)),
    )(dx, w_o, z2, oa, ob)


def _adamw_math(w, g, m, v):
    m = ADAM_B1 * m + (1.0 - ADAM_B1) * g
    v = ADAM_B2 * v + (1.0 - ADAM_B2) * (g * g)
    m_hat = m / (1.0 - ADAM_B1 ** ADAM_STEP)
    v_hat = v / (1.0 - ADAM_B2 ** ADAM_STEP)
    delta = -ADAM_LR * (m_hat / (jnp.sqrt(v_hat) + ADAM_EPS) + ADAM_WD * w)
    return delta, m, v


def _sum_in_order(ref):
    g = ref[0].astype(F32)
    for p in range(1, ref.shape[0]):
        g = g + ref[p].astype(F32)
    return g


def adamw_sharded(name, recv, own, chip, w, m, v):
    r, c = w.shape
    rows = r // DEPTH
    tr = _tile(rows, max(SUBLANES, (256 * 1024) // c), 16)
    per = rows // tr

    def body(chip_ref, *refs):
        recv_refs, own_refs = refs[:DEPTH], refs[DEPTH:2 * DEPTH]
        w_ref, m_ref, v_ref, g_ref, d_ref, mo_ref, vo_ref = refs[2 * DEPTH:]
        step = pl.program_id(0)
        for l in range(DEPTH):
            @pl.when((step >= l * per) & (step < (l + 1) * per))
            def _(l=l):
                g = None
                for q in range(N_CHIP):
                    part = jnp.where(chip_ref[0] == q, own_refs[l][...], recv_refs[l][q]).astype(F32)
                    g = part if g is None else g + part
                delta, m2, v2 = _adamw_math(w_ref[...], g, m_ref[...], v_ref[...])
                g_ref[...] = g
                d_ref[...] = delta
                mo_ref[...] = m2
                vo_ref[...] = v2

    def in_layer(l, i):
        return jnp.clip(i - l * per, 0, per - 1)

    recv_specs = [pl.BlockSpec((N_CHIP, tr, c), lambda i, chip_ref, l=l: (0, in_layer(l, i), 0)) for l in range(DEPTH)]
    own_specs = [pl.BlockSpec((None, tr, c), lambda i, chip_ref, l=l: (chip_ref[0], in_layer(l, i), 0))
                 for l in range(DEPTH)]
    blk = pl.BlockSpec((tr, c), lambda i, chip_ref: (i, 0))
    out = jax.ShapeDtypeStruct((r, c), F32)
    blocks = DEPTH * _nbytes((N_CHIP + 1, tr, c), BF16) + 7 * _nbytes((tr, c), F32)
    return pl.pallas_call(
        body, name=name, out_shape=[out, out, out, out],
        grid_spec=pltpu.PrefetchScalarGridSpec(
            num_scalar_prefetch=1, grid=(DEPTH * per,), in_specs=recv_specs + own_specs + [blk, blk, blk],
            out_specs=[blk, blk, blk, blk]),
        compiler_params=pltpu.CompilerParams(dimension_semantics=("parallel",),
                                             vmem_limit_bytes=_vmem_limit(blocks)),
    )(chip, *recv, *own, w, m, v)


def sum_partials(name, parts):
    _, r, c = parts.shape

    def body(p_ref, o_ref):
        o_ref[...] = _sum_in_order(p_ref)

    return pl.pallas_call(body, name=name, out_shape=jax.ShapeDtypeStruct((r, c), F32))(parts)


def adamw_small(name, g, w, m, v):
    def body(g_ref, w_ref, m_ref, v_ref, d_ref, mo_ref, vo_ref):
        delta, m2, v2 = _adamw_math(w_ref[...], g_ref[...], m_ref[...], v_ref[...])
        d_ref[...] = delta
        mo_ref[...] = m2
        vo_ref[...] = v2

    out = jax.ShapeDtypeStruct(w.shape, F32)
    return pl.pallas_call(body, name=name, out_shape=[out, out, out])(g, w, m, v)


def _pack(arrays):
    flat = jnp.concatenate([a.reshape(-1) for a in arrays])
    n = flat.shape[0]
    rows = -(-n // (SUBLANES * LANES)) * SUBLANES
    flat = jnp.pad(flat, (0, rows * LANES - n))
    return flat.reshape(rows, LANES)


def _unpack(packed, shapes):
    flat = packed.reshape(-1)
    out, off = [], 0
    for s in shapes:
        n = math.prod(s)
        out.append(flat[off:off + n].reshape(s))
        off += n
    return out


def kernel(x, norm1_g, w_in, b_in, conv_a_w, conv_a_b, lru_wr, lru_br, lru_wi, lru_bi, lru_lam, conv_b_w, w_pa, w_pb, w_o, norm2_g, w_mlp1, w_mlp2, final_g, loss_target, m_norm1_g, m_w_in, m_b_in, m_conv_a_w, m_conv_a_b, m_lru_wr, m_lru_br, m_lru_wi, m_lru_bi, m_lru_lam, m_conv_b_w, m_w_pa, m_w_pb, m_w_o, m_norm2_g, m_w_mlp1, m_w_mlp2, m_final_g, v_norm1_g, v_w_in, v_b_in, v_conv_a_w, v_conv_a_b, v_lru_wr, v_lru_br, v_lru_wi, v_lru_bi, v_lru_lam, v_conv_b_w, v_w_pa, v_w_pb, v_w_o, v_norm2_g, v_w_mlp1, v_w_mlp2, v_final_g):
    weights = dict(norm1_g=norm1_g, w_in=w_in, b_in=b_in, conv_a_w=conv_a_w, conv_a_b=conv_a_b, lru_wr=lru_wr,
                   lru_br=lru_br, lru_wi=lru_wi, lru_bi=lru_bi, lru_lam=lru_lam, conv_b_w=conv_b_w, w_pa=w_pa,
                   w_pb=w_pb, w_o=w_o, norm2_g=norm2_g, w_mlp1=w_mlp1, w_mlp2=w_mlp2, final_g=final_g)
    mom1 = dict(norm1_g=m_norm1_g, w_in=m_w_in, b_in=m_b_in, conv_a_w=m_conv_a_w, conv_a_b=m_conv_a_b,
                lru_wr=m_lru_wr, lru_br=m_lru_br, lru_wi=m_lru_wi, lru_bi=m_lru_bi, lru_lam=m_lru_lam,
                conv_b_w=m_conv_b_w, w_pa=m_w_pa, w_pb=m_w_pb, w_o=m_w_o, norm2_g=m_norm2_g, w_mlp1=m_w_mlp1,
                w_mlp2=m_w_mlp2, final_g=m_final_g)
    mom2 = dict(norm1_g=v_norm1_g, w_in=v_w_in, b_in=v_b_in, conv_a_w=v_conv_a_w, conv_a_b=v_conv_a_b,
                lru_wr=v_lru_wr, lru_br=v_lru_br, lru_wi=v_lru_wi, lru_bi=v_lru_bi, lru_lam=v_lru_lam,
                conv_b_w=v_conv_b_w, w_pa=v_w_pa, w_pb=v_w_pb, w_o=v_w_o, norm2_g=v_norm2_g, w_mlp1=v_w_mlp1,
                w_mlp2=v_w_mlp2, final_g=v_final_g)
    names = list(weights)

    t, d = x.shape[1], x.shape[2]
    n_in = b_in.shape[1]
    nb, bw = lru_wr.shape[1], lru_wr.shape[3]
    mx, my, mc = _my_coords()
    me = _flat(mx, my, mc)
    me_index = me.astype(jnp.int32).reshape(1)
    my_core = mc.astype(jnp.int32).reshape(1)
    my_chip = (2 * mx + my).astype(jnp.int32).reshape(1)
    x0 = x.reshape(t, d)
    target = loss_target.reshape(t, d)

    big = ["w_in", "w_pa", "w_pb", "w_o", "w_mlp1", "w_mlp2", "lru_wr", "lru_wi"]
    big_axis = dict(w_in=1, w_pa=0, w_pb=0, w_o=0, w_mlp1=1, w_mlp2=0, lru_wr=1, lru_wi=1)
    small_sharded = ["conv_a_w", "conv_b_w", "lru_br", "lru_bi"]
    small_packed = _pack([weights[k] for k in small_sharded])
    small_rows = all_gather("gather_small_params", [small_packed[None]], [0])[0].reshape(N_DEV, -1)
    groups = (("w_in", "lru_wr", "lru_wi"), ("w_pa", "w_pb", "w_o"), ("w_mlp1", "w_mlp2"))
    gathers, full = {}, [{} for _ in range(DEPTH)]
    token = small_rows
    for l in range(DEPTH):
        for gi, group in enumerate(groups):
            axes = [big_axis[k] for k in group]
            lands = [place_shard(f"place_{k}_l{l}", weights[k], l, big_axis[k], me_index) for k in group]
            own_sems, flying, token = gather_start(f"gather_start_l{l}g{gi}", lands, axes, after=token)
            gathers[l, gi] = dict(axes=axes, own=own_sems, flying=flying)

    def pass_on(l, gi, after):
        st = gathers[l, gi]
        st["passed"], st["flying"] = gather_pass_on(f"gather_pass_on_l{l}g{gi}", st["flying"], st["axes"], st["own"], after)

    def finish(l, gi, after):
        st = gathers[l, gi]
        out = gather_finish(f"gather_finish_l{l}g{gi}", st["flying"], st["axes"], st["own"], st["passed"], after)
        full[l].update(zip(groups[gi], out))

    pass_on(0, 0, token)
    finish(0, 0, token)
    small_full = {}
    off = 0
    for k in small_sharded:
        shard_shape = weights[k].shape
        n = math.prod(shard_shape)
        g = small_rows[:, off:off + n].reshape(N_DEV, *shard_shape)
        small_full[k] = jnp.moveaxis(g, 0, 2).reshape(shard_shape[0], shard_shape[1], N_DEV * shard_shape[2])
        off += n

    def layer_params(l):
        return dict(
            g1=norm1_g[l][None], b_in=b_in[l][None], caw=small_full["conv_a_w"][l], cab=conv_a_b[l][None],
            br=small_full["lru_br"][l].reshape(1, d), bi=small_full["lru_bi"][l].reshape(1, d),
            lam=lru_lam[l][None], cbw=small_full["conv_b_w"][l], g2=norm2_g[l][None])

    params = [layer_params(l) for l in range(DEPTH)]

    relu2 = lambda acc: (acc, jnp.square(jnp.maximum(acc, 0.0)))
    add_res = lambda acc, r: (acc + r,)
    relu2_bwd = lambda acc, pre: (acc * (2.0 * jnp.maximum(pre, 0.0)),)

    saved = []
    xl = x0
    for l in range(DEPTH):
        p, w = params[l], full[l]
        tag = f"l{l}"
        h1 = rmsnorm_fwd(f"norm1_{tag}", xl, p["g1"], after=token if l == 0 else None)
        z5 = mm_in_proj(f"in_proj5_{tag}", h1, w["w_in"], p["b_in"], 0, N_BRANCH)
        z2 = mm_in_proj(f"in_proj2_{tag}", h1, w["w_in"], p["b_in"], N_BRANCH * d, N_SPLIT - N_BRANCH)
        pass_on(l, 1, z2)
        pa, pb, hl = branch_fwd(f"branch_fwd_{tag}", z5, p["caw"], p["cab"], p["cbw"], w["lru_wr"], w["lru_wi"],
                                p["br"], p["bi"], p["lam"])
        finish(l, 1, pa)
        oa, ob, mg = proj_ab_merge(f"proj_ab_merge_{tag}", pa, pb, w["w_pa"], w["w_pb"], z2)
        pass_on(l, 2, mg)
        x1 = mm_plain(f"proj_o_{tag}", mg, w["w_o"], epilogue=add_res, extras=(xl,), extra_kinds=("full",))
        h2 = rmsnorm_fwd(f"norm2_{tag}", x1, p["g2"])
        finish(l, 2, h2)
        pre, uu = mm_plain(f"mlp1_{tag}", h2, w["w_mlp1"], epilogue=relu2, out_dtypes=[F32, BF16])
        if l + 1 < DEPTH:
            pass_on(l + 1, 0, pre)
        x2 = mm_plain(f"mlp2_{tag}", uu, w["w_mlp2"], epilogue=add_res, extras=(x1,), extra_kinds=("full",))
        if l + 1 < DEPTH:
            finish(l + 1, 0, x2)
        saved.append(dict(x0=xl, h1=h1, z5=z5, z2=z2, pa=pa, pb=pb, hl=hl, oa=oa, ob=ob, mg=mg, x1=x1, h2=h2,
                          pre=pre, uu=uu))
        xl = x2

    dx, dxb, d_final_g, loss_cols = loss_head("loss_head", xl, final_g[None], target)
    loss = lax.psum(jnp.sum(loss_cols), ("x", "y", "c"))

    small_grads = {}
    exchanges = {}

    def pair_start(l, gi, grads_g):
        axes = [big_axis[k] for k in groups[gi]]
        sems, grads_g, theirs, tok = pair_exchange_start(f"pair_start_l{l}g{gi}", grads_g, axes)
        exchanges[l, gi] = dict(axes=axes, pair_sems=sems, grads=grads_g, theirs=theirs)
        return tok

    def chip_start(l, gi, after):
        st = exchanges[l, gi]
        grads_g, theirs = pair_exchange_finish(f"pair_finish_l{l}g{gi}", st["grads"], st["theirs"], st["axes"],
                                               st["pair_sems"], after)
        sums = [chip_sum(f"chip_sum_{k}_l{l}", g, a, th, my_core)
                for k, g, a, th in zip(groups[gi], grads_g, st["axes"], theirs)]
        st["chip_sems"], st["sums"], st["recv"], tok = chip_exchange_start(f"chip_start_l{l}g{gi}", sums)
        return tok

    token = None
    for l in reversed(range(DEPTH)):
        p, w, sv = params[l], full[l], saved[l]
        tag = f"l{l}"
        dpre = mm_plain(f"mlp2_dx_{tag}", dxb, w["w_mlp2"], tb=True, out_dtype=BF16, epilogue=relu2_bwd,
                        extras=(sv["pre"],), extra_kinds=("full",), after=token)
        dw2 = mm_plain(f"mlp2_dw_{tag}", sv["uu"], dxb, ta=True, out_dtype=BF16)
        dw1 = mm_plain(f"mlp1_dw_{tag}", sv["h2"], dpre, ta=True, out_dtype=BF16)
        token = pair_start(l, 2, [dw1, dw2])
        dh2 = mm_plain(f"mlp1_dx_{tag}", dpre, w["w_mlp1"], tb=True, after=token)
        dx1, dx1b, dg2 = rmsnorm_bwd(f"norm2_bwd_{tag}", dh2, sv["x1"], p["g2"], dx)
        token = chip_start(l, 2, dx1b)
        dwo = mm_plain(f"proj_o_dw_{tag}", sv["mg"], dx1b, ta=True, out_dtype=BF16, after=token)
        doa, dob, dz2, dbias2 = proj_o_dx_merge_bwd(f"proj_o_dx_merge_bwd_{tag}", dx1b, w["w_o"], sv["z2"], sv["oa"],
                                                    sv["ob"])
        dbias2 = jnp.sum(dbias2, axis=0)
        dpa = mm_plain(f"proj_a_dx_{tag}", doa, w["w_pa"], tb=True)
        dpb = mm_plain(f"proj_b_dx_{tag}", dob, w["w_pb"], tb=True)
        dwpa = mm_plain(f"proj_a_dw_{tag}", sv["pa"], doa, ta=True, out_dtype=BF16)
        dwpb = mm_plain(f"proj_b_dw_{tag}", sv["pb"], dob, ta=True, out_dtype=BF16)
        token = pair_start(l, 1, [dwpa, dwpb, dwo])
        (dz5, dbias5, dcaw, dcab, dcbw, dwr, dwi, dbr, dbi, dlam) = branch_bwd(
            f"branch_bwd_{tag}", sv["z5"], sv["hl"], dpa, dpb, p["caw"], p["cab"], p["cbw"], w["lru_wr"], w["lru_wi"],
            p["br"], p["bi"], p["lam"])
        token = chip_start(l, 1, dz5)
        dwin = mm_in_proj_dw(f"in_proj5_dw_{tag}", sv["h1"], dz5, 0, n_in, after=token)
        dwin = mm_in_proj_dw(f"in_proj2_dw_{tag}", sv["h1"], dz2, N_BRANCH * d, n_in, prev=dwin)
        token = pair_start(l, 0, [dwin, dwr.astype(BF16), dwi.astype(BF16)])
        token = chip_start(l, 0, token)
        dh1 = mm_in_proj_dh(f"in_proj5_dx_{tag}", dz5, w["w_in"], 0, after=token)
        dh1 = mm_in_proj_dh(f"in_proj2_dx_{tag}", dz2, w["w_in"], N_BRANCH * d, res=dh1)
        dx, dxb, dg1 = rmsnorm_bwd(f"norm1_bwd_{tag}", dh1, sv["x0"], p["g1"], dx1)
        token = None
        small_grads[l] = dict(norm1_g=dg1, b_in=jnp.concatenate([dbias5.reshape(1, -1), dbias2.reshape(1, -1)], axis=1),
                              conv_a_b=dcab, lru_lam=dlam, norm2_g=dg2, conv_a_w=dcaw, conv_b_w=dcbw,
                              lru_br=dbr, lru_bi=dbi)
    grad_x = dx.reshape(x.shape)


    replicated = ["norm1_g", "b_in", "conv_a_b", "lru_lam", "norm2_g"]
    small_list = [jnp.stack([small_grads[l][k] for l in range(DEPTH)]) for k in replicated + small_sharded]
    small_list.append(d_final_g)
    packed_partial = _pack(small_list)
    partials = all_gather("gather_small_grads", [packed_partial[None]], [0])[0]
    summed = sum_partials("sum_small_grads", partials)
    pieces = _unpack(summed, [a.shape for a in small_list])
    small_sum = dict(zip(replicated + small_sharded + ["final_g"], pieces))

    grads = {}
    for k in replicated:
        grads[k] = small_sum[k].reshape(weights[k].shape)
    grads["final_g"] = small_sum["final_g"].reshape(weights["final_g"].shape)
    for k in small_sharded:
        shard_shape = weights[k].shape
        g = small_sum[k].reshape(shard_shape[0], shard_shape[1], N_DEV, shard_shape[2])
        grads[k] = lax.dynamic_index_in_dim(g, me, axis=2, keepdims=False)

    small_names = replicated + ["final_g"] + small_sharded
    small_shapes = [weights[k].shape for k in small_names]
    pk = lambda src: _pack([src[k] for k in small_names])
    sd, sm, sv2 = adamw_small("adamw_small", pk(grads), pk(weights), pk(mom1), pk(mom2))
    delta = dict(zip(small_names, _unpack(sd, small_shapes)))
    new_m = dict(zip(small_names, _unpack(sm, small_shapes)))
    new_v = dict(zip(small_names, _unpack(sv2, small_shapes)))

    after = dx
    for gi in reversed(range(len(groups))):
        for l in reversed(range(DEPTH)):
            st = exchanges[l, gi]
            st["own"], st["recv"] = chip_exchange_finish(f"chip_finish_l{l}g{gi}", st["sums"], st["recv"],
                                                         st["chip_sems"], after)
        for i, k in enumerate(groups[gi]):
            shape = weights[k].shape
            c = shape[-1]
            flat = lambda a: a.reshape(-1, c)
            per_chip = lambda key: [exchanges[l, gi][key][i].reshape(N_CHIP, -1, c) for l in range(DEPTH)]
            g, dl, m2, v2 = adamw_sharded(f"adamw_{k}", per_chip("recv"), per_chip("own"), my_chip,
                                          flat(weights[k]), flat(mom1[k]), flat(mom2[k]))
            grads[k], delta[k], new_m[k], new_v[k] = (a.reshape(shape) for a in (g, dl, m2, v2))
            after = g

    return (loss, grad_x, *[grads[k] for k in names], *[delta[k] for k in names],
            *[new_m[k] for k in names], *[new_v[k] for k in names])
```

```python
import functools
import math

import jax
import jax.numpy as jnp
from jax import lax
from jax.experimental import pallas as pl
from jax.experimental.pallas import tpu as pltpu

F32 = jnp.float32
BF16 = jnp.bfloat16

N_DEV = 8
DEPTH = 2
N_SPLIT = 7
N_BRANCH = 5
EPS = 1e-6
LRU_C = 8.0
CONV_A_WIDTH = 4
CONV_B_WIDTH = 3
GELU_C = math.sqrt(2.0 / math.pi)
GELU_A = 0.044715

ADAM_LR = 0.001
ADAM_B1 = 0.9
ADAM_B2 = 0.999
ADAM_EPS = 1e-08
ADAM_WD = 0.01
ADAM_STEP = 10

LANES = 128
SUBLANES = 8
VMEM_LIMIT_CAP = 56 * 2 ** 20
MESH = pl.DeviceIdType.MESH
ANY = pl.BlockSpec(memory_space=pl.ANY)


def _tile(n, target, align=LANES):
    if n <= target:
        return n
    t = (target // align) * align
    while t >= align:
        if n % t == 0:
            return t
        t -= align
    return n


def _nbytes(shape, dtype):
    return math.prod(shape) * jnp.dtype(dtype).itemsize


def _vmem_limit(block_bytes, scratch_bytes=0):
    est = 2 * block_bytes + scratch_bytes
    return int(min(max(2 * est, 32 * 2 ** 20), VMEM_LIMIT_CAP))


def _my_coords():
    return lax.axis_index("x"), lax.axis_index("y"), lax.axis_index("c")


def _flat(x, y, c):
    return 4 * x + 2 * y + c


def _peer(k, x, y, c):
    return (1 - x if k & 4 else x, 1 - y if k & 2 else y, 1 - c if k & 1 else c)


def _slab(ref, axis, start, size):
    idx = tuple(pl.ds(start, size) if d == axis else slice(None) for d in range(len(ref.shape)))
    return ref.at[idx]


def all_gather(name, shards, axes):
    n = len(shards)
    sizes = [s.shape[a] for s, a in zip(shards, axes)]
    out_shape = []
    for s, a in zip(shards, axes):
        full = list(s.shape)
        full[a] *= N_DEV
        out_shape.append(jax.ShapeDtypeStruct(tuple(full), s.dtype))

    def body(*refs):
        srcs, outs = refs[:n], refs[n:2 * n]
        send_sems, recv_sems, local_sems = refs[2 * n:]
        x, y, c = _my_coords()
        me = _flat(x, y, c)
        local = []
        for i in range(n):
            cp = pltpu.make_async_copy(srcs[i], _slab(outs[i], axes[i], me * sizes[i], sizes[i]), local_sems.at[i])
            cp.start()
            local.append(cp)
        remote = []
        for k in range(1, N_DEV):
            for i in range(n):
                mine = _slab(outs[i], axes[i], me * sizes[i], sizes[i])
                cp = pltpu.make_async_remote_copy(
                    src_ref=srcs[i], dst_ref=mine,
                    send_sem=send_sems.at[(k - 1) * n + i], recv_sem=recv_sems.at[(k - 1) * n + i],
                    device_id=_peer(k, x, y, c), device_id_type=MESH)
                cp.start()
                remote.append(cp)
        for cp in remote:
            cp.wait()
        for cp in local:
            cp.wait()

    return pl.pallas_call(
        body, name=name, out_shape=out_shape,
        in_specs=[ANY] * n, out_specs=[ANY] * n,
        scratch_shapes=[pltpu.SemaphoreType.DMA(((N_DEV - 1) * n,)),
                        pltpu.SemaphoreType.DMA(((N_DEV - 1) * n,)),
                        pltpu.SemaphoreType.DMA((n,))],
    )(*shards)


N_CHIP = 4
OTHER_CHIPS = ((1, 0), (0, 1), (1, 1))


HBM = pl.BlockSpec(memory_space=pltpu.HBM)
SEM = pl.BlockSpec(memory_space=pltpu.SEMAPHORE)
TOKEN = jax.ShapeDtypeStruct((SUBLANES, LANES), F32)


def _split_copy_params():
    return pltpu.CompilerParams(has_side_effects=pltpu.SideEffectType.DATAFLOW_SIDE_EFFECTING)


def _in_hbm(a):
    return pltpu.with_memory_space_constraint(a, pltpu.HBM)


def _other_chips(x, y):
    return [(x + dx - 2 * x * dx, y + dy - 2 * y * dy) for dx, dy in OTHER_CHIPS]


def place_shard(name, w, layer, axis, me):
    shard = w.shape[1:]
    if len(shard) == 2 and axis == 1:
        r, cs = shard
        tr = _tile(r, max(SUBLANES, (512 * 1024) // cs), 16)
        grid, full = (r // tr,), (r, N_DEV * cs)
        in_spec = pl.BlockSpec((None, tr, cs), lambda i, me_ref: (layer, i, 0))
        out_spec = pl.BlockSpec((tr, cs), lambda i, me_ref: (i, me_ref[0]))
    elif len(shard) == 2 and axis == 0:
        rs, cs = shard
        tr = _tile(rs, max(SUBLANES, (512 * 1024) // cs), 16)
        per = rs // tr
        grid, full = (per,), (N_DEV * rs, cs)
        in_spec = pl.BlockSpec((None, tr, cs), lambda i, me_ref: (layer, i, 0))
        out_spec = pl.BlockSpec((tr, cs), lambda i, me_ref: (me_ref[0] * per + i, 0))
    else:
        assert len(shard) == 3 and axis == 1
        lead, rs, cs = shard
        grid, full = (1,), (lead, N_DEV * rs, cs)
        in_spec = pl.BlockSpec((None, lead, rs, cs), lambda i, me_ref: (layer, 0, 0, 0))
        out_spec = pl.BlockSpec((lead, rs, cs), lambda i, me_ref: (0, me_ref[0], 0))

    def body(me_ref, w_ref, o_ref):
        o_ref[...] = w_ref[...].astype(BF16)

    return pl.pallas_call(
        body, name=name, out_shape=jax.ShapeDtypeStruct(full, BF16),
        grid_spec=pltpu.PrefetchScalarGridSpec(num_scalar_prefetch=1, grid=grid, in_specs=[in_spec],
                                               out_specs=out_spec),
        compiler_params=pltpu.CompilerParams(dimension_semantics=("parallel",)),
    )(me, w)


class _GatherCopies:
    def __init__(self, land, axes):
        self.land, self.axes, self.n = land, axes, len(land)
        self.sizes = [r.shape[a] // N_DEV for r, a in zip(land, axes)]
        x, y, c = _my_coords()
        self.c = c
        self.me, self.sibling = (x, y, c), (x, y, 1 - c)
        self.chips = _other_chips(x, y)

    def copy(self, sems, slot, i, block, to):
        rows = _slab(self.land[i], self.axes[i], _flat(*block) * self.sizes[i], self.sizes[i])
        return pltpu.make_async_remote_copy(
            src_ref=rows, dst_ref=rows, send_sem=sems[0].at[slot * self.n + i], recv_sem=sems[1].at[slot * self.n + i],
            device_id=to, device_id_type=MESH)

    def own(self, sems):
        out = []
        for i in range(self.n):
            out.append(self.copy(sems, 0, i, self.me, self.sibling))
            out += [self.copy(sems, 1 + j, i, self.me, (*chip, self.c)) for j, chip in enumerate(self.chips)]
        return out

    def landed_over_ici(self, sems):
        return [self.copy(sems, 1 + j, i, (*chip, self.c), self.me)
                for j, chip in enumerate(self.chips) for i in range(self.n)]

    def passed_on(self, sems):
        return [self.copy(sems, j, i, (*chip, self.c), self.sibling)
                for j, chip in enumerate(self.chips) for i in range(self.n)]

    def from_sibling(self, own_sems, pass_sems):
        out = [self.copy(own_sems, 0, i, self.sibling, self.me) for i in range(self.n)]
        out += [self.copy(pass_sems, j, i, (*chip, 1 - self.c), self.me)
                for j, chip in enumerate(self.chips) for i in range(self.n)]
        return out


def gather_start(name, land, axes, after=None):
    n = len(land)
    order = [] if after is None else [after]
    n_in = n + len(order)

    def body(*refs):
        own_sems, token = (refs[n_in], refs[n_in + 1]), refs[n_in + n + 2]
        for cp in _GatherCopies(refs[:n], axes).own(own_sems):
            cp.start()
        token[...] = jnp.zeros_like(token)

    out = pl.pallas_call(
        body, name=name,
        out_shape=(pltpu.SemaphoreType.DMA((4 * n,)), pltpu.SemaphoreType.DMA((4 * n,)),
                   *[pltpu.HBM(a.shape, a.dtype) for a in land], TOKEN),
        in_specs=[HBM] * n + [ANY] * len(order),
        out_specs=(SEM, SEM, *[HBM] * n, pl.BlockSpec(memory_space=pltpu.VMEM)),
        input_output_aliases={i: 2 + i for i in range(n)}, compiler_params=_split_copy_params(),
    )(*[_in_hbm(a) for a in land], *order)
    return (out[0], out[1]), list(out[2:2 + n]), out[-1]


def gather_pass_on(name, land, axes, own_sems, after):
    n = len(land)

    def body(*refs):
        sems_in, pass_sems = (refs[n], refs[n + 1]), (refs[n + 3], refs[n + 4])
        copies = _GatherCopies(refs[:n], axes)
        for arrived, onward in zip(copies.landed_over_ici(sems_in), copies.passed_on(pass_sems)):
            arrived.wait_recv()
            onward.start()

    out = pl.pallas_call(
        body, name=name,
        out_shape=(pltpu.SemaphoreType.DMA((3 * n,)), pltpu.SemaphoreType.DMA((3 * n,)),
                   *[pltpu.HBM(a.shape, a.dtype) for a in land]),
        in_specs=[HBM] * n + [SEM, SEM, ANY], out_specs=(SEM, SEM, *[HBM] * n),
        input_output_aliases={i: 2 + i for i in range(n)}, compiler_params=_split_copy_params(),
    )(*land, *own_sems, after)
    return (out[0], out[1]), list(out[2:])


def gather_finish(name, land, axes, own_sems, pass_sems, after):
    n = len(land)

    def body(*refs):
        own, passed = (refs[n], refs[n + 1]), (refs[n + 2], refs[n + 3])
        copies = _GatherCopies(refs[:n], axes)
        for cp in copies.from_sibling(own, passed):
            cp.wait_recv()
        for cp in copies.own(own) + copies.passed_on(passed):
            cp.wait_send()

    return list(pl.pallas_call(
        body, name=name, out_shape=tuple(pltpu.HBM(a.shape, a.dtype) for a in land),
        in_specs=[HBM] * n + [SEM] * 4 + [ANY], out_specs=tuple([HBM] * n),
        input_output_aliases={i: i for i in range(n)}, compiler_params=_split_copy_params(),
    )(*land, *own_sems, *pass_sems, after))


def _pair_copies(grads, axes, theirs, sems):
    x, y, c = _my_coords()
    out = []
    for i, (g, a) in enumerate(zip(grads, axes)):
        size = g.shape[a] // N_DEV
        for q in range(N_CHIP):
            k = i * N_CHIP + q
            out.append(pltpu.make_async_remote_copy(
                src_ref=_slab(g, a, (2 * q + 1 - c) * size, size), dst_ref=theirs[i].at[q],
                send_sem=sems[0].at[k], recv_sem=sems[1].at[k], device_id=(x, y, 1 - c), device_id_type=MESH))
    return out


def pair_exchange_start(name, grads, axes):
    n = len(grads)
    theirs = []
    for g, a in zip(grads, axes):
        shard = list(g.shape)
        shard[a] //= N_DEV
        theirs.append(lax.empty((N_CHIP, *shard), g.dtype))

    def body(*refs):
        sems, token = (refs[2 * n], refs[2 * n + 1]), refs[4 * n + 2]
        for cp in _pair_copies(refs[:n], axes, refs[n:2 * n], sems):
            cp.start()
        token[...] = jnp.zeros_like(token)

    both = [pltpu.HBM(a.shape, a.dtype) for a in list(grads) + theirs]
    out = pl.pallas_call(
        body, name=name,
        out_shape=(pltpu.SemaphoreType.DMA((N_CHIP * n,)), pltpu.SemaphoreType.DMA((N_CHIP * n,)), *both, TOKEN),
        in_specs=[HBM] * (2 * n), out_specs=(SEM, SEM, *[HBM] * (2 * n), pl.BlockSpec(memory_space=pltpu.VMEM)),
        input_output_aliases={i: 2 + i for i in range(2 * n)}, compiler_params=_split_copy_params(),
    )(*[_in_hbm(a) for a in grads], *[_in_hbm(a) for a in theirs])
    return (out[0], out[1]), list(out[2:2 + n]), list(out[2 + n:2 + 2 * n]), out[-1]


def pair_exchange_finish(name, grads, theirs, axes, sems, after):
    n = len(grads)

    def body(*refs):
        for cp in _pair_copies(refs[:n], axes, refs[n:2 * n], (refs[2 * n], refs[2 * n + 1])):
            cp.wait_send()
            cp.wait_recv()

    both = tuple(pltpu.HBM(a.shape, a.dtype) for a in list(grads) + list(theirs))
    out = pl.pallas_call(
        body, name=name, out_shape=both,
        in_specs=[HBM] * (2 * n) + [SEM, SEM, ANY], out_specs=tuple([HBM] * (2 * n)),
        input_output_aliases={i: i for i in range(2 * n)}, compiler_params=_split_copy_params(),
    )(*grads, *theirs, *sems, after)
    return list(out[:n]), list(out[n:])


def chip_sum(name, grad, axis, theirs, core):
    if grad.ndim == 2 and axis == 1:
        r, cs = grad.shape[0], grad.shape[1] // N_DEV
        tr = _tile(r, max(SUBLANES, (512 * 1024) // cs), 16)
        grid = (N_CHIP, r // tr)
        g_spec = pl.BlockSpec((tr, cs), lambda q, i, core_ref: (i, 2 * q + core_ref[0]))
        t_spec = pl.BlockSpec((None, tr, cs), lambda q, i, core_ref: (q, i, 0))
    elif grad.ndim == 2 and axis == 0:
        rs, cs = grad.shape[0] // N_DEV, grad.shape[1]
        tr = _tile(rs, max(SUBLANES, (512 * 1024) // cs), 16)
        per = rs // tr
        grid = (N_CHIP, per)
        g_spec = pl.BlockSpec((tr, cs), lambda q, i, core_ref: ((2 * q + core_ref[0]) * per + i, 0))
        t_spec = pl.BlockSpec((None, tr, cs), lambda q, i, core_ref: (q, i, 0))
    else:
        assert grad.ndim == 3 and axis == 1
        lead, rs, cs = grad.shape[0], grad.shape[1] // N_DEV, grad.shape[2]
        grid = (N_CHIP, 1)
        g_spec = pl.BlockSpec((lead, rs, cs), lambda q, i, core_ref: (0, 2 * q + core_ref[0], 0))
        t_spec = pl.BlockSpec((None, lead, rs, cs), lambda q, i, core_ref: (q, 0, 0, 0))

    def body(core_ref, g_ref, t_ref, o_ref):
        o_ref[...] = (g_ref[...].astype(F32) + t_ref[...].astype(F32)).astype(o_ref.dtype)

    return pl.pallas_call(
        body, name=name, out_shape=jax.ShapeDtypeStruct(theirs.shape, theirs.dtype),
        grid_spec=pltpu.PrefetchScalarGridSpec(num_scalar_prefetch=1, grid=grid, in_specs=[g_spec, t_spec],
                                               out_specs=t_spec),
        compiler_params=pltpu.CompilerParams(dimension_semantics=("parallel", "parallel")),
    )(core, grad, theirs)


def _chip_copies(sums, recv, sems):
    n = len(sums)
    x, y, c = _my_coords()
    out = []
    for j, (tx, ty) in enumerate(_other_chips(x, y)):
        for i in range(n):
            out.append(pltpu.make_async_remote_copy(
                src_ref=sums[i].at[2 * tx + ty], dst_ref=recv[i].at[2 * x + y],
                send_sem=sems[0].at[j * n + i], recv_sem=sems[1].at[j * n + i],
                device_id=(tx, ty, c), device_id_type=MESH))
    return out


def chip_exchange_start(name, sums):
    n = len(sums)

    def body(*refs):
        sems, token = (refs[2 * n], refs[2 * n + 1]), refs[4 * n + 2]
        for cp in _chip_copies(refs[:n], refs[n:2 * n], sems):
            cp.start()
        token[...] = jnp.zeros_like(token)

    both = [pltpu.HBM(a.shape, a.dtype) for a in sums]
    out = pl.pallas_call(
        body, name=name,
        out_shape=(pltpu.SemaphoreType.DMA((3 * n,)), pltpu.SemaphoreType.DMA((3 * n,)), *both, *both, TOKEN),
        in_specs=[HBM] * (2 * n), out_specs=(SEM, SEM, *[HBM] * (2 * n), pl.BlockSpec(memory_space=pltpu.VMEM)),
        input_output_aliases={i: 2 + i for i in range(2 * n)}, compiler_params=_split_copy_params(),
    )(*[_in_hbm(a) for a in sums], *[_in_hbm(lax.empty(a.shape, a.dtype)) for a in sums])
    return (out[0], out[1]), list(out[2:2 + n]), list(out[2 + n:2 + 2 * n]), out[-1]


def chip_exchange_finish(name, sums, recv, sems, after):
    n = len(sums)

    def body(*refs):
        for cp in _chip_copies(refs[:n], refs[n:2 * n], (refs[2 * n], refs[2 * n + 1])):
            cp.wait_send()
            cp.wait_recv()

    both = tuple(pltpu.HBM(a.shape, a.dtype) for a in sums)
    out = pl.pallas_call(
        body, name=name, out_shape=both + both,
        in_specs=[HBM] * (2 * n) + [SEM, SEM, ANY], out_specs=tuple([HBM] * (2 * n)),
        input_output_aliases={i: i for i in range(2 * n)}, compiler_params=_split_copy_params(),
    )(*sums, *recv, *sems, after)
    return list(out[:n]), list(out[n:])


def _mm(name, a, b, *, ta, tb, grid, tm, tn, tk, a_spec, b_spec, outs, extras=(), epilogue=None, alias=None,
        after=None, sum_in_out=False):
    n_extra, n_out = len(extras), len(outs)
    nk = grid[2]
    dn = (((0,) if ta else (1,), (1,) if tb else (0,)), ((), ()))

    def finish(acc, extra_refs, out_refs):
        vals = epilogue(acc, *[r[...] for r in extra_refs]) if epilogue is not None else (acc,)
        for o_ref, v in zip(out_refs, vals):
            o_ref[...] = v.astype(o_ref.dtype)

    def body(*refs):
        a_ref, b_ref = refs[0], refs[1]
        extra_refs = refs[2:2 + n_extra]
        first_out = 2 + n_extra + (1 if alias is not None else 0) + (1 if after is not None else 0)
        out_refs = refs[first_out:first_out + n_out]
        part = lax.dot_general(a_ref[...], b_ref[...], dn, preferred_element_type=F32)
        if nk == 1:
            finish(part, extra_refs, out_refs)
            return
        k = pl.program_id(2)
        if sum_in_out:
            @pl.when(k == 0)
            def _():
                first = part
                for r in extra_refs:
                    first = first + r[...]
                out_refs[0][...] = first

            @pl.when(k > 0)
            def _():
                out_refs[0][...] += part

            return
        acc_ref = refs[-1]

        @pl.when(k == 0)
        def _():
            acc_ref[...] = part

        @pl.when(k > 0)
        def _():
            acc_ref[...] += part

        @pl.when(k == nk - 1)
        def _():
            finish(acc_ref[...], extra_refs, out_refs)

    operands = [a, b] + [e for e, _ in extras]
    in_specs = [a_spec, b_spec] + [s for _, s in extras]
    aliases = {}
    if alias is not None:
        aliases = {len(operands): 0}
        operands.append(alias)
        in_specs.append(ANY)
    if after is not None:
        operands.append(after)
        in_specs.append(ANY)
    blocks = _nbytes((tm, tk), a.dtype) + _nbytes((tk, tn), b.dtype)
    blocks += sum(_nbytes((tm, tn), s.dtype) for s, _ in outs)
    blocks += sum(_nbytes((e.shape[-2] if e.shape[-2] < tm else tm, tn), e.dtype) for e, _ in extras)
    scratch = [] if nk == 1 or sum_in_out else [pltpu.VMEM((tm, tn), F32)]
    return pl.pallas_call(
        body, name=name, grid=grid,
        out_shape=[s for s, _ in outs], in_specs=in_specs, out_specs=[s for _, s in outs],
        input_output_aliases=aliases, scratch_shapes=scratch,
        compiler_params=pltpu.CompilerParams(
            dimension_semantics=("parallel", "parallel", "arbitrary"),
            vmem_limit_bytes=_vmem_limit(blocks, 0 if nk == 1 else 3 * tm * tn * 4)),
    )(*operands)


WEIGHT_GRAD_DEPTH = 4096


def _mm_tiles(m, n, k):
    return _tile(m, 1024), _tile(n, 1024), _tile(k, 2048)


def mm_plain(name, a, b, *, ta=False, tb=False, out_dtype=F32, epilogue=None, extras=(), extra_kinds=(),
             out_dtypes=None, after=None, row_tile=None):
    m = a.shape[1] if ta else a.shape[0]
    k = a.shape[0] if ta else a.shape[1]
    n = b.shape[0] if tb else b.shape[1]
    tm, tn, tk = _mm_tiles(m, n, k)
    if ta and not extras:
        tk = _tile(k, WEIGHT_GRAD_DEPTH)
    if row_tile is not None:
        tm = _tile(m, row_tile, SUBLANES)
    grid = (m // tm, n // tn, k // tk)
    a_spec = pl.BlockSpec((tk, tm), lambda i, j, kk: (kk, i)) if ta else pl.BlockSpec((tm, tk), lambda i, j, kk: (i, kk))
    b_spec = pl.BlockSpec((tn, tk), lambda i, j, kk: (j, kk)) if tb else pl.BlockSpec((tk, tn), lambda i, j, kk: (kk, j))
    ex = []
    for e, kind in zip(extras, extra_kinds):
        if kind == "row":
            ex.append((e, pl.BlockSpec((1, tn), lambda i, j, kk: (0, j))))
        else:
            ex.append((e, pl.BlockSpec((tm, tn), lambda i, j, kk: (i, j))))
    out_dtypes = out_dtypes or [out_dtype]
    outs = [(jax.ShapeDtypeStruct((m, n), dt), pl.BlockSpec((tm, tn), lambda i, j, kk: (i, j))) for dt in out_dtypes]
    res = _mm(name, a, b, ta=ta, tb=tb, grid=grid, tm=tm, tn=tn, tk=tk, a_spec=a_spec, b_spec=b_spec,
              outs=outs, extras=ex, epilogue=epilogue, after=after)
    return res if len(res) > 1 else res[0]


def mm_in_proj(name, h, w, bias, col0, parts):
    t, d = h.shape
    tm, tn, tk = _mm_tiles(t, d, d)
    per = d // tn
    off = col0 // tn
    grid = (t // tm, parts * per, d // tk)
    outs = [(jax.ShapeDtypeStruct((parts, t, d), F32),
             pl.BlockSpec((None, tm, tn), lambda i, j, kk: (j // per, i, j % per)))]
    ex = [(bias, pl.BlockSpec((1, tn), lambda i, j, kk: (0, j + off)))]
    return _mm(name, h, w, ta=False, tb=False, grid=grid, tm=tm, tn=tn, tk=tk,
               a_spec=pl.BlockSpec((tm, tk), lambda i, j, kk: (i, kk)),
               b_spec=pl.BlockSpec((tk, tn), lambda i, j, kk: (kk, j + off)),
               outs=outs, extras=ex, epilogue=lambda acc, bb: (acc + bb,))[0]


def mm_in_proj_dw(name, h, dz, col0, n_total, prev=None, after=None):
    parts, t, d = dz.shape
    tm, tn, _ = _mm_tiles(d, d, t)
    tk = _tile(t, WEIGHT_GRAD_DEPTH)
    per = d // tn
    off = col0 // tn
    grid = (d // tm, parts * per, t // tk)
    outs = [(jax.ShapeDtypeStruct((d, n_total), BF16), pl.BlockSpec((tm, tn), lambda i, j, kk: (i, j + off)))]
    return _mm(name, h, dz, ta=True, tb=False, grid=grid, tm=tm, tn=tn, tk=tk,
               a_spec=pl.BlockSpec((tk, tm), lambda i, j, kk: (kk, i)),
               b_spec=pl.BlockSpec((None, tk, tn), lambda i, j, kk: (j // per, kk, j % per)),
               outs=outs, alias=prev, after=after)[0]


def mm_in_proj_dh(name, dz, w, col0, res=None, after=None):
    parts, t, d = dz.shape
    tm, tn, tk = _mm_tiles(t, d, d)
    per = d // tk
    off = col0 // tk
    grid = (t // tm, d // tn, parts * per)
    outs = [(jax.ShapeDtypeStruct((t, d), F32), pl.BlockSpec((tm, tn), lambda i, j, kk: (i, j)))]
    ex, epi = [], None
    if res is not None:
        ex = [(res, pl.BlockSpec((tm, tn), lambda i, j, kk: (i, j)))]
        epi = lambda acc, r: (acc + r,)
    return _mm(name, dz, w, ta=False, tb=True, grid=grid, tm=tm, tn=tn, tk=tk,
               a_spec=pl.BlockSpec((None, tm, tk), lambda i, j, kk: (kk // per, i, kk % per)),
               b_spec=pl.BlockSpec((tn, tk), lambda i, j, kk: (j, kk + off)),
               outs=outs, extras=ex, epilogue=epi, after=after, sum_in_out=True)[0]


ROW_TILE = 256


def rmsnorm_fwd(name, x, g, after=None):
    t, d = x.shape
    tr = _tile(t, ROW_TILE, SUBLANES)
    order = [] if after is None else [after]

    def body(x_ref, g_ref, *rest):
        h_ref = rest[-1]
        xv = x_ref[...]
        rstd = lax.rsqrt(jnp.mean(xv * xv, axis=-1, keepdims=True) + EPS)
        h_ref[...] = (xv * rstd * g_ref[...]).astype(BF16)

    return pl.pallas_call(
        body, name=name, grid=(t // tr,),
        out_shape=jax.ShapeDtypeStruct((t, d), BF16),
        in_specs=[pl.BlockSpec((tr, d), lambda i: (i, 0)), pl.BlockSpec((1, d), lambda i: (0, 0))] + [ANY] * len(order),
        out_specs=pl.BlockSpec((tr, d), lambda i: (i, 0)),
        compiler_params=pltpu.CompilerParams(dimension_semantics=("parallel",)),
    )(x, g, *order)


def _rmsnorm_bwd_math(xv, gv, dout):
    rstd = lax.rsqrt(jnp.mean(xv * xv, axis=-1, keepdims=True) + EPS)
    xhat = xv * rstd
    dy = dout * gv
    dx = rstd * (dy - xhat * jnp.mean(dy * xhat, axis=-1, keepdims=True))
    dg = jnp.sum(dout * xhat, axis=0, keepdims=True)
    return dx, dg


def rmsnorm_bwd(name, dh, x, g, dres):
    t, d = x.shape
    tr = _tile(t, ROW_TILE, SUBLANES)

    def body(dh_ref, x_ref, g_ref, dres_ref, dx_ref, dxb_ref, dg_ref):
        dx, dg = _rmsnorm_bwd_math(x_ref[...], g_ref[...], dh_ref[...])
        dx = dx + dres_ref[...]
        dx_ref[...] = dx
        dxb_ref[...] = dx.astype(BF16)

        @pl.when(pl.program_id(0) == 0)
        def _():
            dg_ref[...] = dg

        @pl.when(pl.program_id(0) > 0)
        def _():
            dg_ref[...] += dg

    row = pl.BlockSpec((tr, d), lambda i: (i, 0))
    vec = pl.BlockSpec((1, d), lambda i: (0, 0))
    return pl.pallas_call(
        body, name=name, grid=(t // tr,),
        out_shape=[jax.ShapeDtypeStruct((t, d), F32), jax.ShapeDtypeStruct((t, d), BF16),
                   jax.ShapeDtypeStruct((1, d), F32)],
        in_specs=[row, row, vec, row], out_specs=[row, row, vec],
        compiler_params=pltpu.CompilerParams(dimension_semantics=("arbitrary",)),
    )(dh, x, g, dres)


def loss_head(name, x, g, target):
    t, d = x.shape
    tr = _tile(t, ROW_TILE, SUBLANES)

    def body(x_ref, g_ref, tgt_ref, dx_ref, dxb_ref, dg_ref, loss_ref):
        xv, gv = x_ref[...], g_ref[...]
        rstd = lax.rsqrt(jnp.mean(xv * xv, axis=-1, keepdims=True) + EPS)
        err = xv * rstd * gv - tgt_ref[...]
        lsum = (0.5 / d) * jnp.sum(err * err, axis=0, keepdims=True)
        dx, dg = _rmsnorm_bwd_math(xv, gv, err * (1.0 / d))
        dx_ref[...] = dx
        dxb_ref[...] = dx.astype(BF16)

        @pl.when(pl.program_id(0) == 0)
        def _():
            dg_ref[...] = dg
            loss_ref[...] = lsum

        @pl.when(pl.program_id(0) > 0)
        def _():
            dg_ref[...] += dg
            loss_ref[...] += lsum

    row = pl.BlockSpec((tr, d), lambda i: (i, 0))
    vec = pl.BlockSpec((1, d), lambda i: (0, 0))
    return pl.pallas_call(
        body, name=name, grid=(t // tr,),
        out_shape=[jax.ShapeDtypeStruct((t, d), F32), jax.ShapeDtypeStruct((t, d), BF16),
                   jax.ShapeDtypeStruct((1, d), F32), jax.ShapeDtypeStruct((1, d), F32)],
        in_specs=[row, vec, row], out_specs=[row, row, vec, vec],
        compiler_params=pltpu.CompilerParams(dimension_semantics=("arbitrary",)),
    )(x, g, target)


TIME_TILE = 512


def _sigmoid(v):
    return 1.0 / (1.0 + jnp.exp(-v))


def _gelu(v):
    return 0.5 * v * (1.0 + jnp.tanh(GELU_C * (v + GELU_A * v * v * v)))


def _gelu_grad(v):
    th = jnp.tanh(GELU_C * (v + GELU_A * v * v * v))
    return 0.5 * (1.0 + th) + 0.5 * v * (1.0 - th * th) * GELU_C * (1.0 + 3.0 * GELU_A * v * v)


def _one_minus_exp(v):
    series = -v * (1.0 + v * (0.5 + v * (1.0 / 6.0 + v * (1.0 / 24.0))))
    return jnp.where(v > -0.05, series, 1.0 - jnp.exp(v))


def _roll_in_groups(v, s):
    tt, cb = v.shape
    return pltpu.roll(v.reshape(tt // SUBLANES, SUBLANES, cb), s, 1).reshape(tt, cb)


def _back(cur, halo, s, row):
    tt, cb = cur.shape
    turned = _roll_in_groups(cur, s)
    group_before = jnp.concatenate([pltpu.roll(halo, s, 0), turned[:tt - SUBLANES]], axis=0)
    return jnp.where((row & (SUBLANES - 1)) < s, group_before, turned)


def _ahead(cur, halo, s, row):
    tt, cb = cur.shape
    turned = _roll_in_groups(cur, SUBLANES - s)
    group_after = jnp.concatenate([turned[SUBLANES:], pltpu.roll(halo, SUBLANES - s, 0)], axis=0)
    return jnp.where((row & (SUBLANES - 1)) >= SUBLANES - s, group_after, turned)


def _lru_gates(u, wr, wi, br, bi, lam):
    ub = u.astype(BF16)
    r = _sigmoid(jnp.dot(ub, wr, preferred_element_type=F32) + br)
    i = _sigmoid(jnp.dot(ub, wi, preferred_element_type=F32) + bi)
    sp = jnp.maximum(-lam, 0.0) + jnp.log1p(jnp.exp(-jnp.abs(lam)))
    log_a = -LRU_C * r * sp
    a = jnp.exp(log_a)
    mult = jnp.sqrt(_one_minus_exp(2.0 * log_a))
    return ub, r, i, a, mult


def _branch_specs(tt, cb, nt, time_of):
    per = tt // SUBLANES

    def tile(parts):
        if parts is None:
            return pl.BlockSpec((tt, cb), lambda n, s: (time_of(s), n))
        return pl.BlockSpec((parts, tt, cb), lambda n, s: (0, time_of(s), n))

    def halo_before(parts):
        if parts is None:
            return pl.BlockSpec((SUBLANES, cb), lambda n, s: (jnp.maximum(time_of(s) * per - 1, 0), n))
        return pl.BlockSpec((parts, SUBLANES, cb), lambda n, s: (0, jnp.maximum(time_of(s) * per - 1, 0), n))

    def rows(k):
        return pl.BlockSpec((k, cb), lambda n, s: (0, n))

    gate_w = pl.BlockSpec((None, cb, cb), lambda n, s: (n, 0, 0))
    return tile, halo_before, rows, gate_w


def branch_fwd(name, z5, caw, cab, cbw, wr, wi, br, bi, lam):
    _, t, d = z5.shape
    cb = wr.shape[-1]
    tt = _tile(t, TIME_TILE, SUBLANES)
    nt = t // tt
    tile, halo_before, rows, gate_w = _branch_specs(tt, cb, nt, lambda s: s)

    def body(z_ref, zh_ref, caw_ref, cab_ref, cbw_ref, wr_ref, wi_ref, br_ref, bi_ref, lam_ref,
             pa_ref, pb_ref, h_ref, carry_ref, acum_ref, bcum_ref):
        s = pl.program_id(1)
        row = lax.broadcasted_iota(jnp.int32, (tt, cb), 0)
        keep = jnp.where(s > 0, 1.0, 0.0)

        @pl.when(s == 0)
        def _():
            carry_ref[...] = jnp.zeros_like(carry_ref)

        xa, ya, cbv, ccv, cxv = (z_ref[p] for p in range(N_BRANCH))
        xa_h = zh_ref[0] * keep
        caw_v = caw_ref[...]
        u = caw_v[3:4] * xa + cab_ref[...]
        for j in range(1, CONV_A_WIDTH):
            u = u + caw_v[3 - j:4 - j] * _back(xa, xa_h, j, row)
        _, _, gi, a, mult = _lru_gates(u, wr_ref[...], wi_ref[...], br_ref[...], bi_ref[...], lam_ref[...])
        b = mult * (gi * u)

        rm = row & (SUBLANES - 1)
        for sh in (1, 2, 4):
            a_prev = jnp.where(rm >= sh, _roll_in_groups(a, sh), 1.0)
            b_prev = jnp.where(rm >= sh, _roll_in_groups(b, sh), 0.0)
            b = a * b_prev + b
            a = a * a_prev
        acum_ref[...] = a
        bcum_ref[...] = b

        def group(g, h_in):
            r0 = pl.multiple_of(g * SUBLANES, SUBLANES)
            hg = acum_ref[pl.ds(r0, SUBLANES), :] * h_in + bcum_ref[pl.ds(r0, SUBLANES), :]
            h_ref[pl.ds(r0, SUBLANES), :] = hg
            return jnp.broadcast_to(hg[SUBLANES - 1:SUBLANES, :], (SUBLANES, cb))

        carry_ref[...] = lax.fori_loop(0, tt // SUBLANES, group, carry_ref[...], unroll=4)
        pa_ref[...] = (h_ref[...] * _gelu(ya)).astype(BF16)

        q = ccv * cxv
        q_h = zh_ref[3] * zh_ref[4] * keep
        cbw_v = cbw_ref[...]
        v = cbw_v[2:3] * q
        for j in range(1, CONV_B_WIDTH):
            v = v + cbw_v[2 - j:3 - j] * _back(q, q_h, j, row)
        pb_ref[...] = (cbv * v).astype(BF16)

    act = jax.ShapeDtypeStruct((t, d), BF16)
    blocks = _nbytes((2 * N_BRANCH, tt, cb), F32) + _nbytes((2, cb, cb), BF16) + _nbytes((4, tt, cb), F32)
    return pl.pallas_call(
        body, name=name, grid=(d // cb, nt),
        out_shape=[act, act, jax.ShapeDtypeStruct((t, d), F32)],
        in_specs=[tile(N_BRANCH), halo_before(N_BRANCH), rows(CONV_A_WIDTH), rows(1), rows(CONV_B_WIDTH),
                  gate_w, gate_w, rows(1), rows(1), rows(1)],
        out_specs=[tile(None), tile(None), tile(None)],
        scratch_shapes=[pltpu.VMEM((SUBLANES, cb), F32), pltpu.VMEM((tt, cb), F32), pltpu.VMEM((tt, cb), F32)],
        compiler_params=pltpu.CompilerParams(dimension_semantics=("parallel", "arbitrary"),
                                             vmem_limit_bytes=_vmem_limit(blocks, 40 * tt * cb * 4)),
    )(z5, z5, caw, cab, cbw, wr, wi, br, bi, lam)


def branch_bwd(name, z5, hl, dpa, dpb, caw, cab, cbw, wr, wi, br, bi, lam):
    _, t, d = z5.shape
    cb = wr.shape[-1]
    tt = _tile(t, TIME_TILE, SUBLANES)
    nt = t // tt
    tile, halo_before, rows, gate_w = _branch_specs(tt, cb, nt, lambda s: nt - 1 - s)

    def body(z_ref, zh_ref, h_ref, hh_ref, dpa_ref, dpb_ref, caw_ref, cab_ref, cbw_ref, wr_ref, wi_ref,
             br_ref, bi_ref, lam_ref,
             dz_ref, dbias_ref, dcaw_ref, dcab_ref, dcbw_ref, dwr_ref, dwi_ref, dbr_ref, dbi_ref, dlam_ref,
             g_carry_ref, a_next_ref, du_next_ref, dv_next_ref, acum_ref, bcum_ref, gout_ref):
        s = pl.program_id(1)
        row = lax.broadcasted_iota(jnp.int32, (tt, cb), 0)
        keep = jnp.where(s < nt - 1, 1.0, 0.0)

        @pl.when(s == 0)
        def _():
            for r in (g_carry_ref, a_next_ref, du_next_ref, dv_next_ref, dbias_ref, dcaw_ref, dcab_ref, dcbw_ref,
                      dwr_ref, dwi_ref, dbr_ref, dbi_ref, dlam_ref):
                r[...] = jnp.zeros_like(r)

        xa, ya, cbv, ccv, cxv = (z_ref[p] for p in range(N_BRANCH))
        xa_h = zh_ref[0] * keep
        caw_v = caw_ref[...]
        wr_v, wi_v, lam_v = wr_ref[...], wi_ref[...], lam_ref[...]
        xa_back = [xa] + [_back(xa, xa_h, j, row) for j in range(1, CONV_A_WIDTH)]
        u = caw_v[3:4] * xa + cab_ref[...]
        for j in range(1, CONV_A_WIDTH):
            u = u + caw_v[3 - j:4 - j] * xa_back[j]
        ub, gr, gi, a, mult = _lru_gates(u, wr_v, wi_v, br_ref[...], bi_ref[...], lam_v)

        hv = h_ref[...]
        dpa_v = dpa_ref[...]
        dya = dpa_v * hv * _gelu_grad(ya)
        dh = dpa_v * _gelu(ya)

        a_up = _ahead(a, a_next_ref[...], 1, row)
        rm = row & (SUBLANES - 1)
        ca, cg = a_up, dh
        for sh in (1, 2, 4):
            a_nxt = jnp.where(rm + sh < SUBLANES, _roll_in_groups(ca, SUBLANES - sh), 1.0)
            g_nxt = jnp.where(rm + sh < SUBLANES, _roll_in_groups(cg, SUBLANES - sh), 0.0)
            cg = ca * g_nxt + cg
            ca = ca * a_nxt
        acum_ref[...] = ca
        bcum_ref[...] = cg
        n_groups = tt // SUBLANES

        def group(k, g_in):
            r0 = pl.multiple_of((n_groups - 1 - k) * SUBLANES, SUBLANES)
            gg = acum_ref[pl.ds(r0, SUBLANES), :] * g_in + bcum_ref[pl.ds(r0, SUBLANES), :]
            gout_ref[pl.ds(r0, SUBLANES), :] = gg
            return jnp.broadcast_to(gg[0:1, :], (SUBLANES, cb))

        g_carry_ref[...] = lax.fori_loop(0, n_groups, group, g_carry_ref[...], unroll=4)
        a_next_ref[...] = a[0:SUBLANES, :]
        gv = gout_ref[...]

        h_prev = _back(hv, hh_ref[...] * keep, 1, row)
        da = gv * h_prev
        iu = gi * u
        dmult = gv * iu
        di = gv * mult * u
        du = gv * mult * gi
        dla = da * a - dmult * (a * a) / mult
        sp = jnp.maximum(-lam_v, 0.0) + jnp.log1p(jnp.exp(-jnp.abs(lam_v)))
        dr = dla * (-LRU_C * sp)
        dlam_ref[...] += jnp.sum(dla * gr, axis=0, keepdims=True) * (LRU_C * _sigmoid(-lam_v))
        drp = dr * gr * (1.0 - gr)
        dip = di * gi * (1.0 - gi)
        dbr_ref[...] += jnp.sum(drp, axis=0, keepdims=True)
        dbi_ref[...] += jnp.sum(dip, axis=0, keepdims=True)
        drp_b, dip_b = drp.astype(BF16), dip.astype(BF16)
        tn_dims = (((0,), (0,)), ((), ()))
        nt_dims = (((1,), (1,)), ((), ()))
        dwr_ref[...] += lax.dot_general(ub, drp_b, tn_dims, preferred_element_type=F32)
        dwi_ref[...] += lax.dot_general(ub, dip_b, tn_dims, preferred_element_type=F32)
        du = du + lax.dot_general(drp_b, wr_v, nt_dims, preferred_element_type=F32)
        du = du + lax.dot_general(dip_b, wi_v, nt_dims, preferred_element_type=F32)

        du_h = du_next_ref[...]
        dxa = caw_v[3:4] * du
        dcaw_ref[3:4, :] += jnp.sum(du * xa, axis=0, keepdims=True)
        for j in range(1, CONV_A_WIDTH):
            dxa = dxa + caw_v[3 - j:4 - j] * _ahead(du, du_h, j, row)
            dcaw_ref[3 - j:4 - j, :] += jnp.sum(du * xa_back[j], axis=0, keepdims=True)
        dcab_ref[...] += jnp.sum(du, axis=0, keepdims=True)
        du_next_ref[...] = du[0:SUBLANES, :]

        q = ccv * cxv
        q_h = zh_ref[3] * zh_ref[4] * keep
        cbw_v = cbw_ref[...]
        q_back = [q] + [_back(q, q_h, j, row) for j in range(1, CONV_B_WIDTH)]
        v = cbw_v[2:3] * q
        for j in range(1, CONV_B_WIDTH):
            v = v + cbw_v[2 - j:3 - j] * q_back[j]
        dpb_v = dpb_ref[...]
        dcb = dpb_v * v
        dv = dpb_v * cbv
        dv_h = dv_next_ref[...]
        dq = cbw_v[2:3] * dv
        dcbw_ref[2:3, :] += jnp.sum(dv * q, axis=0, keepdims=True)
        for j in range(1, CONV_B_WIDTH):
            dq = dq + cbw_v[2 - j:3 - j] * _ahead(dv, dv_h, j, row)
            dcbw_ref[2 - j:3 - j, :] += jnp.sum(dv * q_back[j], axis=0, keepdims=True)
        dv_next_ref[...] = dv[0:SUBLANES, :]
        dcc = dq * cxv
        dcx = dq * ccv

        for p, val in enumerate((dxa, dya, dcb, dcc, dcx)):
            dz_ref[p] = val.astype(BF16)
            dbias_ref[p:p + 1, :] += jnp.sum(val, axis=0, keepdims=True)

    def acc_rows(k):
        return jax.ShapeDtypeStruct((k, d), F32)

    nb = d // cb
    gate_grad = jax.ShapeDtypeStruct((nb, cb, cb), F32)
    small = pltpu.VMEM((SUBLANES, cb), F32)
    big = pltpu.VMEM((tt, cb), F32)
    blocks = _nbytes((2 * (N_BRANCH + 3), tt, cb), F32) + _nbytes((4, cb, cb), F32)
    return pl.pallas_call(
        body, name=name, grid=(nb, nt),
        out_shape=[jax.ShapeDtypeStruct((N_BRANCH, t, d), BF16), acc_rows(N_BRANCH), acc_rows(CONV_A_WIDTH),
                   acc_rows(1), acc_rows(CONV_B_WIDTH), gate_grad, gate_grad, acc_rows(1), acc_rows(1), acc_rows(1)],
        in_specs=[tile(N_BRANCH), halo_before(N_BRANCH), tile(None), halo_before(None), tile(None), tile(None),
                  rows(CONV_A_WIDTH), rows(1), rows(CONV_B_WIDTH), gate_w, gate_w, rows(1), rows(1), rows(1)],
        out_specs=[tile(N_BRANCH), rows(N_BRANCH), rows(CONV_A_WIDTH), rows(1), rows(CONV_B_WIDTH),
                   gate_w, gate_w, rows(1), rows(1), rows(1)],
        scratch_shapes=[small, small, small, small, big, big, big],
        compiler_params=pltpu.CompilerParams(dimension_semantics=("parallel", "arbitrary"),
                                             vmem_limit_bytes=_vmem_limit(blocks, 60 * tt * cb * 4)),
    )(z5, z5, hl, hl, dpa, dpb, caw, cab, cbw, wr, wi, br, bi, lam)


MERGE_MM_ROWS, MERGE_MM_COLS = 1024, 512


def proj_ab_merge(name, pa, pb, w_pa, w_pb, z2):
    t, k = pa.shape
    n = w_pa.shape[1]
    tm, tn = _tile(t, MERGE_MM_ROWS, SUBLANES), _tile(n, MERGE_MM_COLS)

    def body(pa_ref, pb_ref, wa_ref, wb_ref, z_ref, oa_ref, ob_ref, mg_ref):
        oa = jnp.dot(pa_ref[...], wa_ref[...], preferred_element_type=F32)
        ob = jnp.dot(pb_ref[...], wb_ref[...], preferred_element_type=F32)
        oa_ref[...] = oa
        ob_ref[...] = ob
        mg_ref[...] = (_sigmoid(z_ref[0]) * oa + _sigmoid(z_ref[1]) * ob).astype(BF16)

    act = pl.BlockSpec((tm, k), lambda i, j: (i, 0))
    wgt = pl.BlockSpec((k, tn), lambda i, j: (0, j))
    blk = pl.BlockSpec((tm, tn), lambda i, j: (i, j))
    out = jax.ShapeDtypeStruct((t, n), F32)
    blocks = 2 * _nbytes((tm, k), BF16) + 2 * _nbytes((k, tn), BF16) + 4 * _nbytes((tm, tn), F32) + _nbytes((tm, tn), BF16)
    return pl.pallas_call(
        body, name=name, grid=(t // tm, n // tn),
        out_shape=[out, out, jax.ShapeDtypeStruct((t, n), BF16)],
        in_specs=[act, act, wgt, wgt, pl.BlockSpec((2, tm, tn), lambda i, j: (0, i, j))],
        out_specs=[blk, blk, blk],
        compiler_params=pltpu.CompilerParams(dimension_semantics=("parallel", "parallel"),
                                             vmem_limit_bytes=_vmem_limit(blocks)),
    )(pa, pb, w_pa, w_pb, z2)


def proj_o_dx_merge_bwd(name, dx, w_o, z2, oa, ob):
    t, k = dx.shape
    n = w_o.shape[0]
    tm, tn = _tile(t, MERGE_MM_ROWS, SUBLANES), _tile(n, MERGE_MM_COLS)

    def body(dx_ref, w_ref, z_ref, oa_ref, ob_ref, doa_ref, dob_ref, dz_ref, sums_ref):
        dm = lax.dot_general(dx_ref[...], w_ref[...], (((1,), (1,)), ((), ())), preferred_element_type=F32)
        sa, sb = _sigmoid(z_ref[0]), _sigmoid(z_ref[1])
        doa_ref[...] = (dm * sa).astype(BF16)
        dob_ref[...] = (dm * sb).astype(BF16)
        dga = dm * oa_ref[...] * sa * (1.0 - sa)
        dgb = dm * ob_ref[...] * sb * (1.0 - sb)
        dz_ref[0] = dga.astype(BF16)
        dz_ref[1] = dgb.astype(BF16)
        sums_ref[0:1, :] = jnp.sum(dga, axis=0, keepdims=True)
        sums_ref[1:2, :] = jnp.sum(dgb, axis=0, keepdims=True)

    blk = pl.BlockSpec((tm, tn), lambda i, j: (i, j))
    pair = pl.BlockSpec((2, tm, tn), lambda i, j: (0, i, j))
    act = jax.ShapeDtypeStruct((t, n), BF16)
    blocks = _nbytes((tm, k), BF16) + _nbytes((tn, k), BF16) + 4 * _nbytes((tm, tn), F32) + 4 * _nbytes((tm, tn), BF16)
    return pl.pallas_call(
        body, name=name, grid=(t // tm, n // tn),
        out_shape=[act, act, jax.ShapeDtypeStruct((2, t, n), BF16), jax.ShapeDtypeStruct((t // tm, 2, n), F32)],
        in_specs=[pl.BlockSpec((tm, k), lambda i, j: (i, 0)), pl.BlockSpec((tn, k), lambda i, j: (j, 0)), pair, blk, blk],
        out_specs=[blk, blk, pair, pl.BlockSpec((None, 2, tn), lambda i, j: (i, 0, j))],
        compiler_params=pltpu.CompilerParams(dimension_semantics=("parallel", "parallel"),
                                             vmem_limit_bytes=_vmem_limit(blocks)),
    )(dx, w_o, z2, oa, ob)


def _adamw_math(w, g, m, v):
    m = ADAM_B1 * m + (1.0 - ADAM_B1) * g
    v = ADAM_B2 * v + (1.0 - ADAM_B2) * (g * g)
    m_hat = m / (1.0 - ADAM_B1 ** ADAM_STEP)
    v_hat = v / (1.0 - ADAM_B2 ** ADAM_STEP)
    delta = -ADAM_LR * (m_hat / (jnp.sqrt(v_hat) + ADAM_EPS) + ADAM_WD * w)
    return delta, m, v


def _sum_in_order(ref):
    g = ref[0].astype(F32)
    for p in range(1, ref.shape[0]):
        g = g + ref[p].astype(F32)
    return g


def adamw_sharded(name, recv, own, chip, w, m, v):
    r, c = w.shape
    rows = r // DEPTH
    tr = _tile(rows, max(SUBLANES, (256 * 1024) // c), 16)
    per = rows // tr

    def body(chip_ref, *refs):
        recv_refs, own_refs = refs[:DEPTH], refs[DEPTH:2 * DEPTH]
        w_ref, m_ref, v_ref, g_ref, d_ref, mo_ref, vo_ref = refs[2 * DEPTH:]
        step = pl.program_id(0)
        for l in range(DEPTH):
            @pl.when((step >= l * per) & (step < (l + 1) * per))
            def _(l=l):
                g = None
                for q in range(N_CHIP):
                    part = jnp.where(chip_ref[0] == q, own_refs[l][...], recv_refs[l][q]).astype(F32)
                    g = part if g is None else g + part
                delta, m2, v2 = _adamw_math(w_ref[...], g, m_ref[...], v_ref[...])
                g_ref[...] = g
                d_ref[...] = delta
                mo_ref[...] = m2
                vo_ref[...] = v2

    def in_layer(l, i):
        return jnp.clip(i - l * per, 0, per - 1)

    recv_specs = [pl.BlockSpec((N_CHIP, tr, c), lambda i, chip_ref, l=l: (0, in_layer(l, i), 0)) for l in range(DEPTH)]
    own_specs = [pl.BlockSpec((None, tr, c), lambda i, chip_ref, l=l: (chip_ref[0], in_layer(l, i), 0))
                 for l in range(DEPTH)]
    blk = pl.BlockSpec((tr, c), lambda i, chip_ref: (i, 0))
    out = jax.ShapeDtypeStruct((r, c), F32)
    blocks = DEPTH * _nbytes((N_CHIP + 1, tr, c), BF16) + 7 * _nbytes((tr, c), F32)
    return pl.pallas_call(
        body, name=name, out_shape=[out, out, out, out],
        grid_spec=pltpu.PrefetchScalarGridSpec(
            num_scalar_prefetch=1, grid=(DEPTH * per,), in_specs=recv_specs + own_specs + [blk, blk, blk],
            out_specs=[blk, blk, blk, blk]),
        compiler_params=pltpu.CompilerParams(dimension_semantics=("parallel",),
                                             vmem_limit_bytes=_vmem_limit(blocks)),
    )(chip, *recv, *own, w, m, v)


def sum_partials(name, parts):
    _, r, c = parts.shape

    def body(p_ref, o_ref):
        o_ref[...] = _sum_in_order(p_ref)

    return pl.pallas_call(body, name=name, out_shape=jax.ShapeDtypeStruct((r, c), F32))(parts)


def adamw_small(name, g, w, m, v):
    def body(g_ref, w_ref, m_ref, v_ref, d_ref, mo_ref, vo_ref):
        delta, m2, v2 = _adamw_math(w_ref[...], g_ref[...], m_ref[...], v_ref[...])
        d_ref[...] = delta
        mo_ref[...] = m2
        vo_ref[...] = v2

    out = jax.ShapeDtypeStruct(w.shape, F32)
    return pl.pallas_call(body, name=name, out_shape=[out, out, out])(g, w, m, v)


def _pack(arrays):
    flat = jnp.concatenate([a.reshape(-1) for a in arrays])
    n = flat.shape[0]
    rows = -(-n // (SUBLANES * LANES)) * SUBLANES
    flat = jnp.pad(flat, (0, rows * LANES - n))
    return flat.reshape(rows, LANES)


def _unpack(packed, shapes):
    flat = packed.reshape(-1)
    out, off = [], 0
    for s in shapes:
        n = math.prod(s)
        out.append(flat[off:off + n].reshape(s))
        off += n
    return out


def kernel(x, norm1_g, w_in, b_in, conv_a_w, conv_a_b, lru_wr, lru_br, lru_wi, lru_bi, lru_lam, conv_b_w, w_pa, w_pb, w_o, norm2_g, w_mlp1, w_mlp2, final_g, loss_target, m_norm1_g, m_w_in, m_b_in, m_conv_a_w, m_conv_a_b, m_lru_wr, m_lru_br, m_lru_wi, m_lru_bi, m_lru_lam, m_conv_b_w, m_w_pa, m_w_pb, m_w_o, m_norm2_g, m_w_mlp1, m_w_mlp2, m_final_g, v_norm1_g, v_w_in, v_b_in, v_conv_a_w, v_conv_a_b, v_lru_wr, v_lru_br, v_lru_wi, v_lru_bi, v_lru_lam, v_conv_b_w, v_w_pa, v_w_pb, v_w_o, v_norm2_g, v_w_mlp1, v_w_mlp2, v_final_g):
    weights = dict(norm1_g=norm1_g, w_in=w_in, b_in=b_in, conv_a_w=conv_a_w, conv_a_b=conv_a_b, lru_wr=lru_wr,
                   lru_br=lru_br, lru_wi=lru_wi, lru_bi=lru_bi, lru_lam=lru_lam, conv_b_w=conv_b_w, w_pa=w_pa,
                   w_pb=w_pb, w_o=w_o, norm2_g=norm2_g, w_mlp1=w_mlp1, w_mlp2=w_mlp2, final_g=final_g)
    mom1 = dict(norm1_g=m_norm1_g, w_in=m_w_in, b_in=m_b_in, conv_a_w=m_conv_a_w, conv_a_b=m_conv_a_b,
                lru_wr=m_lru_wr, lru_br=m_lru_br, lru_wi=m_lru_wi, lru_bi=m_lru_bi, lru_lam=m_lru_lam,
                conv_b_w=m_conv_b_w, w_pa=m_w_pa, w_pb=m_w_pb, w_o=m_w_o, norm2_g=m_norm2_g, w_mlp1=m_w_mlp1,
                w_mlp2=m_w_mlp2, final_g=m_final_g)
    mom2 = dict(norm1_g=v_norm1_g, w_in=v_w_in, b_in=v_b_in, conv_a_w=v_conv_a_w, conv_a_b=v_conv_a_b,
                lru_wr=v_lru_wr, lru_br=v_lru_br, lru_wi=v_lru_wi, lru_bi=v_lru_bi, lru_lam=v_lru_lam,
                conv_b_w=v_conv_b_w, w_pa=v_w_pa, w_pb=v_w_pb, w_o=v_w_o, norm2_g=v_norm2_g, w_mlp1=v_w_mlp1,
                w_mlp2=v_w_mlp2, final_g=v_final_g)
    names = list(weights)

    t, d = x.shape[1], x.shape[2]
    n_in = b_in.shape[1]
    nb, bw = lru_wr.shape[1], lru_wr.shape[3]
    mx, my, mc = _my_coords()
    me = _flat(mx, my, mc)
    me_index = me.astype(jnp.int32).reshape(1)
    my_core = mc.astype(jnp.int32).reshape(1)
    my_chip = (2 * mx + my).astype(jnp.int32).reshape(1)
    x0 = x.reshape(t, d)
    target = loss_target.reshape(t, d)

    big = ["w_in", "w_pa", "w_pb", "w_o", "w_mlp1", "w_mlp2", "lru_wr", "lru_wi"]
    big_axis = dict(w_in=1, w_pa=0, w_pb=0, w_o=0, w_mlp1=1, w_mlp2=0, lru_wr=1, lru_wi=1)
    small_sharded = ["conv_a_w", "conv_b_w", "lru_br", "lru_bi"]
    small_packed = _pack([weights[k] for k in small_sharded])
    small_rows = all_gather("gather_small_params", [small_packed[None]], [0])[0].reshape(N_DEV, -1)
    groups = (("w_in", "lru_wr", "lru_wi"), ("w_pa", "w_pb", "w_o"), ("w_mlp1", "w_mlp2"))
    gathers, full = {}, [{} for _ in range(DEPTH)]
    token = small_rows
    for l in range(DEPTH):
        for gi, group in enumerate(groups):
            axes = [big_axis[k] for k in group]
            lands = [place_shard(f"place_{k}_l{l}", weights[k], l, big_axis[k], me_index) for k in group]
            own_sems, flying, token = gather_start(f"gather_start_l{l}g{gi}", lands, axes, after=token)
            gathers[l, gi] = dict(axes=axes, own=own_sems, flying=flying)

    def pass_on(l, gi, after):
        st = gathers[l, gi]
        st["passed"], st["flying"] = gather_pass_on(f"gather_pass_on_l{l}g{gi}", st["flying"], st["axes"], st["own"], after)

    def finish(l, gi, after):
        st = gathers[l, gi]
        out = gather_finish(f"gather_finish_l{l}g{gi}", st["flying"], st["axes"], st["own"], st["passed"], after)
        full[l].update(zip(groups[gi], out))

    pass_on(0, 0, token)
    finish(0, 0, token)
    small_full = {}
    off = 0
    for k in small_sharded:
        shard_shape = weights[k].shape
        n = math.prod(shard_shape)
        g = small_rows[:, off:off + n].reshape(N_DEV, *shard_shape)
        small_full[k] = jnp.moveaxis(g, 0, 2).reshape(shard_shape[0], shard_shape[1], N_DEV * shard_shape[2])
        off += n

    def layer_params(l):
        return dict(
            g1=norm1_g[l][None], b_in=b_in[l][None], caw=small_full["conv_a_w"][l], cab=conv_a_b[l][None],
            br=small_full["lru_br"][l].reshape(1, d), bi=small_full["lru_bi"][l].reshape(1, d),
            lam=lru_lam[l][None], cbw=small_full["conv_b_w"][l], g2=norm2_g[l][None])

    params = [layer_params(l) for l in range(DEPTH)]

    relu2 = lambda acc: (acc, jnp.square(jnp.maximum(acc, 0.0)))
    add_res = lambda acc, r: (acc + r,)
    relu2_bwd = lambda acc, pre: (acc * (2.0 * jnp.maximum(pre, 0.0)),)

    saved = []
    xl = x0
    for l in range(DEPTH):
        p, w = params[l], full[l]
        tag = f"l{l}"
        h1 = rmsnorm_fwd(f"norm1_{tag}", xl, p["g1"], after=token if l == 0 else None)
        z5 = mm_in_proj(f"in_proj5_{tag}", h1, w["w_in"], p["b_in"], 0, N_BRANCH)
        z2 = mm_in_proj(f"in_proj2_{tag}", h1, w["w_in"], p["b_in"], N_BRANCH * d, N_SPLIT - N_BRANCH)
        pass_on(l, 1, z2)
        pa, pb, hl = branch_fwd(f"branch_fwd_{tag}", z5, p["caw"], p["cab"], p["cbw"], w["lru_wr"], w["lru_wi"],
                                p["br"], p["bi"], p["lam"])
        finish(l, 1, pa)
        oa, ob, mg = proj_ab_merge(f"proj_ab_merge_{tag}", pa, pb, w["w_pa"], w["w_pb"], z2)
        pass_on(l, 2, mg)
        x1 = mm_plain(f"proj_o_{tag}", mg, w["w_o"], epilogue=add_res, extras=(xl,), extra_kinds=("full",),
                      row_tile=512)
        h2 = rmsnorm_fwd(f"norm2_{tag}", x1, p["g2"])
        finish(l, 2, h2)
        pre, uu = mm_plain(f"mlp1_{tag}", h2, w["w_mlp1"], epilogue=relu2, out_dtypes=[F32, BF16])
        if l + 1 < DEPTH:
            pass_on(l + 1, 0, pre)
        x2 = mm_plain(f"mlp2_{tag}", uu, w["w_mlp2"], epilogue=add_res, extras=(x1,), extra_kinds=("full",))
        if l + 1 < DEPTH:
            finish(l + 1, 0, x2)
        saved.append(dict(x0=xl, h1=h1, z5=z5, z2=z2, pa=pa, pb=pb, hl=hl, oa=oa, ob=ob, mg=mg, x1=x1, h2=h2,
                          pre=pre, uu=uu))
        xl = x2

    dx, dxb, d_final_g, loss_cols = loss_head("loss_head", xl, final_g[None], target)
    loss = lax.psum(jnp.sum(loss_cols), ("x", "y", "c"))

    small_grads = {}
    exchanges = {}

    def pair_start(l, gi, grads_g):
        axes = [big_axis[k] for k in groups[gi]]
        sems, grads_g, theirs, tok = pair_exchange_start(f"pair_start_l{l}g{gi}", grads_g, axes)
        exchanges[l, gi] = dict(axes=axes, pair_sems=sems, grads=grads_g, theirs=theirs)
        return tok

    def chip_start(l, gi, after):
        st = exchanges[l, gi]
        grads_g, theirs = pair_exchange_finish(f"pair_finish_l{l}g{gi}", st["grads"], st["theirs"], st["axes"],
                                               st["pair_sems"], after)
        sums = [chip_sum(f"chip_sum_{k}_l{l}", g, a, th, my_core)
                for k, g, a, th in zip(groups[gi], grads_g, st["axes"], theirs)]
        st["chip_sems"], st["sums"], st["recv"], tok = chip_exchange_start(f"chip_start_l{l}g{gi}", sums)
        return tok

    token = None
    for l in reversed(range(DEPTH)):
        p, w, sv = params[l], full[l], saved[l]
        tag = f"l{l}"
        dpre = mm_plain(f"mlp2_dx_{tag}", dxb, w["w_mlp2"], tb=True, out_dtype=BF16, epilogue=relu2_bwd,
                        extras=(sv["pre"],), extra_kinds=("full",), after=token)
        dw2 = mm_plain(f"mlp2_dw_{tag}", sv["uu"], dxb, ta=True, out_dtype=BF16)
        dw1 = mm_plain(f"mlp1_dw_{tag}", sv["h2"], dpre, ta=True, out_dtype=BF16)
        token = pair_start(l, 2, [dw1, dw2])
        dh2 = mm_plain(f"mlp1_dx_{tag}", dpre, w["w_mlp1"], tb=True, after=token)
        dx1, dx1b, dg2 = rmsnorm_bwd(f"norm2_bwd_{tag}", dh2, sv["x1"], p["g2"], dx)
        token = chip_start(l, 2, dx1b)
        dwo = mm_plain(f"proj_o_dw_{tag}", sv["mg"], dx1b, ta=True, out_dtype=BF16, after=token)
        doa, dob, dz2, dbias2 = proj_o_dx_merge_bwd(f"proj_o_dx_merge_bwd_{tag}", dx1b, w["w_o"], sv["z2"], sv["oa"],
                                                    sv["ob"])
        dbias2 = jnp.sum(dbias2, axis=0)
        dpa = mm_plain(f"proj_a_dx_{tag}", doa, w["w_pa"], tb=True)
        dpb = mm_plain(f"proj_b_dx_{tag}", dob, w["w_pb"], tb=True)
        dwpa = mm_plain(f"proj_a_dw_{tag}", sv["pa"], doa, ta=True, out_dtype=BF16)
        dwpb = mm_plain(f"proj_b_dw_{tag}", sv["pb"], dob, ta=True, out_dtype=BF16)
        token = pair_start(l, 1, [dwpa, dwpb, dwo])
        (dz5, dbias5, dcaw, dcab, dcbw, dwr, dwi, dbr, dbi, dlam) = branch_bwd(
            f"branch_bwd_{tag}", sv["z5"], sv["hl"], dpa, dpb, p["caw"], p["cab"], p["cbw"], w["lru_wr"], w["lru_wi"],
            p["br"], p["bi"], p["lam"])
        token = chip_start(l, 1, dz5)
        dwin = mm_in_proj_dw(f"in_proj5_dw_{tag}", sv["h1"], dz5, 0, n_in, after=token)
        dwin = mm_in_proj_dw(f"in_proj2_dw_{tag}", sv["h1"], dz2, N_BRANCH * d, n_in, prev=dwin)
        token = pair_start(l, 0, [dwin, dwr.astype(BF16), dwi.astype(BF16)])
        if l == 0:
            token = chip_start(l, 0, token)
        dh1 = mm_in_proj_dh(f"in_proj5_dx_{tag}", dz5, w["w_in"], 0, after=token)
        if l > 0:
            token = chip_start(l, 0, dh1)
        dh1 = mm_in_proj_dh(f"in_proj2_dx_{tag}", dz2, w["w_in"], N_BRANCH * d, res=dh1, after=token)
        dx, dxb, dg1 = rmsnorm_bwd(f"norm1_bwd_{tag}", dh1, sv["x0"], p["g1"], dx1)
        token = None
        small_grads[l] = dict(norm1_g=dg1, b_in=jnp.concatenate([dbias5.reshape(1, -1), dbias2.reshape(1, -1)], axis=1),
                              conv_a_b=dcab, lru_lam=dlam, norm2_g=dg2, conv_a_w=dcaw, conv_b_w=dcbw,
                              lru_br=dbr, lru_bi=dbi)
    grad_x = dx.reshape(x.shape)


    replicated = ["norm1_g", "b_in", "conv_a_b", "lru_lam", "norm2_g"]
    small_list = [jnp.stack([small_grads[l][k] for l in range(DEPTH)]) for k in replicated + small_sharded]
    small_list.append(d_final_g)
    packed_partial = _pack(small_list)
    partials = all_gather("gather_small_grads", [packed_partial[None]], [0])[0]
    summed = sum_partials("sum_small_grads", partials)
    pieces = _unpack(summed, [a.shape for a in small_list])
    small_sum = dict(zip(replicated + small_sharded + ["final_g"], pieces))

    grads = {}
    for k in replicated:
        grads[k] = small_sum[k].reshape(weights[k].shape)
    grads["final_g"] = small_sum["final_g"].reshape(weights["final_g"].shape)
    for k in small_sharded:
        shard_shape = weights[k].shape
        g = small_sum[k].reshape(shard_shape[0], shard_shape[1], N_DEV, shard_shape[2])
        grads[k] = lax.dynamic_index_in_dim(g, me, axis=2, keepdims=False)

    small_names = replicated + ["final_g"] + small_sharded
    small_shapes = [weights[k].shape for k in small_names]
    pk = lambda src: _pack([src[k] for k in small_names])
    sd, sm, sv2 = adamw_small("adamw_small", pk(grads), pk(weights), pk(mom1), pk(mom2))
    delta = dict(zip(small_names, _unpack(sd, small_shapes)))
    new_m = dict(zip(small_names, _unpack(sm, small_shapes)))
    new_v = dict(zip(small_names, _unpack(sv2, small_shapes)))

    after = dx
    for gi in reversed(range(len(groups))):
        for l in reversed(range(DEPTH)):
            st = exchanges[l, gi]
            st["own"], st["recv"] = chip_exchange_finish(f"chip_finish_l{l}g{gi}", st["sums"], st["recv"],
                                                         st["chip_sems"], after)
        for i, k in enumerate(groups[gi]):
            shape = weights[k].shape
            c = shape[-1]
            flat = lambda a: a.reshape(-1, c)
            per_chip = lambda key: [exchanges[l, gi][key][i].reshape(N_CHIP, -1, c) for l in range(DEPTH)]
            g, dl, m2, v2 = adamw_sharded(f"adamw_{k}", per_chip("recv"), per_chip("own"), my_chip,
                                          flat(weights[k]), flat(mom1[k]), flat(mom2[k]))
            grads[k], delta[k], new_m[k], new_v[k] = (a.reshape(shape) for a in (g, dl, m2, v2))
            after = g

    return (loss, grad_x, *[grads[k] for k in names], *[delta[k] for k in names],
            *[new_m[k] for k in names], *[new_v[k] for k in names])
```
